```python
import math, functools
import jax, jax.numpy as jnp
from jax import lax
import numpy as np

D_MODEL = 2048
BATCH = 4
SEQ = 2048
DEPTH = 1
DEC_BATCH = 128
DEC_SEQ = 1
PAST_LEN = 8192
PAGE_SIZE = 128

N_META = 16
D_SSM = D_MODEL // 2
SSM_GROUP = 16
N_GROUPS = D_SSM // SSM_GROUP
N_STATE = 64
DT_MIN = 1e-3
DT_MAX = 1e-1
D_ATT = D_MODEL - D_SSM
V_HEAD = 128
N_HEADS = D_ATT // V_HEAD
QK_NOPE = 128
QK_ROPE = 64
KV_LORA = 512
Q_LORA = 768
ROPE_THETA = 10000.0
SOFTMAX_SCALE = (QK_NOPE + QK_ROPE) ** -0.5
Q_BLOCK = 128
EPS = 1e-6
SPLIT_IDX = (D_SSM, 2 * D_SSM, 2 * D_SSM + Q_LORA, 2 * D_SSM + Q_LORA + KV_LORA,
             2 * D_SSM + Q_LORA + KV_LORA + QK_ROPE)
D_IN_PROJ = 2 * D_SSM + Q_LORA + KV_LORA + QK_ROPE + D_ATT

kernel_name = "hymba_s5_mla_decode_step"


def rmsnorm(x, g):
    xf = x.astype(jnp.float32)
    y = xf * lax.rsqrt(jnp.mean(xf * xf, axis=-1, keepdims=True) + EPS)
    return (y * g.astype(jnp.float32)).astype(x.dtype)


def rope(x, pos):
    half = QK_ROPE // 2
    inv = ROPE_THETA ** (-jnp.arange(half, dtype=jnp.float32) / half)
    ang = pos.astype(jnp.float32)[:, None] * inv[None, :]
    ang = ang.reshape((ang.shape[0],) + (1,) * (x.ndim - 3) + (half,))
    cos, sin = jnp.cos(ang), jnp.sin(ang)
    xf = x.astype(jnp.float32)
    x1, x2 = xf[..., :half], xf[..., half:]
    return jnp.concatenate([x1 * cos - x2 * sin, x1 * sin + x2 * cos], axis=-1).astype(x.dtype)


def ssm_discretize(a_re, a_im, log_dt, b_re, b_im):
    dt = jnp.exp(log_dt.astype(jnp.float32))[:, None]
    ar, ai = a_re.astype(jnp.float32), a_im.astype(jnp.float32)
    mag = jnp.exp(dt * ar)
    abr, abi = mag * jnp.cos(dt * ai), mag * jnp.sin(dt * ai)
    den = ar * ar + ai * ai
    nr, ni = abr - 1.0, abi
    f_re = (nr * ar + ni * ai) / den
    f_im = (ni * ar - nr * ai) / den
    br, bi = b_re.astype(jnp.float32), b_im.astype(jnp.float32)
    bbr = f_re[..., None] * br - f_im[..., None] * bi
    bbi = f_re[..., None] * bi + f_im[..., None] * br
    return abr, abi, bbr, bbi


def _ssm_combine(e1, e2):
    a1r, a1i, b1r, b1i = e1
    a2r, a2i, b2r, b2i = e2
    return (a1r * a2r - a1i * a2i,
            a1r * a2i + a1i * a2r,
            a2r * b1r - a2i * b1i + b2r,
            a2r * b1i + a2i * b1r + b2i)


def ssm_scan(u, h0r, h0i, abr, abi, bbr, bbi, c_re, c_im):
    bu_r = jnp.einsum('gnc,blgc->blgn', bbr, u)
    bu_i = jnp.einsum('gnc,blgc->blgn', bbi, u)
    bu_r = bu_r.at[:, 0].add(abr * h0r - abi * h0i)
    bu_i = bu_i.at[:, 0].add(abr * h0i + abi * h0r)
    a_r = jnp.broadcast_to(abr, bu_r.shape)
    a_i = jnp.broadcast_to(abi, bu_i.shape)
    _, _, hr, hi = lax.associative_scan(_ssm_combine, (a_r, a_i, bu_r, bu_i), axis=1)
    y = (jnp.einsum('gcn,blgn->blgc', c_re.astype(jnp.float32), hr)
         - jnp.einsum('gcn,blgn->blgc', c_im.astype(jnp.float32), hi))
    return y, hr[:, -1], hi[:, -1]


def mla_prompt_attn(q_abs, q_pe, c, k_pe, w_uv):
    bsz, seq = q_abs.shape[0], q_abs.shape[1]
    nb = -(-seq // Q_BLOCK)
    padded = nb * Q_BLOCK
    pad = ((0, 0), (0, padded - seq), (0, 0), (0, 0))
    qa, qp = jnp.pad(q_abs, pad), jnp.pad(q_pe, pad)
    kpos = jnp.arange(seq)

    def block(i):
        start = i * Q_BLOCK
        qa_b = lax.dynamic_slice_in_dim(qa, start, Q_BLOCK, axis=1)
        qp_b = lax.dynamic_slice_in_dim(qp, start, Q_BLOCK, axis=1)
        sc = (jnp.einsum('bqhr,bkr->bhqk', qa_b, c).astype(jnp.float32)
              + jnp.einsum('bqhe,bke->bhqk', qp_b, k_pe).astype(jnp.float32)) * SOFTMAX_SCALE
        qpos = start + jnp.arange(Q_BLOCK)
        sc = jnp.where(kpos[None, :] <= qpos[:, None], sc, -jnp.inf)
        w = jax.nn.softmax(sc, axis=-1).astype(c.dtype)
        o_lat = jnp.einsum('bhqk,bkr->bqhr', w, c)
        return jnp.einsum('bqhr,rhv->bqhv', o_lat, w_uv)

    o = lax.map(block, jnp.arange(nb))
    o = o.transpose(1, 0, 2, 3, 4).reshape(bsz, padded, N_HEADS, V_HEAD)
    return o[:, :seq]


def mla_sample_attn(q_abs, q_pe, c_new, kpe_new, c_pool, kpe_pool, page_table, w_uv):
    dbs, sq = q_abs.shape[0], q_abs.shape[1]
    c_past = c_pool[page_table].reshape(dbs, -1, KV_LORA)
    kpe_past = kpe_pool[page_table].reshape(dbs, -1, QK_ROPE)
    n_past = c_past.shape[1]
    s_past = (jnp.einsum('bqhr,bkr->bhqk', q_abs, c_past).astype(jnp.float32)
              + jnp.einsum('bqhe,bke->bhqk', q_pe, kpe_past).astype(jnp.float32)) * SOFTMAX_SCALE
    s_new = (jnp.einsum('bqhr,bkr->bhqk', q_abs, c_new).astype(jnp.float32)
             + jnp.einsum('bqhe,bke->bhqk', q_pe, kpe_new).astype(jnp.float32)) * SOFTMAX_SCALE
    causal = jnp.tril(jnp.ones((sq, sq), dtype=bool))
    s_new = jnp.where(causal, s_new, -jnp.inf)
    w = jax.nn.softmax(jnp.concatenate([s_past, s_new], axis=-1), axis=-1).astype(c_new.dtype)
    o_lat = (jnp.einsum('bhqk,bkr->bqhr', w[..., :n_past], c_past)
             + jnp.einsum('bhqk,bkr->bqhr', w[..., n_past:], c_new))
    return jnp.einsum('bqhr,rhv->bqhv', o_lat, w_uv)


def mixer(xn, pos, h0r, h0i, attend, w_in, a_re, a_im, log_dt, b_re, b_im, c_re, c_im,
          d_skip, w_glu, b_glu, g_qn, w_uq, g_kvn, w_uk, w_out):
    bsz, seq, _ = xn.shape
    proj = xn @ w_in
    u, zs, cq, ckv, kpe, za = jnp.split(proj, SPLIT_IDX, axis=-1)
    uf = u.astype(jnp.float32)
    abr, abi, bbr, bbi = ssm_discretize(a_re, a_im, log_dt, b_re, b_im)
    ys, hr, hi = ssm_scan(uf.reshape(bsz, seq, N_GROUPS, SSM_GROUP), h0r.astype(jnp.float32),
                          h0i.astype(jnp.float32), abr, abi, bbr, bbi, c_re, c_im)
    ys = ys.reshape(bsz, seq, D_SSM) + d_skip.astype(jnp.float32) * uf
    ys = jax.nn.gelu(ys).astype(xn.dtype)
    ys = ys * jax.nn.sigmoid(ys @ w_glu + b_glu)
    ssm_out = ys * jax.nn.silu(zs)
    q = (rmsnorm(cq, g_qn) @ w_uq).reshape(bsz, seq, N_HEADS, QK_NOPE + QK_ROPE)
    q_nope, q_pe = q[..., :QK_NOPE], rope(q[..., QK_NOPE:], pos)
    q_abs = jnp.einsum('blhd,rhd->blhr', q_nope, w_uk)
    c = rmsnorm(ckv, g_kvn)
    k_pe = rope(kpe, pos)
    o = attend(q_abs, q_pe, c, k_pe)
    att_out = o.reshape(bsz, seq, D_ATT) * jax.nn.silu(za)
    y = jnp.concatenate([ssm_out, att_out], axis=-1) @ w_out
    return y, c, k_pe, hr, hi


def setup_inputs(seed: int = 0) -> dict:
    key = jax.random.key(seed)
    ks = jax.random.split(key, 28)
    f32 = jnp.float32
    n_pages = PAST_LEN // PAGE_SIZE
    n_pool = (DEC_BATCH * n_pages * 5) // 4

    def nrm(k, shape, std):
        return jax.random.normal(k, shape, f32) * std

    page_table = jax.random.permutation(ks[6], n_pool)[: DEC_BATCH * n_pages]
    page_table = page_table.reshape(DEC_BATCH, n_pages).astype(jnp.int32)
    n_idx = jnp.arange(N_STATE, dtype=f32)
    return {
        'x_prompt': nrm(ks[0], (BATCH, SEQ, D_MODEL), 1.0),
        'x_sample': nrm(ks[1], (DEC_BATCH, DEC_SEQ, D_MODEL), 1.0),
        'cache_ckv': nrm(ks[2], (DEPTH, n_pool, PAGE_SIZE, KV_LORA), 1.0),
        'cache_kpe': nrm(ks[3], (DEPTH, n_pool, PAGE_SIZE, QK_ROPE), 1.0),
        'state_ssm_re': nrm(ks[4], (DEPTH, DEC_BATCH, N_GROUPS, N_STATE), 0.1),
        'state_ssm_im': nrm(ks[5], (DEPTH, DEC_BATCH, N_GROUPS, N_STATE), 0.1),
        'page_table': page_table,
        'meta_tokens': nrm(ks[7], (N_META, D_MODEL), 1.0),
        'g_norm': 1.0 + nrm(ks[8], (DEPTH, D_MODEL), 0.02),
        'w_in': nrm(ks[9], (DEPTH, D_MODEL, D_IN_PROJ), D_MODEL ** -0.5),
        'ssm_a_re': -0.5 + nrm(ks[10], (DEPTH, N_GROUPS, N_STATE), 0.01),
        'ssm_a_im': jnp.pi * n_idx + nrm(ks[11], (DEPTH, N_GROUPS, N_STATE), 0.01),
        'ssm_log_dt': jax.random.uniform(ks[12], (DEPTH, N_GROUPS), f32,
                                         math.log(DT_MIN), math.log(DT_MAX)),
        'ssm_b_re': nrm(ks[13], (DEPTH, N_GROUPS, N_STATE, SSM_GROUP), (2 * SSM_GROUP) ** -0.5),
        'ssm_b_im': nrm(ks[14], (DEPTH, N_GROUPS, N_STATE, SSM_GROUP), (2 * SSM_GROUP) ** -0.5),
        'ssm_c_re': nrm(ks[15], (DEPTH, N_GROUPS, SSM_GROUP, N_STATE), N_STATE ** -0.5),
        'ssm_c_im': nrm(ks[16], (DEPTH, N_GROUPS, SSM_GROUP, N_STATE), N_STATE ** -0.5),
        'ssm_d': nrm(ks[17], (DEPTH, D_SSM), 1.0),
        'w_glu': nrm(ks[18], (DEPTH, D_SSM, D_SSM), D_SSM ** -0.5),
        'b_glu': nrm(ks[19], (DEPTH, D_SSM), 0.02),
        'g_q_norm': 1.0 + nrm(ks[20], (DEPTH, Q_LORA), 0.02),
        'w_uq': nrm(ks[21], (DEPTH, Q_LORA, N_HEADS * (QK_NOPE + QK_ROPE)), Q_LORA ** -0.5),
        'g_kv_norm': 1.0 + nrm(ks[22], (DEPTH, KV_LORA), 0.02),
        'w_uk': nrm(ks[23], (DEPTH, KV_LORA, N_HEADS, QK_NOPE), KV_LORA ** -0.5),
        'w_uv': nrm(ks[24], (DEPTH, KV_LORA, N_HEADS, V_HEAD), KV_LORA ** -0.5),
        'w_out': nrm(ks[25], (DEPTH, D_MODEL, D_MODEL), D_MODEL ** -0.5),
        'g_final': 1.0 + nrm(ks[26], (D_MODEL,), 0.02),
    }


def reference(x_prompt, x_sample, cache_ckv, cache_kpe, state_ssm_re, state_ssm_im, page_table,
              meta_tokens, g_norm, w_in, ssm_a_re, ssm_a_im, ssm_log_dt, ssm_b_re, ssm_b_im,
              ssm_c_re, ssm_c_im, ssm_d, w_glu, b_glu, g_q_norm, w_uq, g_kv_norm, w_uk, w_uv,
              w_out, g_final):
    bsz = x_prompt.shape[0]
    meta = jnp.broadcast_to(meta_tokens.astype(x_prompt.dtype)[None], (bsz, N_META, D_MODEL))
    hp = jnp.concatenate([meta, x_prompt], axis=1)
    pos_p = jnp.arange(hp.shape[1], dtype=jnp.int32)
    hs = x_sample
    past_len = page_table.shape[1] * cache_ckv.shape[2]
    pos_s = past_len + jnp.arange(hs.shape[1], dtype=jnp.int32)
    zero_state = jnp.zeros((bsz, N_GROUPS, N_STATE), jnp.float32)

    ckv_p, kpe_p, sre_p, sim_p = [], [], [], []
    ckv_s, kpe_s, sre_s, sim_s = [], [], [], []
    for l in range(DEPTH):
        lw = (w_in[l], ssm_a_re[l], ssm_a_im[l], ssm_log_dt[l], ssm_b_re[l], ssm_b_im[l],
              ssm_c_re[l], ssm_c_im[l], ssm_d[l], w_glu[l], b_glu[l], g_q_norm[l], w_uq[l],
              g_kv_norm[l], w_uk[l], w_out[l])
        attend_p = functools.partial(mla_prompt_attn, w_uv=w_uv[l])
        yp, cp, kp, hrp, hip = mixer(rmsnorm(hp, g_norm[l]), pos_p, zero_state, zero_state,
                                     attend_p, *lw)
        hp = hp + yp
        attend_s = functools.partial(mla_sample_attn, c_pool=cache_ckv[l], kpe_pool=cache_kpe[l],
                                     page_table=page_table, w_uv=w_uv[l])
        ysm, cs, kse, hrs, his = mixer(rmsnorm(hs, g_norm[l]), pos_s, state_ssm_re[l],
                                       state_ssm_im[l], attend_s, *lw)
        hs = hs + ysm
        ckv_p.append(cp); kpe_p.append(kp)
        sre_p.append(hrp.astype(state_ssm_re.dtype)); sim_p.append(hip.astype(state_ssm_im.dtype))
        ckv_s.append(cs); kpe_s.append(kse)
        sre_s.append(hrs.astype(state_ssm_re.dtype)); sim_s.append(his.astype(state_ssm_im.dtype))

    y_prompt = rmsnorm(hp, g_final)[:, N_META:]
    y_sample = rmsnorm(hs, g_final)
    return (y_prompt, y_sample,
            jnp.stack(ckv_p), jnp.stack(kpe_p), jnp.stack(sre_p), jnp.stack(sim_p),
            jnp.stack(ckv_s), jnp.stack(kpe_s), jnp.stack(sre_s), jnp.stack(sim_s))
```

```python
import functools
import math

import numpy as np
import jax
import jax.numpy as jnp
from jax import lax
from jax.experimental import pallas as pl
from jax.experimental.pallas import tpu as pltpu

F32 = jnp.float32
BF16 = jnp.bfloat16

D_MODEL = 2048
N_META = 16
D_SSM = 1024
SSM_GROUP = 16
N_GROUPS = 64
N_STATE = 64
D_STATE = N_GROUPS * N_STATE
D_ATT = 1024
V_HEAD = 128
N_HEADS = 8
QK_NOPE = 128
QK_ROPE = 64
KV_LORA = 512
Q_LORA = 768
ROPE_THETA = 10000.0
SOFTMAX_SCALE = (QK_NOPE + QK_ROPE) ** -0.5
EPS = 1e-6

LANES = 128
SUBLANES = 8
MIB = 1024 * 1024

C_CQ = 0
C_KPE = 768
C_U = 1024
C_ZS = 2048
C_ZA = 3072
C_CKV = 4096
N_PROJ = 4608
HEAD_Q = 256
N_SLABS = D_STATE // LANES
SLAB_GROUP = 8
SSM_KT = 256
SSM_NT = SSM_KT * N_STATE // SSM_GROUP


def _params(sem, vmem_mib=None):
    return pltpu.CompilerParams(
        dimension_semantics=sem,
        vmem_limit_bytes=None if vmem_mib is None else vmem_mib * MIB)


def _rms(x, g):
    return x * lax.rsqrt(jnp.mean(x * x, axis=-1, keepdims=True) + EPS) * g


def _rope128(x, cos_t, sin_lo, sin_hi):
    return (x * cos_t + pltpu.roll(x, LANES - QK_ROPE // 2, axis=1) * sin_lo
            + pltpu.roll(x, QK_ROPE // 2, axis=1) * sin_hi)


def _rope_table_kernel(pos_ref, inv_ref, cos_ref, slo_ref, shi_ref):
    ang = pos_ref[...] * inv_ref[...]
    c, s = jnp.cos(ang), jnp.sin(ang)
    lane = lax.broadcasted_iota(jnp.int32, ang.shape, 1)
    half = QK_ROPE // 2
    cos_ref[...] = jnp.where(lane < QK_ROPE, c, 0.0)
    slo_ref[...] = jnp.where(lane < half, -s, 0.0)
    shi_ref[...] = jnp.where((lane >= half) & (lane < QK_ROPE), s, 0.0)


def _rope_tables(pos):
    n = pos.shape[0]
    half = QK_ROPE // 2
    inv = ROPE_THETA ** (-(np.arange(LANES) % half).astype(np.float64) / half)
    inv = jnp.asarray(inv.astype(np.float32)).reshape(1, LANES)
    shp = jax.ShapeDtypeStruct((n, LANES), F32)
    return pl.pallas_call(_rope_table_kernel, out_shape=(shp, shp, shp), name="rope_tables")(
        pos.reshape(n, 1).astype(F32), inv)


def _inproj_kernel(x_ref, g_ref, w_ref, o_ref, xn_ref):
    @pl.when(pl.program_id(1) == 0)
    def _():
        xn_ref[...] = _rms(x_ref[...], g_ref[...]).astype(BF16)

    o_ref[...] = jnp.dot(xn_ref[...], w_ref[...], preferred_element_type=F32)


def _inproj(x, g, w, tm, tn):
    m = x.shape[0]
    return pl.pallas_call(
        _inproj_kernel,
        grid=(m // tm, N_PROJ // tn),
        in_specs=[pl.BlockSpec((tm, D_MODEL), lambda i, j: (i, 0)),
                  pl.BlockSpec((1, D_MODEL), lambda i, j: (0, 0)),
                  pl.BlockSpec((D_MODEL, tn), lambda i, j: (0, j))],
        out_specs=pl.BlockSpec((tm, tn), lambda i, j: (i, j)),
        out_shape=jax.ShapeDtypeStruct((m, N_PROJ), F32),
        scratch_shapes=[pltpu.VMEM((tm, D_MODEL), BF16)],
        compiler_params=_params(("parallel", "arbitrary"), 48),
        name="inproj")(x, g, w)


def _kvprep_kernel(ckv_ref, kpe_ref, g_ref, cos_ref, slo_ref, shi_ref, wuk_ref, wuv_ref,
                   c_ref, kr_ref, kcat_ref, v_ref):
    c = _rms(ckv_ref[...], g_ref[...])
    c_ref[...] = c
    cb = c.astype(BF16)
    kr = _rope128(kpe_ref[...], cos_ref[...], slo_ref[...], shi_ref[...])
    kr_ref[...] = kr
    krb = kr.astype(BF16)
    knope = jnp.dot(cb, wuk_ref[...], preferred_element_type=F32).astype(BF16)
    for h in range(N_HEADS):
        kcat_ref[:, HEAD_Q * h:HEAD_Q * h + QK_NOPE] = knope[:, QK_NOPE * h:QK_NOPE * (h + 1)]
        kcat_ref[:, HEAD_Q * h + QK_NOPE:HEAD_Q * (h + 1)] = krb
    v_ref[...] = jnp.dot(cb, wuv_ref[...], preferred_element_type=F32).astype(BF16)


def _kvprep(proj, g, tabs, wuk, wuv, tm):
    m = proj.shape[0]
    nt = tabs[0].shape[0] // tm
    tab = pl.BlockSpec((tm, LANES), lambda i: (i % nt, 0))
    row = lambda w: pl.BlockSpec((tm, w), lambda i: (i, 0))
    return pl.pallas_call(
        _kvprep_kernel,
        grid=(m // tm,),
        in_specs=[pl.BlockSpec((tm, KV_LORA), lambda i: (i, C_CKV // KV_LORA)),
                  pl.BlockSpec((tm, LANES), lambda i: (i, C_KPE // LANES)),
                  pl.BlockSpec((1, KV_LORA), lambda i: (0, 0)),
                  tab, tab, tab,
                  pl.BlockSpec((KV_LORA, D_ATT), lambda i: (0, 0)),
                  pl.BlockSpec((KV_LORA, D_ATT), lambda i: (0, 0))],
        out_specs=(row(KV_LORA), row(LANES), row(N_HEADS * HEAD_Q), row(D_ATT)),
        out_shape=(jax.ShapeDtypeStruct((m, KV_LORA), F32),
                   jax.ShapeDtypeStruct((m, LANES), F32),
                   jax.ShapeDtypeStruct((m, N_HEADS * HEAD_Q), BF16),
                   jax.ShapeDtypeStruct((m, D_ATT), BF16)),
        compiler_params=_params(("parallel",)),
        name="kvprep")(proj, proj, g, *tabs, wuk, wuv)


def _qproj_kernel(cq_ref, g_ref, w_ref, cos_ref, slo_ref, shi_ref, q_ref):
    cqn = _rms(cq_ref[...], g_ref[...]).astype(BF16)
    q = jnp.dot(cqn, w_ref[...], preferred_element_type=F32)
    cos_t, slo, shi = cos_ref[...], slo_ref[...], shi_ref[...]
    for h in range(N_HEADS):
        lo = HEAD_Q * h
        q_ref[:, lo:lo + QK_NOPE] = (q[:, lo:lo + QK_NOPE] * SOFTMAX_SCALE).astype(BF16)
        pe = _rope128(q[:, lo + QK_NOPE:lo + HEAD_Q], cos_t, slo, shi)
        q_ref[:, lo + QK_NOPE:lo + HEAD_Q] = (pe * SOFTMAX_SCALE).astype(BF16)


def _qproj(proj, g, w, tabs, tm):
    m = proj.shape[0]
    nt = tabs[0].shape[0] // tm
    tab = pl.BlockSpec((tm, LANES), lambda i: (i % nt, 0))
    return pl.pallas_call(
        _qproj_kernel,
        grid=(m // tm,),
        in_specs=[pl.BlockSpec((tm, Q_LORA), lambda i: (i, C_CQ // Q_LORA)),
                  pl.BlockSpec((1, Q_LORA), lambda i: (0, 0)),
                  pl.BlockSpec((Q_LORA, N_HEADS * HEAD_Q), lambda i: (0, 0)),
                  tab, tab, tab],
        out_specs=pl.BlockSpec((tm, N_HEADS * HEAD_Q), lambda i: (i, 0)),
        out_shape=jax.ShapeDtypeStruct((m, N_HEADS * HEAD_Q), BF16),
        compiler_params=_params(("parallel",)),
        name="qproj")(proj, g, w, *tabs)


def _ssm_prep_kernel(are_ref, aim_ref, ldt_ref, bre_ref, bim_ref, abr_ref, abi_ref, bbr_ref, bbi_ref):
    dt = jnp.exp(ldt_ref[...])
    ar, ai = are_ref[...], aim_ref[...]
    mag = jnp.exp(dt * ar)
    abr, abi = mag * jnp.cos(dt * ai), mag * jnp.sin(dt * ai)
    den = ar * ar + ai * ai
    nr, ni = abr - 1.0, abi
    f_re = (nr * ar + ni * ai) / den
    f_im = (ni * ar - nr * ai) / den
    br, bi = bre_ref[...], bim_ref[...]
    abr_ref[...] = abr
    abi_ref[...] = abi
    bbr_ref[...] = f_re * br - f_im * bi
    bbi_ref[...] = f_re * bi + f_im * br


def _ssm_prep(a_re, a_im, log_dt, b_re_t, b_im_t):
    gn = jax.ShapeDtypeStruct((N_GROUPS, 1, N_STATE), F32)
    gcn = jax.ShapeDtypeStruct((N_GROUPS, SSM_GROUP, N_STATE), F32)
    return pl.pallas_call(_ssm_prep_kernel, out_shape=(gn, gn, gcn, gcn), name="ssm_prep")(
        a_re.reshape(N_GROUPS, 1, N_STATE), a_im.reshape(N_GROUPS, 1, N_STATE),
        log_dt.reshape(N_GROUPS, 1, 1), b_re_t, b_im_t)


def _blockdiag_in(b_t):
    x = b_t.reshape(D_SSM // SSM_KT, SSM_KT // SSM_GROUP, SSM_GROUP, N_STATE)
    eye = jnp.eye(SSM_KT // SSM_GROUP, dtype=b_t.dtype)
    out = x[:, :, :, None, :] * eye[None, :, None, :, None]
    return out.reshape(D_SSM // SSM_KT, SSM_KT, SSM_NT)


def _blockdiag_out(c):
    x = c.reshape(D_SSM // SSM_KT, SSM_KT // SSM_GROUP, SSM_GROUP, N_STATE).transpose(0, 1, 3, 2)
    eye = jnp.eye(SSM_KT // SSM_GROUP, dtype=c.dtype)
    out = x[:, :, :, None, :] * eye[None, :, None, :, None]
    return out.reshape(D_SSM // SSM_KT, SSM_NT, SSM_KT)


def _ssm_epilogue(y, uf, zs, d, wglu, bglu):
    ys = jax.nn.gelu(y + d * uf)
    gate = jax.nn.sigmoid(jnp.dot(ys.astype(BF16), wglu, preferred_element_type=F32) + bglu)
    return ys * gate * (zs * jax.nn.sigmoid(zs))


def _s5_scan_kernel(u_ref, zs_ref, h0_ref, abr_ref, abi_ref, bre_ref, bim_ref, crt_ref, cit_ref,
                    d_ref, wglu_ref, bglu_ref, out_ref, hfin_ref, buf_ref, hst_ref, y_ref,
                    *, nb, tc, pitch):
    i = pl.program_id(0)

    @pl.when(i == 0)
    def _():
        hst_ref[...] = h0_ref[...]

    uf = u_ref[...].reshape(nb * tc, D_SSM)
    ub = uf.astype(BF16)
    n_kt = D_SSM // SSM_KT
    slabs_per_kt = SSM_NT // LANES
    for kt in range(n_kt):
        lhs = ub[:, SSM_KT * kt:SSM_KT * (kt + 1)]
        for ri, w_ref in ((0, bre_ref), (1, bim_ref)):
            res = jnp.dot(lhs, w_ref[kt], preferred_element_type=F32)
            for s in range(slabs_per_kt):
                for b in range(nb):
                    r0 = (ri * nb + b) * pitch
                    buf_ref[kt * slabs_per_kt + s, r0:r0 + tc, :] = (
                        res[b * tc:(b + 1) * tc, LANES * s:LANES * (s + 1)])

    row = lax.broadcasted_iota(jnp.int32, (2 * nb, LANES), 0)

    def scan_group(g, carry):
        base = g * SLAB_GROUP
        a1, a2, h = [], [], []
        for s in range(SLAB_GROUP):
            a1.append(jnp.broadcast_to(abr_ref[base + s], (2 * nb, LANES)))
            ai = jnp.broadcast_to(abi_ref[base + s], (2 * nb, LANES))
            a2.append(jnp.where(row < nb, -ai, ai))
            h.append(hst_ref[base + s])
        for t in range(tc):
            for s in range(SLAB_GROUP):
                bu = buf_ref[base + s, pl.ds(t, 2 * nb, stride=pitch), :]
                h[s] = a1[s] * h[s] + a2[s] * pltpu.roll(h[s], nb, axis=0) + bu
                buf_ref[base + s, pl.ds(t, 2 * nb, stride=pitch), :] = h[s]
        for s in range(SLAB_GROUP):
            hst_ref[base + s] = h[s]
        return carry

    lax.fori_loop(0, N_SLABS // SLAB_GROUP, scan_group, 0)

    def states(jt, ri):
        rows = []
        for b in range(nb):
            r0 = (ri * nb + b) * pitch
            rows.append(jnp.concatenate(
                [buf_ref[jt * slabs_per_kt + s, r0:r0 + tc, :] for s in range(slabs_per_kt)], axis=1))
        return jnp.concatenate(rows, axis=0).astype(BF16)

    for jt in range(n_kt):
        y_ref[:, SSM_KT * jt:SSM_KT * (jt + 1)] = (
            jnp.dot(states(jt, 0), crt_ref[jt], preferred_element_type=F32)
            - jnp.dot(states(jt, 1), cit_ref[jt], preferred_element_type=F32))

    zs = zs_ref[...].reshape(nb * tc, D_SSM)
    out = _ssm_epilogue(y_ref[...], uf, zs, d_ref[...], wglu_ref[...], bglu_ref[...])
    out_ref[...] = out.reshape(nb, tc, D_SSM).astype(BF16)

    @pl.when(i == pl.num_programs(0) - 1)
    def _():
        hfin_ref[...] = hst_ref[...]


def _s5_scan(proj3, h0, abr, abi, bre, bim, crt, cit, d, wglu, bglu, tc):
    nb, t_len = proj3.shape[0], proj3.shape[1]
    pitch = tc + SUBLANES
    const = lambda shape: pl.BlockSpec(shape, lambda i: (0,) * len(shape))
    kern = functools.partial(_s5_scan_kernel, nb=nb, tc=tc, pitch=pitch)
    return pl.pallas_call(
        kern,
        grid=(t_len // tc,),
        in_specs=[pl.BlockSpec((nb, tc, D_SSM), lambda i: (0, i, C_U // D_SSM)),
                  pl.BlockSpec((nb, tc, D_SSM), lambda i: (0, i, C_ZS // D_SSM)),
                  const((N_SLABS, 2 * nb, LANES)),
                  const((N_SLABS, 1, LANES)), const((N_SLABS, 1, LANES)),
                  const(bre.shape), const(bim.shape), const(crt.shape), const(cit.shape),
                  const((1, D_SSM)), const((D_SSM, D_SSM)), const((1, D_SSM))],
        out_specs=(pl.BlockSpec((nb, tc, D_SSM), lambda i: (0, i, 0)),
                   const((N_SLABS, 2 * nb, LANES))),
        out_shape=(jax.ShapeDtypeStruct((nb, t_len, D_SSM), BF16),
                   jax.ShapeDtypeStruct((N_SLABS, 2 * nb, LANES), F32)),
        scratch_shapes=[pltpu.VMEM((N_SLABS, 2 * nb * pitch, LANES), F32),
                        pltpu.VMEM((N_SLABS, 2 * nb, LANES), F32),
                        pltpu.VMEM((nb * tc, D_SSM), F32)],
        compiler_params=_params(("arbitrary",), 56),
        name="s5_scan")(proj3, proj3, h0, abr.reshape(N_SLABS, 1, LANES), abi.reshape(N_SLABS, 1, LANES),
                        bre, bim, crt, cit, d, wglu, bglu)


def _s5_step_kernel(u_ref, zs_ref, h0r_ref, h0i_ref, abr_ref, abi_ref, bre_ref, bim_ref, crt_ref,
                    cit_ref, d_ref, wglu_ref, bglu_ref, out_ref, hr_ref, hi_ref, y_ref):
    uf = u_ref[...]
    ub = uf.astype(BF16)
    n_kt = D_SSM // SSM_KT
    for kt in range(n_kt):
        lhs = ub[:, SSM_KT * kt:SSM_KT * (kt + 1)]
        sl = slice(SSM_NT * kt, SSM_NT * (kt + 1))
        ar, ai = abr_ref[:, sl], abi_ref[:, sl]
        h0r, h0i = h0r_ref[:, sl], h0i_ref[:, sl]
        hr_ref[:, sl] = ar * h0r - ai * h0i + jnp.dot(lhs, bre_ref[kt], preferred_element_type=F32)
        hi_ref[:, sl] = ar * h0i + ai * h0r + jnp.dot(lhs, bim_ref[kt], preferred_element_type=F32)
    for jt in range(n_kt):
        sl = slice(SSM_NT * jt, SSM_NT * (jt + 1))
        y_ref[:, SSM_KT * jt:SSM_KT * (jt + 1)] = (
            jnp.dot(hr_ref[:, sl].astype(BF16), crt_ref[jt], preferred_element_type=F32)
            - jnp.dot(hi_ref[:, sl].astype(BF16), cit_ref[jt], preferred_element_type=F32))
    out = _ssm_epilogue(y_ref[...], uf, zs_ref[...], d_ref[...], wglu_ref[...], bglu_ref[...])
    out_ref[...] = out.astype(BF16)


def _s5_step(proj, h0r, h0i, abr, abi, bre, bim, crt, cit, d, wglu, bglu):
    m = h0r.shape[0]
    const = lambda shape: pl.BlockSpec(shape, lambda i: (0,) * len(shape))
    return pl.pallas_call(
        _s5_step_kernel,
        grid=(1,),
        in_specs=[pl.BlockSpec((m, D_SSM), lambda i: (0, C_U // D_SSM)),
                  pl.BlockSpec((m, D_SSM), lambda i: (0, C_ZS // D_SSM)),
                  const((m, D_STATE)), const((m, D_STATE)),
                  const((1, D_STATE)), const((1, D_STATE)),
                  const(bre.shape), const(bim.shape), const(crt.shape), const(cit.shape),
                  const((1, D_SSM)), const((D_SSM, D_SSM)), const((1, D_SSM))],
        out_specs=(const((m, D_SSM)), const((m, D_STATE)), const((m, D_STATE))),
        out_shape=(jax.ShapeDtypeStruct((m, D_SSM), BF16),
                   jax.ShapeDtypeStruct((m, D_STATE), F32),
                   jax.ShapeDtypeStruct((m, D_STATE), F32)),
        scratch_shapes=[pltpu.VMEM((m, D_SSM), F32)],
        compiler_params=_params(("arbitrary",), 48),
        name="s5_step")(proj, proj, h0r, h0i, abr, abi, bre, bim, crt, cit, d, wglu, bglu)


def _attn_kernel(q_ref, k_ref, v_ref, km_ref, vm_ref, za_ref, o_ref, *, tq):
    qi = pl.program_id(2)
    q = q_ref[...]
    nt = (((1,), (1,)), ((), ()))

    s = lax.dot_general(q, km_ref[...], nt, preferred_element_type=F32)
    m = jnp.max(s, axis=-1, keepdims=True)
    p = jnp.exp(s - m)
    l = jnp.sum(p, axis=-1, keepdims=True)
    acc = jnp.dot(p.astype(BF16), vm_ref[...], preferred_element_type=F32)

    def block(j, carry, masked):
        m, l, acc = carry
        off = pl.multiple_of(j * tq, tq)
        k = k_ref[pl.ds(off, tq), :]
        v = v_ref[pl.ds(off, tq), :]
        s = lax.dot_general(q, k, nt, preferred_element_type=F32)
        if masked:
            r = lax.broadcasted_iota(jnp.int32, s.shape, 0)
            c = lax.broadcasted_iota(jnp.int32, s.shape, 1)
            s = jnp.where(c <= r, s, -jnp.inf)
        m_new = jnp.maximum(m, jnp.max(s, axis=-1, keepdims=True))
        alpha = jnp.exp(m - m_new)
        p = jnp.exp(s - m_new)
        l = alpha * l + jnp.sum(p, axis=-1, keepdims=True)
        acc = alpha * acc + jnp.dot(p.astype(BF16), v, preferred_element_type=F32)
        return m_new, l, acc

    carry = lax.fori_loop(0, qi, lambda j, c: block(j, c, False), (m, l, acc))
    m, l, acc = block(qi, carry, True)
    za = za_ref[...]
    o_ref[...] = ((acc / l) * (za * jax.nn.sigmoid(za))).astype(BF16)


def _attention(q, kcat, v, kmeta, vmeta, proj, bsz, seq, tq):
    nq = seq // tq
    kern = functools.partial(_attn_kernel, tq=tq)
    return pl.pallas_call(
        kern,
        grid=(bsz, N_HEADS, nq),
        in_specs=[pl.BlockSpec((tq, HEAD_Q), lambda b, h, i: (b * nq + i, h)),
                  pl.BlockSpec((None, seq, HEAD_Q), lambda b, h, i: (b, 0, h)),
                  pl.BlockSpec((None, seq, V_HEAD), lambda b, h, i: (b, 0, h)),
                  pl.BlockSpec((N_META, HEAD_Q), lambda b, h, i: (0, h)),
                  pl.BlockSpec((N_META, V_HEAD), lambda b, h, i: (0, h)),
                  pl.BlockSpec((tq, V_HEAD), lambda b, h, i: (b * nq + i, C_ZA // V_HEAD + h))],
        out_specs=pl.BlockSpec((tq, V_HEAD), lambda b, h, i: (b * nq + i, h)),
        out_shape=jax.ShapeDtypeStruct((bsz * seq, D_ATT), BF16),
        compiler_params=_params(("parallel", "parallel", "arbitrary")),
        name="attn_prompt")(q, kcat.reshape(bsz, seq, N_HEADS * HEAD_Q), v.reshape(bsz, seq, D_ATT),
                            kmeta, vmeta, proj)


def _qabsorb_kernel(q_ref, wukt_ref, qa_ref):
    for h in range(N_HEADS):
        qn = q_ref[:, HEAD_Q * h:HEAD_Q * h + QK_NOPE]
        qa_ref[:, KV_LORA * h:KV_LORA * (h + 1)] = jnp.dot(
            qn, wukt_ref[h], preferred_element_type=F32).astype(BF16)


def _qabsorb(q, wukt):
    m = q.shape[0]
    return pl.pallas_call(
        _qabsorb_kernel,
        out_shape=jax.ShapeDtypeStruct((m, N_HEADS * KV_LORA), BF16),
        name="q_absorb")(q, wukt)


def _decode_kernel(pt_ref, qa_ref, qp_ref, cn_ref, kn_ref, *rest, npg):
    c_refs, k_refs = rest[:npg], rest[npg:2 * npg]
    o_ref, m_ref, l_ref, acc_ref = rest[2 * npg:]
    j = pl.program_id(1)

    @pl.when(j == 0)
    def _():
        m_ref[...] = jnp.full(m_ref.shape, -jnp.inf, F32)
        l_ref[...] = jnp.zeros(l_ref.shape, F32)
        acc_ref[...] = jnp.zeros(acc_ref.shape, F32)

    qa, qp = qa_ref[...], qp_ref[...]
    c = jnp.concatenate([r[...] for r in c_refs], axis=0).astype(BF16)
    kp = jnp.concatenate([r[...] for r in k_refs], axis=0).astype(BF16)
    nt = (((1,), (1,)), ((), ()))
    s = (lax.dot_general(qa, c, nt, preferred_element_type=F32)
         + lax.dot_general(qp, kp, nt, preferred_element_type=F32))
    m_prev = m_ref[...]
    m_new = jnp.maximum(m_prev, jnp.max(s, axis=-1, keepdims=True))
    alpha = jnp.exp(m_prev - m_new)
    p = jnp.exp(s - m_new)
    l_ref[...] = alpha * l_ref[...] + jnp.sum(p, axis=-1, keepdims=True)
    acc_ref[...] = alpha * acc_ref[...] + jnp.dot(p.astype(BF16), c, preferred_element_type=F32)
    m_ref[...] = m_new

    @pl.when(j == pl.num_programs(1) - 1)
    def _():
        cn, kn = cn_ref[...], kn_ref[...]
        s_new = (jnp.sum(qa.astype(F32) * cn, axis=-1, keepdims=True)
                 + jnp.sum(qp.astype(F32) * kn, axis=-1, keepdims=True))
        m_old = m_ref[...]
        m_fin = jnp.maximum(m_old, s_new)
        a = jnp.exp(m_old - m_fin)
        pn = jnp.exp(s_new - m_fin)
        l_fin = a * l_ref[...] + pn
        o_ref[...] = (a * acc_ref[...] + pn * cn) / l_fin


def _decode(page_table, qa, qp, c_new, k_new, cache_ckv, cache_kpe, npg):
    dbs, n_pages = page_table.shape
    page = cache_ckv.shape[2]
    page_specs = (
        [pl.BlockSpec((None, None, page, KV_LORA),
                      functools.partial(lambda b, j, pt, k: (0, pt[b * n_pages + j * npg + k], 0, 0), k=k))
         for k in range(npg)]
        + [pl.BlockSpec((None, None, page, QK_ROPE),
                        functools.partial(lambda b, j, pt, k: (0, pt[b * n_pages + j * npg + k], 0, 0), k=k))
           for k in range(npg)])
    per_b = lambda shape: pl.BlockSpec((None,) + shape, lambda b, j, pt: (b, 0, 0))
    grid_spec = pltpu.PrefetchScalarGridSpec(
        num_scalar_prefetch=1,
        grid=(dbs, n_pages // npg),
        in_specs=[per_b((N_HEADS, KV_LORA)), per_b((N_HEADS, QK_ROPE)),
                  per_b((1, KV_LORA)), per_b((1, QK_ROPE))] + page_specs,
        out_specs=per_b((N_HEADS, KV_LORA)),
        scratch_shapes=[pltpu.VMEM((N_HEADS, 1), F32), pltpu.VMEM((N_HEADS, 1), F32),
                        pltpu.VMEM((N_HEADS, KV_LORA), F32)])
    return pl.pallas_call(
        functools.partial(_decode_kernel, npg=npg),
        grid_spec=grid_spec,
        out_shape=jax.ShapeDtypeStruct((dbs, N_HEADS, KV_LORA), F32),
        compiler_params=_params(("parallel", "arbitrary"), 48),
        name="attn_decode")(page_table.reshape(-1), qa, qp, c_new, k_new,
                            *([cache_ckv] * npg), *([cache_kpe] * npg))


def _ouv_kernel(ol_ref, wuv_ref, za_ref, o_ref):
    za = za_ref[...]
    gate = za * jax.nn.sigmoid(za)
    for h in range(N_HEADS):
        o = jnp.dot(ol_ref[:, KV_LORA * h:KV_LORA * (h + 1)].astype(BF16),
                    wuv_ref[:, V_HEAD * h:V_HEAD * (h + 1)], preferred_element_type=F32)
        o_ref[:, V_HEAD * h:V_HEAD * (h + 1)] = (o * gate[:, V_HEAD * h:V_HEAD * (h + 1)]).astype(BF16)


def _ouv(o_lat, wuv, proj):
    m = o_lat.shape[0]
    const = lambda shape: pl.BlockSpec(shape, lambda i: (0,) * len(shape))
    return pl.pallas_call(
        _ouv_kernel,
        grid=(1,),
        in_specs=[const((m, N_HEADS * KV_LORA)), const((KV_LORA, D_ATT)),
                  pl.BlockSpec((m, D_ATT), lambda i: (0, C_ZA // D_ATT))],
        out_specs=const((m, D_ATT)),
        out_shape=jax.ShapeDtypeStruct((m, D_ATT), BF16),
        name="o_uv")(o_lat, wuv, proj)


def _outproj_kernel(ssm_ref, att_ref, wtop_ref, wbot_ref, x_ref, g_ref, y_ref):
    h = (x_ref[...]
         + jnp.dot(ssm_ref[...], wtop_ref[...], preferred_element_type=F32)
         + jnp.dot(att_ref[...], wbot_ref[...], preferred_element_type=F32))
    y_ref[...] = _rms(h, g_ref[...])


def _outproj(ssm, att, w, x, g, tm):
    m = x.shape[0]
    return pl.pallas_call(
        _outproj_kernel,
        grid=(m // tm,),
        in_specs=[pl.BlockSpec((tm, D_SSM), lambda i: (i, 0)),
                  pl.BlockSpec((tm, D_ATT), lambda i: (i, 0)),
                  pl.BlockSpec((D_SSM, D_MODEL), lambda i: (0, 0)),
                  pl.BlockSpec((D_ATT, D_MODEL), lambda i: (1, 0)),
                  pl.BlockSpec((tm, D_MODEL), lambda i: (i, 0)),
                  pl.BlockSpec((1, D_MODEL), lambda i: (0, 0))],
        out_specs=pl.BlockSpec((tm, D_MODEL), lambda i: (i, 0)),
        out_shape=jax.ShapeDtypeStruct((m, D_MODEL), F32),
        compiler_params=_params(("parallel",), 48),
        name="outproj")(ssm, att, w, w, x, g)


def _slab_major(h):
    return h.reshape(h.shape[0], N_SLABS, LANES).transpose(1, 0, 2)


def _slab_minor(h):
    return h.transpose(1, 0, 2).reshape(h.shape[1], D_STATE)


def kernel(x_prompt, x_sample, cache_ckv, cache_kpe, state_ssm_re, state_ssm_im, page_table, meta_tokens, g_norm, w_in, ssm_a_re, ssm_a_im, ssm_log_dt, ssm_b_re, ssm_b_im, ssm_c_re, ssm_c_im, ssm_d, w_glu, b_glu, g_q_norm, w_uq, g_kv_norm, w_uk, w_uv, w_out, g_final):
    bsz, seq, _ = x_prompt.shape
    dbs = x_sample.shape[0]
    n_pages, page = page_table.shape[1], cache_ckv.shape[2]
    past_len = n_pages * page
    layer = 0

    wi = w_in[layer]
    o_cq, o_ckv, o_kpe, o_za = 2 * D_SSM, 2 * D_SSM + Q_LORA, 2 * D_SSM + Q_LORA + KV_LORA, \
        2 * D_SSM + Q_LORA + KV_LORA + QK_ROPE
    w_in_p = jnp.concatenate(
        [wi[:, o_cq:o_ckv], wi[:, o_kpe:o_za], jnp.zeros((D_MODEL, C_U - C_KPE - QK_ROPE), wi.dtype),
         wi[:, :D_SSM], wi[:, D_SSM:o_cq], wi[:, o_za:], wi[:, o_ckv:o_kpe]], axis=1).astype(BF16)
    w_uq_p = jnp.pad(w_uq[layer].reshape(Q_LORA, N_HEADS, QK_NOPE + QK_ROPE),
                     ((0, 0), (0, 0), (0, HEAD_Q - QK_NOPE - QK_ROPE))
                     ).reshape(Q_LORA, N_HEADS * HEAD_Q).astype(BF16)
    w_uk_flat = w_uk[layer].reshape(KV_LORA, D_ATT).astype(BF16)
    w_uk_t = w_uk[layer].transpose(1, 2, 0).astype(BF16)
    w_uv_flat = w_uv[layer].reshape(KV_LORA, D_ATT).astype(BF16)
    w_glu_b = w_glu[layer].astype(BF16)
    w_out_b = w_out[layer].astype(BF16)
    g_in = g_norm[layer].reshape(1, D_MODEL)
    g_q = g_q_norm[layer].reshape(1, Q_LORA)
    g_kv = g_kv_norm[layer].reshape(1, KV_LORA)
    g_fin = g_final.reshape(1, D_MODEL)
    d_skip = ssm_d[layer].reshape(1, D_SSM)
    bglu = b_glu[layer].reshape(1, D_SSM)

    abr, abi, bbr_t, bbi_t = _ssm_prep(ssm_a_re[layer], ssm_a_im[layer], ssm_log_dt[layer],
                                       ssm_b_re[layer].transpose(0, 2, 1), ssm_b_im[layer].transpose(0, 2, 1))
    bre, bim = _blockdiag_in(bbr_t).astype(BF16), _blockdiag_in(bbi_t).astype(BF16)
    crt, cit = _blockdiag_out(ssm_c_re[layer]).astype(BF16), _blockdiag_out(ssm_c_im[layer]).astype(BF16)
    abr_f, abi_f = abr.reshape(1, D_STATE), abi.reshape(1, D_STATE)
    ssm_w = (bre, bim, crt, cit, d_skip, w_glu_b, bglu)

    pos = jnp.concatenate([jnp.arange(N_META, N_META + seq), jnp.full((dbs,), past_len), jnp.arange(N_META)])
    tabs = _rope_tables(pos)
    tabs_real = tuple(t[:seq] for t in tabs)
    tabs_small = tuple(t[seq:] for t in tabs)

    x_real = x_prompt.reshape(bsz * seq, D_MODEL)
    x_small = jnp.concatenate([x_sample.reshape(dbs, D_MODEL), meta_tokens.astype(x_prompt.dtype)], axis=0)
    n_small = dbs + N_META
    proj_real = _inproj(x_real, g_in, w_in_p, 512, 768)
    proj_small = _inproj(x_small, g_in, w_in_p, n_small, 768)

    c_real, kr_real, kcat_real, v_real = _kvprep(proj_real, g_kv, tabs_real, w_uk_flat, w_uv_flat, 512)
    c_small, kr_small, kcat_small, v_small = _kvprep(proj_small, g_kv, tabs_small, w_uk_flat, w_uv_flat, n_small)
    q_real = _qproj(proj_real, g_q, w_uq_p, tabs_real, 512)
    q_small = _qproj(proj_small, g_q, w_uq_p, tabs_small, n_small)

    proj_meta = jnp.broadcast_to(proj_small[dbs:][None], (bsz, N_META, N_PROJ))
    h_zero = jnp.zeros((N_SLABS, 2 * bsz, LANES), F32)
    _, h_meta = _s5_scan(proj_meta, h_zero, abr_f, abi_f, *ssm_w, tc=N_META)
    ssm_real, h_fin = _s5_scan(proj_real.reshape(bsz, seq, N_PROJ), h_meta, abr_f, abi_f, *ssm_w, tc=64)
    h_fin = _slab_minor(h_fin)
    ssm_smp, hr_s, hi_s = _s5_step(proj_small, state_ssm_re[layer].reshape(dbs, D_STATE),
                                   state_ssm_im[layer].reshape(dbs, D_STATE), abr_f, abi_f, *ssm_w)

    att_real = _attention(q_real, kcat_real, v_real, kcat_small[dbs:], v_small[dbs:], proj_real, bsz, seq, 256)

    qs = q_small[:dbs]
    qa = _qabsorb(qs, w_uk_t).reshape(dbs, N_HEADS, KV_LORA)
    qp = qs.reshape(dbs, N_HEADS, HEAD_Q)[:, :, QK_NOPE:QK_NOPE + QK_ROPE]
    c_new = c_small[:dbs].reshape(dbs, 1, KV_LORA)
    k_new = kr_small[:dbs, :QK_ROPE].reshape(dbs, 1, QK_ROPE)
    o_lat = _decode(page_table, qa, qp, c_new, k_new, cache_ckv, cache_kpe, 8)
    att_smp = _ouv(o_lat.reshape(dbs, N_HEADS * KV_LORA), w_uv_flat, proj_small)

    y_prompt = _outproj(ssm_real.reshape(bsz * seq, D_SSM), att_real, w_out_b, x_real, g_fin, 256)
    y_sample = _outproj(ssm_smp, att_smp, w_out_b, x_sample.reshape(dbs, D_MODEL), g_fin, dbs)

    c_meta, k_meta = c_small[dbs:], kr_small[dbs:, :QK_ROPE]
    ckv_p = jnp.concatenate([jnp.broadcast_to(c_meta[None], (bsz, N_META, KV_LORA)),
                             c_real.reshape(bsz, seq, KV_LORA)], axis=1)
    kpe_p = jnp.concatenate([jnp.broadcast_to(k_meta[None], (bsz, N_META, QK_ROPE)),
                             kr_real[:, :QK_ROPE].reshape(bsz, seq, QK_ROPE)], axis=1)
    return (y_prompt.reshape(bsz, seq, D_MODEL),
            y_sample.reshape(dbs, 1, D_MODEL),
            ckv_p[None], kpe_p[None],
            h_fin[:bsz].reshape(1, bsz, N_GROUPS, N_STATE),
            h_fin[bsz:].reshape(1, bsz, N_GROUPS, N_STATE),
            c_small[:dbs].reshape(1, dbs, 1, KV_LORA),
            kr_small[:dbs, :QK_ROPE].reshape(1, dbs, 1, QK_ROPE),
            hr_s.reshape(1, dbs, N_GROUPS, N_STATE),
            hi_s.reshape(1, dbs, N_GROUPS, N_STATE))
```

```python
import functools
import math

import numpy as np
import jax
import jax.numpy as jnp
from jax import lax
from jax.experimental import pallas as pl
from jax.experimental.pallas import tpu as pltpu

F32 = jnp.float32
BF16 = jnp.bfloat16

D_MODEL = 2048
N_META = 16
D_SSM = 1024
SSM_GROUP = 16
N_GROUPS = 64
N_STATE = 64
D_STATE = N_GROUPS * N_STATE
D_ATT = 1024
V_HEAD = 128
N_HEADS = 8
QK_NOPE = 128
QK_ROPE = 64
KV_LORA = 512
Q_LORA = 768
ROPE_THETA = 10000.0
SOFTMAX_SCALE = (QK_NOPE + QK_ROPE) ** -0.5
Q_SCALE = SOFTMAX_SCALE * math.log2(math.e)
EPS = 1e-6

LANES = 128
SUBLANES = 8
MIB = 1024 * 1024

C_CQ = 0
C_KPE = 768
C_U = 1024
C_ZS = 2048
C_ZA = 3072
C_CKV = 4096
N_PROJ = 4608
HEAD_Q = 256
N_SLABS = D_STATE // LANES
SLAB_GROUP = 8
ATT_TILE = 256
SSM_KT = 256
SSM_NT = SSM_KT * N_STATE // SSM_GROUP


def _params(sem, vmem_mib=None):
    return pltpu.CompilerParams(
        dimension_semantics=sem,
        vmem_limit_bytes=None if vmem_mib is None else vmem_mib * MIB)


def _rms(x, g):
    return x * lax.rsqrt(jnp.mean(x * x, axis=-1, keepdims=True) + EPS) * g


def _rope128(x, cos_t, sin_lo, sin_hi):
    return (x * cos_t + pltpu.roll(x, LANES - QK_ROPE // 2, axis=1) * sin_lo
            + pltpu.roll(x, QK_ROPE // 2, axis=1) * sin_hi)


def _rope_table_kernel(pos_ref, inv_ref, cos_ref, slo_ref, shi_ref):
    ang = pos_ref[...] * inv_ref[...]
    c, s = jnp.cos(ang), jnp.sin(ang)
    lane = lax.broadcasted_iota(jnp.int32, ang.shape, 1)
    half = QK_ROPE // 2
    cos_ref[...] = jnp.where(lane < QK_ROPE, c, 0.0)
    slo_ref[...] = jnp.where(lane < half, -s, 0.0)
    shi_ref[...] = jnp.where((lane >= half) & (lane < QK_ROPE), s, 0.0)


def _rope_tables(pos):
    n = pos.shape[0]
    half = QK_ROPE // 2
    inv = ROPE_THETA ** (-(np.arange(LANES) % half).astype(np.float64) / half)
    inv = jnp.asarray(inv.astype(np.float32)).reshape(1, LANES)
    shp = jax.ShapeDtypeStruct((n, LANES), F32)
    return pl.pallas_call(_rope_table_kernel, out_shape=(shp, shp, shp), name="rope_tables")(
        pos.reshape(n, 1).astype(F32), inv)


def _inproj_kernel(x_ref, g_ref, w_ref, o_ref, xn_ref):
    @pl.when(pl.program_id(1) == 0)
    def _():
        xn_ref[...] = _rms(x_ref[...], g_ref[...]).astype(BF16)

    o_ref[...] = jnp.dot(xn_ref[...], w_ref[...], preferred_element_type=F32)


def _inproj(x, g, w, tm, tn):
    m = x.shape[0]
    return pl.pallas_call(
        _inproj_kernel,
        grid=(m // tm, N_PROJ // tn),
        in_specs=[pl.BlockSpec((tm, D_MODEL), lambda i, j: (i, 0)),
                  pl.BlockSpec((1, D_MODEL), lambda i, j: (0, 0)),
                  pl.BlockSpec((D_MODEL, tn), lambda i, j: (0, j))],
        out_specs=pl.BlockSpec((tm, tn), lambda i, j: (i, j)),
        out_shape=jax.ShapeDtypeStruct((m, N_PROJ), F32),
        scratch_shapes=[pltpu.VMEM((tm, D_MODEL), BF16)],
        compiler_params=_params(("parallel", "arbitrary"), 48),
        name="inproj")(x, g, w)


def _kvprep_kernel(ckv_ref, kpe_ref, g_ref, cos_ref, slo_ref, shi_ref, wuk_ref, wuvt_ref,
                   c_ref, kr_ref, kcat_ref, vt_ref, *, tk):
    c = _rms(ckv_ref[...], g_ref[...])
    c_ref[...] = c
    cb = c.astype(BF16)
    kr = _rope128(kpe_ref[...], cos_ref[...], slo_ref[...], shi_ref[...])
    kr_ref[...] = kr
    krb = kr.astype(BF16)
    knope = jnp.dot(cb, wuk_ref[...], preferred_element_type=F32).astype(BF16)
    for h in range(N_HEADS):
        kcat_ref[:, HEAD_Q * h:HEAD_Q * h + QK_NOPE] = knope[:, QK_NOPE * h:QK_NOPE * (h + 1)]
        kcat_ref[:, HEAD_Q * h + QK_NOPE:HEAD_Q * (h + 1)] = krb
    vt = lax.dot_general(wuvt_ref[...], cb, (((1,), (1,)), ((), ())),
                         preferred_element_type=F32).astype(BF16)
    for kb in range(vt_ref.shape[0]):
        vt_ref[kb] = vt[:, tk * kb:tk * (kb + 1)]


def _kvprep(proj, g, tabs, wuk, wuvt, tm, tk):
    m = proj.shape[0]
    nt = tabs[0].shape[0] // tm
    tab = pl.BlockSpec((tm, LANES), lambda i: (i % nt, 0))
    row = lambda w: pl.BlockSpec((tm, w), lambda i: (i, 0))
    return pl.pallas_call(
        functools.partial(_kvprep_kernel, tk=tk),
        grid=(m // tm,),
        in_specs=[pl.BlockSpec((tm, KV_LORA), lambda i: (i, C_CKV // KV_LORA)),
                  pl.BlockSpec((tm, LANES), lambda i: (i, C_KPE // LANES)),
                  pl.BlockSpec((1, KV_LORA), lambda i: (0, 0)),
                  tab, tab, tab,
                  pl.BlockSpec((KV_LORA, D_ATT), lambda i: (0, 0)),
                  pl.BlockSpec((D_ATT, KV_LORA), lambda i: (0, 0))],
        out_specs=(row(KV_LORA), row(LANES), row(N_HEADS * HEAD_Q),
                   pl.BlockSpec((tm // tk, D_ATT, tk), lambda i: (i, 0, 0))),
        out_shape=(jax.ShapeDtypeStruct((m, KV_LORA), F32),
                   jax.ShapeDtypeStruct((m, LANES), F32),
                   jax.ShapeDtypeStruct((m, N_HEADS * HEAD_Q), BF16),
                   jax.ShapeDtypeStruct((m // tk, D_ATT, tk), BF16)),
        compiler_params=_params(("parallel",)),
        name="kvprep")(proj, proj, g, *tabs, wuk, wuvt)


def _qproj_kernel(cq_ref, g_ref, w_ref, cos_ref, slo_ref, shi_ref, q_ref):
    cqn = _rms(cq_ref[...], g_ref[...]).astype(BF16)
    q = jnp.dot(cqn, w_ref[...], preferred_element_type=F32)
    cos_t, slo, shi = cos_ref[...], slo_ref[...], shi_ref[...]
    for h in range(N_HEADS):
        lo = HEAD_Q * h
        q_ref[:, lo:lo + QK_NOPE] = (q[:, lo:lo + QK_NOPE] * Q_SCALE).astype(BF16)
        pe = _rope128(q[:, lo + QK_NOPE:lo + HEAD_Q], cos_t, slo, shi)
        q_ref[:, lo + QK_NOPE:lo + HEAD_Q] = (pe * Q_SCALE).astype(BF16)


def _qproj(proj, g, w, tabs, tm):
    m = proj.shape[0]
    nt = tabs[0].shape[0] // tm
    tab = pl.BlockSpec((tm, LANES), lambda i: (i % nt, 0))
    return pl.pallas_call(
        _qproj_kernel,
        grid=(m // tm,),
        in_specs=[pl.BlockSpec((tm, Q_LORA), lambda i: (i, C_CQ // Q_LORA)),
                  pl.BlockSpec((1, Q_LORA), lambda i: (0, 0)),
                  pl.BlockSpec((Q_LORA, N_HEADS * HEAD_Q), lambda i: (0, 0)),
                  tab, tab, tab],
        out_specs=pl.BlockSpec((tm, N_HEADS * HEAD_Q), lambda i: (i, 0)),
        out_shape=jax.ShapeDtypeStruct((m, N_HEADS * HEAD_Q), BF16),
        compiler_params=_params(("parallel",)),
        name="qproj")(proj, g, w, *tabs)


def _ssm_prep_kernel(are_ref, aim_ref, ldt_ref, bre_ref, bim_ref, abr_ref, abi_ref, bbr_ref, bbi_ref):
    dt = jnp.exp(ldt_ref[...])
    ar, ai = are_ref[...], aim_ref[...]
    mag = jnp.exp(dt * ar)
    abr, abi = mag * jnp.cos(dt * ai), mag * jnp.sin(dt * ai)
    den = ar * ar + ai * ai
    nr, ni = abr - 1.0, abi
    f_re = (nr * ar + ni * ai) / den
    f_im = (ni * ar - nr * ai) / den
    br, bi = bre_ref[...], bim_ref[...]
    abr_ref[...] = abr
    abi_ref[...] = abi
    bbr_ref[...] = f_re * br - f_im * bi
    bbi_ref[...] = f_re * bi + f_im * br


def _ssm_prep(a_re, a_im, log_dt, b_re_t, b_im_t):
    gn = jax.ShapeDtypeStruct((N_GROUPS, 1, N_STATE), F32)
    gcn = jax.ShapeDtypeStruct((N_GROUPS, SSM_GROUP, N_STATE), F32)
    return pl.pallas_call(_ssm_prep_kernel, out_shape=(gn, gn, gcn, gcn), name="ssm_prep")(
        a_re.reshape(N_GROUPS, 1, N_STATE), a_im.reshape(N_GROUPS, 1, N_STATE),
        log_dt.reshape(N_GROUPS, 1, 1), b_re_t, b_im_t)


def _blockdiag_in(b_t):
    x = b_t.reshape(D_SSM // SSM_KT, SSM_KT // SSM_GROUP, SSM_GROUP, N_STATE)
    eye = jnp.eye(SSM_KT // SSM_GROUP, dtype=b_t.dtype)
    out = x[:, :, :, None, :] * eye[None, :, None, :, None]
    return out.reshape(D_SSM // SSM_KT, SSM_KT, SSM_NT)


def _blockdiag_out(c):
    x = c.reshape(D_SSM // SSM_KT, SSM_KT // SSM_GROUP, SSM_GROUP, N_STATE).transpose(0, 1, 3, 2)
    eye = jnp.eye(SSM_KT // SSM_GROUP, dtype=c.dtype)
    out = x[:, :, :, None, :] * eye[None, :, None, :, None]
    return out.reshape(D_SSM // SSM_KT, SSM_NT, SSM_KT)


def _ssm_epilogue(y, uf, zs, d, wglu, bglu):
    ys = jax.nn.gelu(y + d * uf)
    gate = jax.nn.sigmoid(jnp.dot(ys.astype(BF16), wglu, preferred_element_type=F32) + bglu)
    return ys * gate * (zs * jax.nn.sigmoid(zs))


def _s5_scan_kernel(u_ref, zs_ref, h0_ref, abr_ref, abi_ref, bre_ref, bim_ref, crt_ref, cit_ref,
                    d_ref, wglu_ref, bglu_ref, out_ref, hfin_ref, buf_ref, hst_ref, y_ref,
                    *, nb, tc, pitch):
    i = pl.program_id(0)

    @pl.when(i == 0)
    def _():
        hst_ref[...] = h0_ref[...]

    uf = u_ref[...].reshape(nb * tc, D_SSM)
    ub = uf.astype(BF16)
    n_kt = D_SSM // SSM_KT
    slabs_per_kt = SSM_NT // LANES
    for kt in range(n_kt):
        lhs = ub[:, SSM_KT * kt:SSM_KT * (kt + 1)]
        for ri, w_ref in ((0, bre_ref), (1, bim_ref)):
            res = jnp.dot(lhs, w_ref[kt], preferred_element_type=F32)
            for s in range(slabs_per_kt):
                for b in range(nb):
                    r0 = (ri * nb + b) * pitch
                    buf_ref[kt * slabs_per_kt + s, r0:r0 + tc, :] = (
                        res[b * tc:(b + 1) * tc, LANES * s:LANES * (s + 1)])

    row = lax.broadcasted_iota(jnp.int32, (2 * nb, LANES), 0)

    def scan_group(g, carry):
        base = g * SLAB_GROUP
        a1, a2, h = [], [], []
        for s in range(SLAB_GROUP):
            a1.append(jnp.broadcast_to(abr_ref[base + s], (2 * nb, LANES)))
            ai = jnp.broadcast_to(abi_ref[base + s], (2 * nb, LANES))
            a2.append(jnp.where(row < nb, -ai, ai))
            h.append(hst_ref[base + s])
        for t in range(tc):
            for s in range(SLAB_GROUP):
                bu = buf_ref[base + s, pl.ds(t, 2 * nb, stride=pitch), :]
                h[s] = a1[s] * h[s] + a2[s] * pltpu.roll(h[s], nb, axis=0) + bu
                buf_ref[base + s, pl.ds(t, 2 * nb, stride=pitch), :] = h[s]
        for s in range(SLAB_GROUP):
            hst_ref[base + s] = h[s]
        return carry

    lax.fori_loop(0, N_SLABS // SLAB_GROUP, scan_group, 0)

    def states(jt, ri):
        rows = []
        for b in range(nb):
            r0 = (ri * nb + b) * pitch
            rows.append(jnp.concatenate(
                [buf_ref[jt * slabs_per_kt + s, r0:r0 + tc, :] for s in range(slabs_per_kt)], axis=1))
        return jnp.concatenate(rows, axis=0).astype(BF16)

    for jt in range(n_kt):
        y_ref[:, SSM_KT * jt:SSM_KT * (jt + 1)] = (
            jnp.dot(states(jt, 0), crt_ref[jt], preferred_element_type=F32)
            - jnp.dot(states(jt, 1), cit_ref[jt], preferred_element_type=F32))

    zs = zs_ref[...].reshape(nb * tc, D_SSM)
    out = _ssm_epilogue(y_ref[...], uf, zs, d_ref[...], wglu_ref[...], bglu_ref[...])
    out_ref[...] = out.reshape(nb, tc, D_SSM).astype(BF16)

    @pl.when(i == pl.num_programs(0) - 1)
    def _():
        hfin_ref[...] = hst_ref[...]


def _s5_scan(proj3, h0, abr, abi, bre, bim, crt, cit, d, wglu, bglu, tc):
    nb, t_len = proj3.shape[0], proj3.shape[1]
    pitch = tc + SUBLANES
    const = lambda shape: pl.BlockSpec(shape, lambda i: (0,) * len(shape))
    kern = functools.partial(_s5_scan_kernel, nb=nb, tc=tc, pitch=pitch)
    return pl.pallas_call(
        kern,
        grid=(t_len // tc,),
        in_specs=[pl.BlockSpec((nb, tc, D_SSM), lambda i: (0, i, C_U // D_SSM)),
                  pl.BlockSpec((nb, tc, D_SSM), lambda i: (0, i, C_ZS // D_SSM)),
                  const((N_SLABS, 2 * nb, LANES)),
                  const((N_SLABS, 1, LANES)), const((N_SLABS, 1, LANES)),
                  const(bre.shape), const(bim.shape), const(crt.shape), const(cit.shape),
                  const((1, D_SSM)), const((D_SSM, D_SSM)), const((1, D_SSM))],
        out_specs=(pl.BlockSpec((nb, tc, D_SSM), lambda i: (0, i, 0)),
                   const((N_SLABS, 2 * nb, LANES))),
        out_shape=(jax.ShapeDtypeStruct((nb, t_len, D_SSM), BF16),
                   jax.ShapeDtypeStruct((N_SLABS, 2 * nb, LANES), F32)),
        scratch_shapes=[pltpu.VMEM((N_SLABS, 2 * nb * pitch, LANES), F32),
                        pltpu.VMEM((N_SLABS, 2 * nb, LANES), F32),
                        pltpu.VMEM((nb * tc, D_SSM), F32)],
        compiler_params=_params(("arbitrary",), 56),
        name="s5_scan")(proj3, proj3, h0, abr.reshape(N_SLABS, 1, LANES), abi.reshape(N_SLABS, 1, LANES),
                        bre, bim, crt, cit, d, wglu, bglu)


def _s5_step_kernel(u_ref, zs_ref, h0r_ref, h0i_ref, abr_ref, abi_ref, bre_ref, bim_ref, crt_ref,
                    cit_ref, d_ref, wglu_ref, bglu_ref, out_ref, hr_ref, hi_ref, y_ref):
    uf = u_ref[...]
    ub = uf.astype(BF16)
    n_kt = D_SSM // SSM_KT
    for kt in range(n_kt):
        lhs = ub[:, SSM_KT * kt:SSM_KT * (kt + 1)]
        sl = slice(SSM_NT * kt, SSM_NT * (kt + 1))
        ar, ai = abr_ref[:, sl], abi_ref[:, sl]
        h0r, h0i = h0r_ref[:, sl], h0i_ref[:, sl]
        hr_ref[:, sl] = ar * h0r - ai * h0i + jnp.dot(lhs, bre_ref[kt], preferred_element_type=F32)
        hi_ref[:, sl] = ar * h0i + ai * h0r + jnp.dot(lhs, bim_ref[kt], preferred_element_type=F32)
    for jt in range(n_kt):
        sl = slice(SSM_NT * jt, SSM_NT * (jt + 1))
        y_ref[:, SSM_KT * jt:SSM_KT * (jt + 1)] = (
            jnp.dot(hr_ref[:, sl].astype(BF16), crt_ref[jt], preferred_element_type=F32)
            - jnp.dot(hi_ref[:, sl].astype(BF16), cit_ref[jt], preferred_element_type=F32))
    out = _ssm_epilogue(y_ref[...], uf, zs_ref[...], d_ref[...], wglu_ref[...], bglu_ref[...])
    out_ref[...] = out.astype(BF16)


def _s5_step(proj, h0r, h0i, abr, abi, bre, bim, crt, cit, d, wglu, bglu):
    m = h0r.shape[0]
    const = lambda shape: pl.BlockSpec(shape, lambda i: (0,) * len(shape))
    return pl.pallas_call(
        _s5_step_kernel,
        grid=(1,),
        in_specs=[pl.BlockSpec((m, D_SSM), lambda i: (0, C_U // D_SSM)),
                  pl.BlockSpec((m, D_SSM), lambda i: (0, C_ZS // D_SSM)),
                  const((m, D_STATE)), const((m, D_STATE)),
                  const((1, D_STATE)), const((1, D_STATE)),
                  const(bre.shape), const(bim.shape), const(crt.shape), const(cit.shape),
                  const((1, D_SSM)), const((D_SSM, D_SSM)), const((1, D_SSM))],
        out_specs=(const((m, D_SSM)), const((m, D_STATE)), const((m, D_STATE))),
        out_shape=(jax.ShapeDtypeStruct((m, D_SSM), BF16),
                   jax.ShapeDtypeStruct((m, D_STATE), F32),
                   jax.ShapeDtypeStruct((m, D_STATE), F32)),
        scratch_shapes=[pltpu.VMEM((m, D_SSM), F32)],
        compiler_params=_params(("arbitrary",), 48),
        name="s5_step")(proj, proj, h0r, h0i, abr, abi, bre, bim, crt, cit, d, wglu, bglu)


def _attn_kernel(q_ref, k_ref, vt_ref, km_ref, vmt_ref, za_ref, o_ref, m_ref, l_ref, acc_ref, *, tq):
    qi = pl.program_id(1)
    nt = (((1,), (1,)), ((), ()))
    heads = [(slice(HEAD_Q * h, HEAD_Q * (h + 1)), slice(V_HEAD * h, V_HEAD * (h + 1)))
             for h in range(N_HEADS)]

    for h, (qs, vs) in enumerate(heads):
        s = lax.dot_general(km_ref[:, qs], q_ref[:, qs], nt, preferred_element_type=F32)
        m = jnp.max(s, axis=0, keepdims=True)
        p = jnp.exp2(s - m)
        m_ref[h] = m
        l_ref[h] = jnp.sum(p, axis=0, keepdims=True)
        acc_ref[h] = jnp.dot(vmt_ref[vs, :], p.astype(BF16), preferred_element_type=F32)

    def kv_block(j, masked):
        off = pl.multiple_of(j * tq, tq)
        if masked:
            key = lax.broadcasted_iota(jnp.int32, (tq, tq), 0)
            qry = lax.broadcasted_iota(jnp.int32, (tq, tq), 1)
            keep = key <= qry
        for h, (qs, vs) in enumerate(heads):
            s = lax.dot_general(k_ref[pl.ds(off, tq), qs], q_ref[:, qs], nt,
                                preferred_element_type=F32)
            if masked:
                s = jnp.where(keep, s, -jnp.inf)
            m_prev = m_ref[h]
            m_new = jnp.maximum(m_prev, jnp.max(s, axis=0, keepdims=True))
            alpha = jnp.exp2(m_prev - m_new)
            p = jnp.exp2(s - m_new)
            l_ref[h] = alpha * l_ref[h] + jnp.sum(p, axis=0, keepdims=True)
            acc_ref[h] = alpha * acc_ref[h] + jnp.dot(
                vt_ref[j, vs, :], p.astype(BF16), preferred_element_type=F32)
            m_ref[h] = m_new

    def full_block(j, carry):
        kv_block(j, False)
        return carry

    lax.fori_loop(0, qi, full_block, 0)
    kv_block(qi, True)
    za = za_ref[...]
    gate = za * jax.nn.sigmoid(za)
    for h, (qs, vs) in enumerate(heads):
        o = (acc_ref[h] / l_ref[h]).T
        o_ref[:, vs] = (o * gate[:, vs]).astype(BF16)


def _attention(q, kcat, vt, kmeta, vmeta_t, proj, bsz, seq, tq):
    nq = seq // tq
    kern = functools.partial(_attn_kernel, tq=tq)
    return pl.pallas_call(
        kern,
        grid=(bsz, nq),
        in_specs=[pl.BlockSpec((tq, N_HEADS * HEAD_Q), lambda b, i: (b * nq + i, 0)),
                  pl.BlockSpec((None, seq, N_HEADS * HEAD_Q), lambda b, i: (b, 0, 0)),
                  pl.BlockSpec((nq, D_ATT, tq), lambda b, i: (b, 0, 0)),
                  pl.BlockSpec((N_META, N_HEADS * HEAD_Q), lambda b, i: (0, 0)),
                  pl.BlockSpec((D_ATT, N_META), lambda b, i: (0, 0)),
                  pl.BlockSpec((tq, D_ATT), lambda b, i: (b * nq + i, C_ZA // D_ATT))],
        out_specs=pl.BlockSpec((tq, D_ATT), lambda b, i: (b * nq + i, 0)),
        out_shape=jax.ShapeDtypeStruct((bsz * seq, D_ATT), BF16),
        scratch_shapes=[pltpu.VMEM((N_HEADS, 1, tq), F32), pltpu.VMEM((N_HEADS, 1, tq), F32),
                        pltpu.VMEM((N_HEADS, V_HEAD, tq), F32)],
        compiler_params=_params(("parallel", "arbitrary"), 48),
        name="attn_prompt")(q, kcat.reshape(bsz, seq, N_HEADS * HEAD_Q), vt, kmeta, vmeta_t, proj)


def _qabsorb_kernel(q_ref, wukt_ref, qa_ref):
    for h in range(N_HEADS):
        qn = q_ref[:, HEAD_Q * h:HEAD_Q * h + QK_NOPE]
        qa_ref[:, KV_LORA * h:KV_LORA * (h + 1)] = jnp.dot(
            qn, wukt_ref[h], preferred_element_type=F32).astype(BF16)


def _qabsorb(q, wukt):
    m = q.shape[0]
    return pl.pallas_call(
        _qabsorb_kernel,
        out_shape=jax.ShapeDtypeStruct((m, N_HEADS * KV_LORA), BF16),
        name="q_absorb")(q, wukt)


def _decode_kernel(pt_ref, qa_ref, qp_ref, cn_ref, kn_ref, *rest, npg):
    c_refs, k_refs = rest[:npg], rest[npg:2 * npg]
    o_ref, m_ref, l_ref, acc_ref = rest[2 * npg:]
    j = pl.program_id(1)

    @pl.when(j == 0)
    def _():
        m_ref[...] = jnp.full(m_ref.shape, -jnp.inf, F32)
        l_ref[...] = jnp.zeros(l_ref.shape, F32)
        acc_ref[...] = jnp.zeros(acc_ref.shape, F32)

    qa, qp = qa_ref[...], qp_ref[...]
    nt = (((1,), (1,)), ((), ()))
    page = c_refs[0].shape[0]
    cb = [r[...].astype(BF16) for r in c_refs]
    s = jnp.concatenate(
        [lax.dot_general(qa, cb[k], nt, preferred_element_type=F32)
         + jnp.dot(qp, k_refs[k][...].astype(BF16), preferred_element_type=F32)
         for k in range(npg)], axis=1)
    m_prev = m_ref[...]
    m_new = jnp.maximum(m_prev, jnp.max(s, axis=-1, keepdims=True))
    alpha = jnp.exp2(m_prev - m_new)
    p32 = jnp.exp2(s - m_new)
    p = p32.astype(BF16)
    pv = jnp.dot(p[:, :page], cb[0], preferred_element_type=F32)
    for k in range(1, npg):
        pv = pv + jnp.dot(p[:, page * k:page * (k + 1)], cb[k], preferred_element_type=F32)
    l_ref[...] = alpha * l_ref[...] + jnp.sum(p32, axis=-1, keepdims=True)
    acc_ref[...] = alpha * acc_ref[...] + pv
    m_ref[...] = m_new

    @pl.when(j == pl.num_programs(1) - 1)
    def _():
        cn, kn = cn_ref[...], kn_ref[...]
        s_new = (jnp.sum(qa.astype(F32) * cn, axis=-1, keepdims=True)
                 + jnp.sum(qp.astype(F32) * kn, axis=-1, keepdims=True))
        m_old = m_ref[...]
        m_fin = jnp.maximum(m_old, s_new)
        a = jnp.exp2(m_old - m_fin)
        pn = jnp.exp2(s_new - m_fin)
        l_fin = a * l_ref[...] + pn
        o_ref[...] = (a * acc_ref[...] + pn * cn) / l_fin


def _decode(page_table, qa, qp, c_new, k_new, cache_ckv, cache_kpe_t, npg):
    dbs, n_pages = page_table.shape
    page = cache_ckv.shape[2]
    page_specs = (
        [pl.BlockSpec((None, None, page, KV_LORA),
                      functools.partial(lambda b, j, pt, k: (0, pt[b * n_pages + j * npg + k], 0, 0), k=k))
         for k in range(npg)]
        + [pl.BlockSpec((None, None, QK_ROPE, page),
                        functools.partial(lambda b, j, pt, k: (0, pt[b * n_pages + j * npg + k], 0, 0), k=k))
           for k in range(npg)])
    per_b = lambda shape: pl.BlockSpec((None,) + shape, lambda b, j, pt: (b, 0, 0))
    grid_spec = pltpu.PrefetchScalarGridSpec(
        num_scalar_prefetch=1,
        grid=(dbs, n_pages // npg),
        in_specs=[per_b((N_HEADS, KV_LORA)), per_b((N_HEADS, QK_ROPE)),
                  per_b((1, KV_LORA)), per_b((1, QK_ROPE))] + page_specs,
        out_specs=per_b((N_HEADS, KV_LORA)),
        scratch_shapes=[pltpu.VMEM((N_HEADS, 1), F32), pltpu.VMEM((N_HEADS, 1), F32),
                        pltpu.VMEM((N_HEADS, KV_LORA), F32)])
    return pl.pallas_call(
        functools.partial(_decode_kernel, npg=npg),
        grid_spec=grid_spec,
        out_shape=jax.ShapeDtypeStruct((dbs, N_HEADS, KV_LORA), F32),
        compiler_params=_params(("parallel", "arbitrary"), 48),
        name="attn_decode")(page_table.reshape(-1), qa, qp, c_new, k_new,
                            *([cache_ckv] * npg), *([cache_kpe_t] * npg))


def _ouv_kernel(ol_ref, wuv_ref, za_ref, o_ref):
    za = za_ref[...]
    gate = za * jax.nn.sigmoid(za)
    for h in range(N_HEADS):
        o = jnp.dot(ol_ref[:, KV_LORA * h:KV_LORA * (h + 1)].astype(BF16),
                    wuv_ref[:, V_HEAD * h:V_HEAD * (h + 1)], preferred_element_type=F32)
        o_ref[:, V_HEAD * h:V_HEAD * (h + 1)] = (o * gate[:, V_HEAD * h:V_HEAD * (h + 1)]).astype(BF16)


def _ouv(o_lat, wuv, proj):
    m = o_lat.shape[0]
    const = lambda shape: pl.BlockSpec(shape, lambda i: (0,) * len(shape))
    return pl.pallas_call(
        _ouv_kernel,
        grid=(1,),
        in_specs=[const((m, N_HEADS * KV_LORA)), const((KV_LORA, D_ATT)),
                  pl.BlockSpec((m, D_ATT), lambda i: (0, C_ZA // D_ATT))],
        out_specs=const((m, D_ATT)),
        out_shape=jax.ShapeDtypeStruct((m, D_ATT), BF16),
        name="o_uv")(o_lat, wuv, proj)


def _outproj_kernel(ssm_ref, att_ref, wtop_ref, wbot_ref, x_ref, g_ref, y_ref):
    h = (x_ref[...]
         + jnp.dot(ssm_ref[...], wtop_ref[...], preferred_element_type=F32)
         + jnp.dot(att_ref[...], wbot_ref[...], preferred_element_type=F32))
    y_ref[...] = _rms(h, g_ref[...])


def _outproj(ssm, att, w, x, g, tm):
    m = x.shape[0]
    return pl.pallas_call(
        _outproj_kernel,
        grid=(m // tm,),
        in_specs=[pl.BlockSpec((tm, D_SSM), lambda i: (i, 0)),
                  pl.BlockSpec((tm, D_ATT), lambda i: (i, 0)),
                  pl.BlockSpec((D_SSM, D_MODEL), lambda i: (0, 0)),
                  pl.BlockSpec((D_ATT, D_MODEL), lambda i: (1, 0)),
                  pl.BlockSpec((tm, D_MODEL), lambda i: (i, 0)),
                  pl.BlockSpec((1, D_MODEL), lambda i: (0, 0))],
        out_specs=pl.BlockSpec((tm, D_MODEL), lambda i: (i, 0)),
        out_shape=jax.ShapeDtypeStruct((m, D_MODEL), F32),
        compiler_params=_params(("parallel",), 48),
        name="outproj")(ssm, att, w, w, x, g)


def _slab_major(h):
    return h.reshape(h.shape[0], N_SLABS, LANES).transpose(1, 0, 2)


def _slab_minor(h):
    return h.transpose(1, 0, 2).reshape(h.shape[1], D_STATE)


def kernel(x_prompt, x_sample, cache_ckv, cache_kpe, state_ssm_re, state_ssm_im, page_table, meta_tokens, g_norm, w_in, ssm_a_re, ssm_a_im, ssm_log_dt, ssm_b_re, ssm_b_im, ssm_c_re, ssm_c_im, ssm_d, w_glu, b_glu, g_q_norm, w_uq, g_kv_norm, w_uk, w_uv, w_out, g_final):
    bsz, seq, _ = x_prompt.shape
    dbs = x_sample.shape[0]
    n_pages, page = page_table.shape[1], cache_ckv.shape[2]
    past_len = n_pages * page
    layer = 0

    wi = w_in[layer]
    o_cq, o_ckv, o_kpe, o_za = 2 * D_SSM, 2 * D_SSM + Q_LORA, 2 * D_SSM + Q_LORA + KV_LORA, \
        2 * D_SSM + Q_LORA + KV_LORA + QK_ROPE
    w_in_p = jnp.concatenate(
        [wi[:, o_cq:o_ckv], wi[:, o_kpe:o_za], jnp.zeros((D_MODEL, C_U - C_KPE - QK_ROPE), wi.dtype),
         wi[:, :D_SSM], wi[:, D_SSM:o_cq], wi[:, o_za:], wi[:, o_ckv:o_kpe]], axis=1).astype(BF16)
    w_uq_p = jnp.pad(w_uq[layer].reshape(Q_LORA, N_HEADS, QK_NOPE + QK_ROPE),
                     ((0, 0), (0, 0), (0, HEAD_Q - QK_NOPE - QK_ROPE))
                     ).reshape(Q_LORA, N_HEADS * HEAD_Q).astype(BF16)
    w_uk_flat = w_uk[layer].reshape(KV_LORA, D_ATT).astype(BF16)
    w_uk_t = w_uk[layer].transpose(1, 2, 0).astype(BF16)
    w_uv_flat = w_uv[layer].reshape(KV_LORA, D_ATT).astype(BF16)
    w_glu_b = w_glu[layer].astype(BF16)
    w_out_b = w_out[layer].astype(BF16)
    g_in = g_norm[layer].reshape(1, D_MODEL)
    g_q = g_q_norm[layer].reshape(1, Q_LORA)
    g_kv = g_kv_norm[layer].reshape(1, KV_LORA)
    g_fin = g_final.reshape(1, D_MODEL)
    d_skip = ssm_d[layer].reshape(1, D_SSM)
    bglu = b_glu[layer].reshape(1, D_SSM)

    abr, abi, bbr_t, bbi_t = _ssm_prep(ssm_a_re[layer], ssm_a_im[layer], ssm_log_dt[layer],
                                       ssm_b_re[layer].transpose(0, 2, 1), ssm_b_im[layer].transpose(0, 2, 1))
    bre, bim = _blockdiag_in(bbr_t).astype(BF16), _blockdiag_in(bbi_t).astype(BF16)
    crt, cit = _blockdiag_out(ssm_c_re[layer]).astype(BF16), _blockdiag_out(ssm_c_im[layer]).astype(BF16)
    abr_f, abi_f = abr.reshape(1, D_STATE), abi.reshape(1, D_STATE)
    ssm_w = (bre, bim, crt, cit, d_skip, w_glu_b, bglu)

    pos = jnp.concatenate([jnp.arange(N_META, N_META + seq), jnp.full((dbs,), past_len), jnp.arange(N_META)])
    tabs = _rope_tables(pos)
    tabs_real = tuple(t[:seq] for t in tabs)
    tabs_small = tuple(t[seq:] for t in tabs)

    x_real = x_prompt.reshape(bsz * seq, D_MODEL)
    x_small = jnp.concatenate([x_sample.reshape(dbs, D_MODEL), meta_tokens.astype(x_prompt.dtype)], axis=0)
    n_small = dbs + N_META
    proj_real = _inproj(x_real, g_in, w_in_p, 512, 768)
    proj_small = _inproj(x_small, g_in, w_in_p, n_small, 768)

    w_uv_t = w_uv_flat.T
    c_real, kr_real, kcat_real, vt_real = _kvprep(proj_real, g_kv, tabs_real, w_uk_flat, w_uv_t, 512, ATT_TILE)
    c_small, kr_small, kcat_small, vt_small = _kvprep(proj_small, g_kv, tabs_small, w_uk_flat, w_uv_t,
                                                      n_small, n_small)
    q_real = _qproj(proj_real, g_q, w_uq_p, tabs_real, 512)
    q_small = _qproj(proj_small, g_q, w_uq_p, tabs_small, n_small)

    proj_meta = jnp.broadcast_to(proj_small[dbs:][None], (bsz, N_META, N_PROJ))
    h_zero = jnp.zeros((N_SLABS, 2 * bsz, LANES), F32)
    _, h_meta = _s5_scan(proj_meta, h_zero, abr_f, abi_f, *ssm_w, tc=N_META)
    ssm_real, h_fin = _s5_scan(proj_real.reshape(bsz, seq, N_PROJ), h_meta, abr_f, abi_f, *ssm_w, tc=64)
    h_fin = _slab_minor(h_fin)
    ssm_smp, hr_s, hi_s = _s5_step(proj_small, state_ssm_re[layer].reshape(dbs, D_STATE),
                                   state_ssm_im[layer].reshape(dbs, D_STATE), abr_f, abi_f, *ssm_w)

    att_real = _attention(q_real, kcat_real, vt_real, kcat_small[dbs:], vt_small[0, :, dbs:], proj_real,
                          bsz, seq, ATT_TILE)

    qs = q_small[:dbs]
    qa = _qabsorb(qs, w_uk_t).reshape(dbs, N_HEADS, KV_LORA)
    qp = qs.reshape(dbs, N_HEADS, HEAD_Q)[:, :, QK_NOPE:QK_NOPE + QK_ROPE]
    c_new = c_small[:dbs].reshape(dbs, 1, KV_LORA)
    k_new = kr_small[:dbs, :QK_ROPE].reshape(dbs, 1, QK_ROPE)
    o_lat = _decode(page_table, qa, qp, c_new, k_new, cache_ckv, jnp.swapaxes(cache_kpe, 2, 3), 32)
    att_smp = _ouv(o_lat.reshape(dbs, N_HEADS * KV_LORA), w_uv_flat, proj_small)

    y_prompt = _outproj(ssm_real.reshape(bsz * seq, D_SSM), att_real, w_out_b, x_real, g_fin, 256)
    y_sample = _outproj(ssm_smp, att_smp, w_out_b, x_sample.reshape(dbs, D_MODEL), g_fin, dbs)

    c_meta, k_meta = c_small[dbs:], kr_small[dbs:, :QK_ROPE]
    ckv_p = jnp.concatenate([jnp.broadcast_to(c_meta[None], (bsz, N_META, KV_LORA)),
                             c_real.reshape(bsz, seq, KV_LORA)], axis=1)
    kpe_p = jnp.concatenate([jnp.broadcast_to(k_meta[None], (bsz, N_META, QK_ROPE)),
                             kr_real[:, :QK_ROPE].reshape(bsz, seq, QK_ROPE)], axis=1)
    return (y_prompt.reshape(bsz, seq, D_MODEL),
            y_sample.reshape(dbs, 1, D_MODEL),
            ckv_p[None], kpe_p[None],
            h_fin[:bsz].reshape(1, bsz, N_GROUPS, N_STATE),
            h_fin[bsz:].reshape(1, bsz, N_GROUPS, N_STATE),
            c_small[:dbs].reshape(1, dbs, 1, KV_LORA),
            kr_small[:dbs, :QK_ROPE].reshape(1, dbs, 1, QK_ROPE),
            hr_s.reshape(1, dbs, N_GROUPS, N_STATE),
            hi_s.reshape(1, dbs, N_GROUPS, N_STATE))
```

```python
import functools
import math

import numpy as np
import jax
import jax.numpy as jnp
from jax import lax
from jax.experimental import pallas as pl
from jax.experimental.pallas import tpu as pltpu

F32 = jnp.float32
BF16 = jnp.bfloat16

D_MODEL = 2048
N_META = 16
D_SSM = 1024
SSM_GROUP = 16
N_GROUPS = 64
N_STATE = 64
D_STATE = N_GROUPS * N_STATE
D_ATT = 1024
V_HEAD = 128
N_HEADS = 8
QK_NOPE = 128
QK_ROPE = 64
KV_LORA = 512
Q_LORA = 768
ROPE_THETA = 10000.0
SOFTMAX_SCALE = (QK_NOPE + QK_ROPE) ** -0.5
Q_SCALE = SOFTMAX_SCALE * math.log2(math.e)
EPS = 1e-6

LANES = 128
SUBLANES = 8
MIB = 1024 * 1024

C_CQ = 0
C_KPE = 768
C_U = 1024
C_ZS = 2048
C_ZA = 3072
C_CKV = 4096
N_PROJ = 4608
HEAD_Q = 256
N_SLABS = D_STATE // LANES
SLAB_GROUP = 8
ATT_TILE = 512
SSM_KT = 256
SSM_NT = SSM_KT * N_STATE // SSM_GROUP


def _params(sem, vmem_mib=None):
    return pltpu.CompilerParams(
        dimension_semantics=sem,
        vmem_limit_bytes=None if vmem_mib is None else vmem_mib * MIB)


def _rms(x, g):
    return x * lax.rsqrt(jnp.mean(x * x, axis=-1, keepdims=True) + EPS) * g


def _rope128(x, cos_t, sin_lo, sin_hi):
    return (x * cos_t + pltpu.roll(x, LANES - QK_ROPE // 2, axis=1) * sin_lo
            + pltpu.roll(x, QK_ROPE // 2, axis=1) * sin_hi)


def _rope_table_kernel(pos_ref, inv_ref, cos_ref, slo_ref, shi_ref):
    ang = pos_ref[...] * inv_ref[...]
    c, s = jnp.cos(ang), jnp.sin(ang)
    lane = lax.broadcasted_iota(jnp.int32, ang.shape, 1)
    half = QK_ROPE // 2
    cos_ref[...] = jnp.where(lane < QK_ROPE, c, 0.0)
    slo_ref[...] = jnp.where(lane < half, -s, 0.0)
    shi_ref[...] = jnp.where((lane >= half) & (lane < QK_ROPE), s, 0.0)


def _rope_tables(pos):
    n = pos.shape[0]
    half = QK_ROPE // 2
    inv = ROPE_THETA ** (-(np.arange(LANES) % half).astype(np.float64) / half)
    inv = jnp.asarray(inv.astype(np.float32)).reshape(1, LANES)
    shp = jax.ShapeDtypeStruct((n, LANES), F32)
    return pl.pallas_call(_rope_table_kernel, out_shape=(shp, shp, shp), name="rope_tables")(
        pos.reshape(n, 1).astype(F32), inv)


def _inproj_kernel(x_ref, g_ref, w_ref, o_ref, xn_ref):
    @pl.when(pl.program_id(1) == 0)
    def _():
        xn_ref[...] = _rms(x_ref[...], g_ref[...]).astype(BF16)

    o_ref[...] = jnp.dot(xn_ref[...], w_ref[...], preferred_element_type=F32)


def _inproj(x, g, w, tm, tn):
    m = x.shape[0]
    return pl.pallas_call(
        _inproj_kernel,
        grid=(m // tm, N_PROJ // tn),
        in_specs=[pl.BlockSpec((tm, D_MODEL), lambda i, j: (i, 0)),
                  pl.BlockSpec((1, D_MODEL), lambda i, j: (0, 0)),
                  pl.BlockSpec((D_MODEL, tn), lambda i, j: (0, j))],
        out_specs=pl.BlockSpec((tm, tn), lambda i, j: (i, j)),
        out_shape=jax.ShapeDtypeStruct((m, N_PROJ), F32),
        scratch_shapes=[pltpu.VMEM((tm, D_MODEL), BF16)],
        compiler_params=_params(("parallel", "arbitrary"), 48),
        name="inproj")(x, g, w)


def _kvprep_kernel(ckv_ref, kpe_ref, g_ref, cos_ref, slo_ref, shi_ref, wuk_ref, wuvt_ref,
                   c_ref, kr_ref, kcat_ref, vt_ref, *, tk):
    c = _rms(ckv_ref[...], g_ref[...])
    c_ref[...] = c
    cb = c.astype(BF16)
    kr = _rope128(kpe_ref[...], cos_ref[...], slo_ref[...], shi_ref[...])
    kr_ref[...] = kr
    krb = kr.astype(BF16)
    knope = jnp.dot(cb, wuk_ref[...], preferred_element_type=F32).astype(BF16)
    for h in range(N_HEADS):
        kcat_ref[:, HEAD_Q * h:HEAD_Q * h + QK_NOPE] = knope[:, QK_NOPE * h:QK_NOPE * (h + 1)]
        kcat_ref[:, HEAD_Q * h + QK_NOPE:HEAD_Q * (h + 1)] = krb
    vt = lax.dot_general(wuvt_ref[...], cb, (((1,), (1,)), ((), ())),
                         preferred_element_type=F32).astype(BF16)
    for kb in range(vt_ref.shape[0]):
        vt_ref[kb] = vt[:, tk * kb:tk * (kb + 1)]


def _kvprep(proj, g, tabs, wuk, wuvt, tm, tk):
    m = proj.shape[0]
    nt = tabs[0].shape[0] // tm
    tab = pl.BlockSpec((tm, LANES), lambda i: (i % nt, 0))
    row = lambda w: pl.BlockSpec((tm, w), lambda i: (i, 0))
    return pl.pallas_call(
        functools.partial(_kvprep_kernel, tk=tk),
        grid=(m // tm,),
        in_specs=[pl.BlockSpec((tm, KV_LORA), lambda i: (i, C_CKV // KV_LORA)),
                  pl.BlockSpec((tm, LANES), lambda i: (i, C_KPE // LANES)),
                  pl.BlockSpec((1, KV_LORA), lambda i: (0, 0)),
                  tab, tab, tab,
                  pl.BlockSpec((KV_LORA, D_ATT), lambda i: (0, 0)),
                  pl.BlockSpec((D_ATT, KV_LORA), lambda i: (0, 0))],
        out_specs=(row(KV_LORA), row(LANES), row(N_HEADS * HEAD_Q),
                   pl.BlockSpec((tm // tk, D_ATT, tk), lambda i: (i, 0, 0))),
        out_shape=(jax.ShapeDtypeStruct((m, KV_LORA), F32),
                   jax.ShapeDtypeStruct((m, LANES), F32),
                   jax.ShapeDtypeStruct((m, N_HEADS * HEAD_Q), BF16),
                   jax.ShapeDtypeStruct((m // tk, D_ATT, tk), BF16)),
        compiler_params=_params(("parallel",)),
        name="kvprep")(proj, proj, g, *tabs, wuk, wuvt)


def _qproj_kernel(cq_ref, g_ref, w_ref, cos_ref, slo_ref, shi_ref, q_ref):
    cqn = _rms(cq_ref[...], g_ref[...]).astype(BF16)
    q = jnp.dot(cqn, w_ref[...], preferred_element_type=F32)
    cos_t, slo, shi = cos_ref[...], slo_ref[...], shi_ref[...]
    for h in range(N_HEADS):
        lo = HEAD_Q * h
        q_ref[:, lo:lo + QK_NOPE] = (q[:, lo:lo + QK_NOPE] * Q_SCALE).astype(BF16)
        pe = _rope128(q[:, lo + QK_NOPE:lo + HEAD_Q], cos_t, slo, shi)
        q_ref[:, lo + QK_NOPE:lo + HEAD_Q] = (pe * Q_SCALE).astype(BF16)


def _qproj(proj, g, w, tabs, tm):
    m = proj.shape[0]
    nt = tabs[0].shape[0] // tm
    tab = pl.BlockSpec((tm, LANES), lambda i: (i % nt, 0))
    return pl.pallas_call(
        _qproj_kernel,
        grid=(m // tm,),
        in_specs=[pl.BlockSpec((tm, Q_LORA), lambda i: (i, C_CQ // Q_LORA)),
                  pl.BlockSpec((1, Q_LORA), lambda i: (0, 0)),
                  pl.BlockSpec((Q_LORA, N_HEADS * HEAD_Q), lambda i: (0, 0)),
                  tab, tab, tab],
        out_specs=pl.BlockSpec((tm, N_HEADS * HEAD_Q), lambda i: (i, 0)),
        out_shape=jax.ShapeDtypeStruct((m, N_HEADS * HEAD_Q), BF16),
        compiler_params=_params(("parallel",)),
        name="qproj")(proj, g, w, *tabs)


def _ssm_prep_kernel(are_ref, aim_ref, ldt_ref, bre_ref, bim_ref, abr_ref, abi_ref, bbr_ref, bbi_ref):
    dt = jnp.exp(ldt_ref[...])
    ar, ai = are_ref[...], aim_ref[...]
    mag = jnp.exp(dt * ar)
    abr, abi = mag * jnp.cos(dt * ai), mag * jnp.sin(dt * ai)
    den = ar * ar + ai * ai
    nr, ni = abr - 1.0, abi
    f_re = (nr * ar + ni * ai) / den
    f_im = (ni * ar - nr * ai) / den
    br, bi = bre_ref[...], bim_ref[...]
    abr_ref[...] = abr
    abi_ref[...] = abi
    bbr_ref[...] = f_re * br - f_im * bi
    bbi_ref[...] = f_re * bi + f_im * br


def _ssm_prep(a_re, a_im, log_dt, b_re_t, b_im_t):
    gn = jax.ShapeDtypeStruct((N_GROUPS, 1, N_STATE), F32)
    gcn = jax.ShapeDtypeStruct((N_GROUPS, SSM_GROUP, N_STATE), F32)
    return pl.pallas_call(_ssm_prep_kernel, out_shape=(gn, gn, gcn, gcn), name="ssm_prep")(
        a_re.reshape(N_GROUPS, 1, N_STATE), a_im.reshape(N_GROUPS, 1, N_STATE),
        log_dt.reshape(N_GROUPS, 1, 1), b_re_t, b_im_t)


def _blockdiag_in(b_t):
    x = b_t.reshape(D_SSM // SSM_KT, SSM_KT // SSM_GROUP, SSM_GROUP, N_STATE)
    eye = jnp.eye(SSM_KT // SSM_GROUP, dtype=b_t.dtype)
    out = x[:, :, :, None, :] * eye[None, :, None, :, None]
    return out.reshape(D_SSM // SSM_KT, SSM_KT, SSM_NT)


def _blockdiag_out(c):
    x = c.reshape(D_SSM // SSM_KT, SSM_KT // SSM_GROUP, SSM_GROUP, N_STATE).transpose(0, 1, 3, 2)
    eye = jnp.eye(SSM_KT // SSM_GROUP, dtype=c.dtype)
    out = x[:, :, :, None, :] * eye[None, :, None, :, None]
    return out.reshape(D_SSM // SSM_KT, SSM_NT, SSM_KT)


def _ssm_epilogue(y, uf, zs, d, wglu, bglu):
    ys = jax.nn.gelu(y + d * uf)
    gate = jax.nn.sigmoid(jnp.dot(ys.astype(BF16), wglu, preferred_element_type=F32) + bglu)
    return ys * gate * (zs * jax.nn.sigmoid(zs))


def _s5_scan_kernel(u_ref, zs_ref, h0_ref, abr_ref, abi_ref, bre_ref, bim_ref, crt_ref, cit_ref,
                    d_ref, wglu_ref, bglu_ref, out_ref, hfin_ref, buf_ref, hst_ref, y_ref,
                    *, nb, tc, pitch):
    i = pl.program_id(0)

    @pl.when(i == 0)
    def _():
        hst_ref[...] = h0_ref[...]

    uf = u_ref[...].reshape(nb * tc, D_SSM)
    ub = uf.astype(BF16)
    n_kt = D_SSM // SSM_KT
    slabs_per_kt = SSM_NT // LANES
    for kt in range(n_kt):
        lhs = ub[:, SSM_KT * kt:SSM_KT * (kt + 1)]
        for ri, w_ref in ((0, bre_ref), (1, bim_ref)):
            res = jnp.dot(lhs, w_ref[kt], preferred_element_type=F32)
            for s in range(slabs_per_kt):
                for b in range(nb):
                    r0 = (ri * nb + b) * pitch
                    buf_ref[kt * slabs_per_kt + s, r0:r0 + tc, :] = (
                        res[b * tc:(b + 1) * tc, LANES * s:LANES * (s + 1)])

    row = lax.broadcasted_iota(jnp.int32, (2 * nb, LANES), 0)

    def scan_group(g, carry):
        base = g * SLAB_GROUP
        a1, a2, h = [], [], []
        for s in range(SLAB_GROUP):
            a1.append(jnp.broadcast_to(abr_ref[base + s], (2 * nb, LANES)))
            ai = jnp.broadcast_to(abi_ref[base + s], (2 * nb, LANES))
            a2.append(jnp.where(row < nb, -ai, ai))
            h.append(hst_ref[base + s])
        for t in range(tc):
            for s in range(SLAB_GROUP):
                bu = buf_ref[base + s, pl.ds(t, 2 * nb, stride=pitch), :]
                h[s] = a1[s] * h[s] + a2[s] * pltpu.roll(h[s], nb, axis=0) + bu
                buf_ref[base + s, pl.ds(t, 2 * nb, stride=pitch), :] = h[s]
        for s in range(SLAB_GROUP):
            hst_ref[base + s] = h[s]
        return carry

    lax.fori_loop(0, N_SLABS // SLAB_GROUP, scan_group, 0)

    def states(jt, ri):
        rows = []
        for b in range(nb):
            r0 = (ri * nb + b) * pitch
            rows.append(jnp.concatenate(
                [buf_ref[jt * slabs_per_kt + s, r0:r0 + tc, :] for s in range(slabs_per_kt)], axis=1))
        return jnp.concatenate(rows, axis=0).astype(BF16)

    for jt in range(n_kt):
        y_ref[:, SSM_KT * jt:SSM_KT * (jt + 1)] = (
            jnp.dot(states(jt, 0), crt_ref[jt], preferred_element_type=F32)
            - jnp.dot(states(jt, 1), cit_ref[jt], preferred_element_type=F32))

    zs = zs_ref[...].reshape(nb * tc, D_SSM)
    out = _ssm_epilogue(y_ref[...], uf, zs, d_ref[...], wglu_ref[...], bglu_ref[...])
    out_ref[...] = out.reshape(nb, tc, D_SSM).astype(BF16)

    @pl.when(i == pl.num_programs(0) - 1)
    def _():
        hfin_ref[...] = hst_ref[...]


def _s5_scan(proj3, h0, abr, abi, bre, bim, crt, cit, d, wglu, bglu, tc):
    nb, t_len = proj3.shape[0], proj3.shape[1]
    pitch = tc + SUBLANES
    const = lambda shape: pl.BlockSpec(shape, lambda i: (0,) * len(shape))
    kern = functools.partial(_s5_scan_kernel, nb=nb, tc=tc, pitch=pitch)
    return pl.pallas_call(
        kern,
        grid=(t_len // tc,),
        in_specs=[pl.BlockSpec((nb, tc, D_SSM), lambda i: (0, i, C_U // D_SSM)),
                  pl.BlockSpec((nb, tc, D_SSM), lambda i: (0, i, C_ZS // D_SSM)),
                  const((N_SLABS, 2 * nb, LANES)),
                  const((N_SLABS, 1, LANES)), const((N_SLABS, 1, LANES)),
                  const(bre.shape), const(bim.shape), const(crt.shape), const(cit.shape),
                  const((1, D_SSM)), const((D_SSM, D_SSM)), const((1, D_SSM))],
        out_specs=(pl.BlockSpec((nb, tc, D_SSM), lambda i: (0, i, 0)),
                   const((N_SLABS, 2 * nb, LANES))),
        out_shape=(jax.ShapeDtypeStruct((nb, t_len, D_SSM), BF16),
                   jax.ShapeDtypeStruct((N_SLABS, 2 * nb, LANES), F32)),
        scratch_shapes=[pltpu.VMEM((N_SLABS, 2 * nb * pitch, LANES), F32),
                        pltpu.VMEM((N_SLABS, 2 * nb, LANES), F32),
                        pltpu.VMEM((nb * tc, D_SSM), F32)],
        compiler_params=_params(("arbitrary",), 56),
        name="s5_scan")(proj3, proj3, h0, abr.reshape(N_SLABS, 1, LANES), abi.reshape(N_SLABS, 1, LANES),
                        bre, bim, crt, cit, d, wglu, bglu)


def _s5_step_kernel(u_ref, zs_ref, h0r_ref, h0i_ref, abr_ref, abi_ref, bre_ref, bim_ref, crt_ref,
                    cit_ref, d_ref, wglu_ref, bglu_ref, out_ref, hr_ref, hi_ref, y_ref):
    uf = u_ref[...]
    ub = uf.astype(BF16)
    n_kt = D_SSM // SSM_KT
    for kt in range(n_kt):
        lhs = ub[:, SSM_KT * kt:SSM_KT * (kt + 1)]
        sl = slice(SSM_NT * kt, SSM_NT * (kt + 1))
        ar, ai = abr_ref[:, sl], abi_ref[:, sl]
        h0r, h0i = h0r_ref[:, sl], h0i_ref[:, sl]
        hr_ref[:, sl] = ar * h0r - ai * h0i + jnp.dot(lhs, bre_ref[kt], preferred_element_type=F32)
        hi_ref[:, sl] = ar * h0i + ai * h0r + jnp.dot(lhs, bim_ref[kt], preferred_element_type=F32)
    for jt in range(n_kt):
        sl = slice(SSM_NT * jt, SSM_NT * (jt + 1))
        y_ref[:, SSM_KT * jt:SSM_KT * (jt + 1)] = (
            jnp.dot(hr_ref[:, sl].astype(BF16), crt_ref[jt], preferred_element_type=F32)
            - jnp.dot(hi_ref[:, sl].astype(BF16), cit_ref[jt], preferred_element_type=F32))
    out = _ssm_epilogue(y_ref[...], uf, zs_ref[...], d_ref[...], wglu_ref[...], bglu_ref[...])
    out_ref[...] = out.astype(BF16)


def _s5_step(proj, h0r, h0i, abr, abi, bre, bim, crt, cit, d, wglu, bglu):
    m = h0r.shape[0]
    const = lambda shape: pl.BlockSpec(shape, lambda i: (0,) * len(shape))
    return pl.pallas_call(
        _s5_step_kernel,
        grid=(1,),
        in_specs=[pl.BlockSpec((m, D_SSM), lambda i: (0, C_U // D_SSM)),
                  pl.BlockSpec((m, D_SSM), lambda i: (0, C_ZS // D_SSM)),
                  const((m, D_STATE)), const((m, D_STATE)),
                  const((1, D_STATE)), const((1, D_STATE)),
                  const(bre.shape), const(bim.shape), const(crt.shape), const(cit.shape),
                  const((1, D_SSM)), const((D_SSM, D_SSM)), const((1, D_SSM))],
        out_specs=(const((m, D_SSM)), const((m, D_STATE)), const((m, D_STATE))),
        out_shape=(jax.ShapeDtypeStruct((m, D_SSM), BF16),
                   jax.ShapeDtypeStruct((m, D_STATE), F32),
                   jax.ShapeDtypeStruct((m, D_STATE), F32)),
        scratch_shapes=[pltpu.VMEM((m, D_SSM), F32)],
        compiler_params=_params(("arbitrary",), 48),
        name="s5_step")(proj, proj, h0r, h0i, abr, abi, bre, bim, crt, cit, d, wglu, bglu)


def _attn_kernel(q_ref, k_ref, vt_ref, km_ref, vmt_ref, za_ref, o_ref, *scratch, tq):
    m_ref, l_ref, acc_ref = (scratch[:N_HEADS], scratch[N_HEADS:2 * N_HEADS], scratch[2 * N_HEADS:])
    qi = pl.program_id(1)
    nt = (((1,), (1,)), ((), ()))
    heads = [(slice(HEAD_Q * h, HEAD_Q * (h + 1)), slice(V_HEAD * h, V_HEAD * (h + 1)))
             for h in range(N_HEADS)]

    for h, (qs, vs) in enumerate(heads):
        s = lax.dot_general(km_ref[:, qs], q_ref[:, qs], nt, preferred_element_type=F32)
        m = jnp.max(s, axis=0, keepdims=True)
        p = jnp.exp2(s - m)
        m_ref[h][...] = m
        l_ref[h][...] = jnp.sum(p, axis=0, keepdims=True)
        acc_ref[h][...] = jnp.dot(vmt_ref[vs, :], p.astype(BF16), preferred_element_type=F32)

    def kv_block(j, masked):
        off = pl.multiple_of(j * tq, tq)
        if masked:
            key = lax.broadcasted_iota(jnp.int32, (tq, tq), 0)
            qry = lax.broadcasted_iota(jnp.int32, (tq, tq), 1)
            keep = key <= qry
        for h, (qs, vs) in enumerate(heads):
            s = lax.dot_general(k_ref[pl.ds(off, tq), qs], q_ref[:, qs], nt,
                                preferred_element_type=F32)
            if masked:
                s = jnp.where(keep, s, -jnp.inf)
            m_prev = m_ref[h][...]
            m_new = jnp.maximum(m_prev, jnp.max(s, axis=0, keepdims=True))
            alpha = jnp.exp2(m_prev - m_new)
            p = jnp.exp2(s - m_new)
            l_ref[h][...] = alpha * l_ref[h][...] + jnp.sum(p, axis=0, keepdims=True)
            acc_ref[h][...] = alpha * acc_ref[h][...] + jnp.dot(
                vt_ref[j, vs, :], p.astype(BF16), preferred_element_type=F32)
            m_ref[h][...] = m_new

    def full_block(j, carry):
        kv_block(j, False)
        return carry

    lax.fori_loop(0, qi, full_block, 0)
    kv_block(qi, True)
    za = za_ref[...]
    gate = za * jax.nn.sigmoid(za)
    for h, (qs, vs) in enumerate(heads):
        o = (acc_ref[h][...] / l_ref[h][...]).T
        o_ref[:, vs] = (o * gate[:, vs]).astype(BF16)


def _attention(q, kcat, vt, kmeta, vmeta_t, proj, bsz, seq, tq):
    nq = seq // tq
    kern = functools.partial(_attn_kernel, tq=tq)
    return pl.pallas_call(
        kern,
        grid=(bsz, nq),
        in_specs=[pl.BlockSpec((tq, N_HEADS * HEAD_Q), lambda b, i: (b * nq + i, 0)),
                  pl.BlockSpec((None, seq, N_HEADS * HEAD_Q), lambda b, i: (b, 0, 0)),
                  pl.BlockSpec((nq, D_ATT, tq), lambda b, i: (b, 0, 0)),
                  pl.BlockSpec((N_META, N_HEADS * HEAD_Q), lambda b, i: (0, 0)),
                  pl.BlockSpec((D_ATT, N_META), lambda b, i: (0, 0)),
                  pl.BlockSpec((tq, D_ATT), lambda b, i: (b * nq + i, C_ZA // D_ATT))],
        out_specs=pl.BlockSpec((tq, D_ATT), lambda b, i: (b * nq + i, 0)),
        out_shape=jax.ShapeDtypeStruct((bsz * seq, D_ATT), BF16),
        scratch_shapes=([pltpu.VMEM((1, tq), F32)] * (2 * N_HEADS)
                        + [pltpu.VMEM((V_HEAD, tq), F32)] * N_HEADS),
        compiler_params=_params(("parallel", "arbitrary"), 48),
        name="attn_prompt")(q, kcat.reshape(bsz, seq, N_HEADS * HEAD_Q), vt, kmeta, vmeta_t, proj)


def _qabsorb_kernel(q_ref, wukt_ref, qa_ref):
    for h in range(N_HEADS):
        qn = q_ref[:, HEAD_Q * h:HEAD_Q * h + QK_NOPE]
        qa_ref[:, KV_LORA * h:KV_LORA * (h + 1)] = jnp.dot(
            qn, wukt_ref[h], preferred_element_type=F32).astype(BF16)


def _qabsorb(q, wukt):
    m = q.shape[0]
    return pl.pallas_call(
        _qabsorb_kernel,
        out_shape=jax.ShapeDtypeStruct((m, N_HEADS * KV_LORA), BF16),
        name="q_absorb")(q, wukt)


def _decode_kernel(pt_ref, qa_ref, qp_ref, cn_ref, kn_ref, ckv_hbm, kpe_hbm, o_ref,
                   cbuf, kbuf, sem, m_ref, l_ref, acc_ref, *, npg, n_pages):
    b, j = pl.program_id(0), pl.program_id(1)
    nj = pl.num_programs(1)
    t = b * nj + j
    last = pl.num_programs(0) * nj - 1
    slot = t % 2

    def page_copies(step, buf, k):
        pg = pt_ref[(step // nj) * n_pages + (step % nj) * npg + k]
        return (pltpu.make_async_copy(ckv_hbm.at[0, pg], cbuf.at[buf, k], sem.at[buf]),
                pltpu.make_async_copy(kpe_hbm.at[0, pg], kbuf.at[buf, k], sem.at[buf]))

    def start_step(step, buf):
        for k in range(npg):
            for cp in page_copies(step, buf, k):
                cp.start()

    def wait_step(step, buf):
        for k in range(npg):
            for cp in page_copies(step, buf, k):
                cp.wait()

    @pl.when(t == 0)
    def _():
        start_step(t, slot)

    t_next = jnp.minimum(t + 1, last)
    start_step(t_next, 1 - slot)
    wait_step(t, slot)
    c_refs = [cbuf.at[slot, k] for k in range(npg)]
    k_refs = [kbuf.at[slot, k] for k in range(npg)]

    @pl.when(j == 0)
    def _():
        m_ref[...] = jnp.full(m_ref.shape, -jnp.inf, F32)
        l_ref[...] = jnp.zeros(l_ref.shape, F32)
        acc_ref[...] = jnp.zeros(acc_ref.shape, F32)

    qa, qp = qa_ref[...], qp_ref[...]
    nt = (((1,), (1,)), ((), ()))
    page = c_refs[0].shape[0]
    cb = [r[...].astype(BF16) for r in c_refs]
    s = jnp.concatenate(
        [lax.dot_general(qa, cb[k], nt, preferred_element_type=F32)
         + jnp.dot(qp, k_refs[k][...].astype(BF16), preferred_element_type=F32)
         for k in range(npg)], axis=1)
    m_prev = m_ref[...]
    m_new = jnp.maximum(m_prev, jnp.max(s, axis=-1, keepdims=True))
    alpha = jnp.exp2(m_prev - m_new)
    p32 = jnp.exp2(s - m_new)
    p = p32.astype(BF16)
    pv = jnp.dot(p[:, :page], cb[0], preferred_element_type=F32)
    for k in range(1, npg):
        pv = pv + jnp.dot(p[:, page * k:page * (k + 1)], cb[k], preferred_element_type=F32)
    l_ref[...] = alpha * l_ref[...] + jnp.sum(p32, axis=-1, keepdims=True)
    acc_ref[...] = alpha * acc_ref[...] + pv
    m_ref[...] = m_new

    @pl.when(j == pl.num_programs(1) - 1)
    def _():
        cn, kn = cn_ref[...], kn_ref[...]
        s_new = (jnp.sum(qa.astype(F32) * cn, axis=-1, keepdims=True)
                 + jnp.sum(qp.astype(F32) * kn, axis=-1, keepdims=True))
        m_old = m_ref[...]
        m_fin = jnp.maximum(m_old, s_new)
        a = jnp.exp2(m_old - m_fin)
        pn = jnp.exp2(s_new - m_fin)
        l_fin = a * l_ref[...] + pn
        o_ref[...] = (a * acc_ref[...] + pn * cn) / l_fin

    @pl.when(t == last)
    def _():
        wait_step(t_next, 1 - slot)


def _decode(page_table, qa, qp, c_new, k_new, cache_ckv, cache_kpe_t, npg):
    dbs, n_pages = page_table.shape
    page = cache_ckv.shape[2]
    per_b = lambda shape: pl.BlockSpec((None,) + shape, lambda b, j, pt: (b, 0, 0))
    grid_spec = pltpu.PrefetchScalarGridSpec(
        num_scalar_prefetch=1,
        grid=(dbs, n_pages // npg),
        in_specs=[per_b((N_HEADS, KV_LORA)), per_b((N_HEADS, QK_ROPE)),
                  per_b((1, KV_LORA)), per_b((1, QK_ROPE)),
                  pl.BlockSpec(memory_space=pl.ANY), pl.BlockSpec(memory_space=pl.ANY)],
        out_specs=per_b((N_HEADS, KV_LORA)),
        scratch_shapes=[pltpu.VMEM((2, npg, page, KV_LORA), F32),
                        pltpu.VMEM((2, npg, QK_ROPE, page), F32),
                        pltpu.SemaphoreType.DMA((2,)),
                        pltpu.VMEM((N_HEADS, 1), F32), pltpu.VMEM((N_HEADS, 1), F32),
                        pltpu.VMEM((N_HEADS, KV_LORA), F32)])
    return pl.pallas_call(
        functools.partial(_decode_kernel, npg=npg, n_pages=n_pages),
        grid_spec=grid_spec,
        out_shape=jax.ShapeDtypeStruct((dbs, N_HEADS, KV_LORA), F32),
        compiler_params=_params(("arbitrary", "arbitrary"), 48),
        name="attn_decode")(page_table.reshape(-1), qa, qp, c_new, k_new, cache_ckv, cache_kpe_t)


def _ouv_kernel(ol_ref, wuv_ref, za_ref, o_ref):
    za = za_ref[...]
    gate = za * jax.nn.sigmoid(za)
    for h in range(N_HEADS):
        o = jnp.dot(ol_ref[:, KV_LORA * h:KV_LORA * (h + 1)].astype(BF16),
                    wuv_ref[:, V_HEAD * h:V_HEAD * (h + 1)], preferred_element_type=F32)
        o_ref[:, V_HEAD * h:V_HEAD * (h + 1)] = (o * gate[:, V_HEAD * h:V_HEAD * (h + 1)]).astype(BF16)


def _ouv(o_lat, wuv, proj):
    m = o_lat.shape[0]
    const = lambda shape: pl.BlockSpec(shape, lambda i: (0,) * len(shape))
    return pl.pallas_call(
        _ouv_kernel,
        grid=(1,),
        in_specs=[const((m, N_HEADS * KV_LORA)), const((KV_LORA, D_ATT)),
                  pl.BlockSpec((m, D_ATT), lambda i: (0, C_ZA // D_ATT))],
        out_specs=const((m, D_ATT)),
        out_shape=jax.ShapeDtypeStruct((m, D_ATT), BF16),
        name="o_uv")(o_lat, wuv, proj)


def _outproj_kernel(ssm_ref, att_ref, wtop_ref, wbot_ref, x_ref, g_ref, y_ref):
    h = (x_ref[...]
         + jnp.dot(ssm_ref[...], wtop_ref[...], preferred_element_type=F32)
         + jnp.dot(att_ref[...], wbot_ref[...], preferred_element_type=F32))
    y_ref[...] = _rms(h, g_ref[...])


def _outproj(ssm, att, w, x, g, tm):
    m = x.shape[0]
    return pl.pallas_call(
        _outproj_kernel,
        grid=(m // tm,),
        in_specs=[pl.BlockSpec((tm, D_SSM), lambda i: (i, 0)),
                  pl.BlockSpec((tm, D_ATT), lambda i: (i, 0)),
                  pl.BlockSpec((D_SSM, D_MODEL), lambda i: (0, 0)),
                  pl.BlockSpec((D_ATT, D_MODEL), lambda i: (1, 0)),
                  pl.BlockSpec((tm, D_MODEL), lambda i: (i, 0)),
                  pl.BlockSpec((1, D_MODEL), lambda i: (0, 0))],
        out_specs=pl.BlockSpec((tm, D_MODEL), lambda i: (i, 0)),
        out_shape=jax.ShapeDtypeStruct((m, D_MODEL), F32),
        compiler_params=_params(("parallel",), 48),
        name="outproj")(ssm, att, w, w, x, g)


def _slab_major(h):
    return h.reshape(h.shape[0], N_SLABS, LANES).transpose(1, 0, 2)


def _slab_minor(h):
    return h.transpose(1, 0, 2).reshape(h.shape[1], D_STATE)


def kernel(x_prompt, x_sample, cache_ckv, cache_kpe, state_ssm_re, state_ssm_im, page_table, meta_tokens, g_norm, w_in, ssm_a_re, ssm_a_im, ssm_log_dt, ssm_b_re, ssm_b_im, ssm_c_re, ssm_c_im, ssm_d, w_glu, b_glu, g_q_norm, w_uq, g_kv_norm, w_uk, w_uv, w_out, g_final):
    bsz, seq, _ = x_prompt.shape
    dbs = x_sample.shape[0]
    n_pages, page = page_table.shape[1], cache_ckv.shape[2]
    past_len = n_pages * page
    layer = 0

    wi = w_in[layer]
    o_cq, o_ckv, o_kpe, o_za = 2 * D_SSM, 2 * D_SSM + Q_LORA, 2 * D_SSM + Q_LORA + KV_LORA, \
        2 * D_SSM + Q_LORA + KV_LORA + QK_ROPE
    w_in_p = jnp.concatenate(
        [wi[:, o_cq:o_ckv], wi[:, o_kpe:o_za], jnp.zeros((D_MODEL, C_U - C_KPE - QK_ROPE), wi.dtype),
         wi[:, :D_SSM], wi[:, D_SSM:o_cq], wi[:, o_za:], wi[:, o_ckv:o_kpe]], axis=1).astype(BF16)
    w_uq_p = jnp.pad(w_uq[layer].reshape(Q_LORA, N_HEADS, QK_NOPE + QK_ROPE),
                     ((0, 0), (0, 0), (0, HEAD_Q - QK_NOPE - QK_ROPE))
                     ).reshape(Q_LORA, N_HEADS * HEAD_Q).astype(BF16)
    w_uk_flat = w_uk[layer].reshape(KV_LORA, D_ATT).astype(BF16)
    w_uk_t = w_uk[layer].transpose(1, 2, 0).astype(BF16)
    w_uv_flat = w_uv[layer].reshape(KV_LORA, D_ATT).astype(BF16)
    w_glu_b = w_glu[layer].astype(BF16)
    w_out_b = w_out[layer].astype(BF16)
    g_in = g_norm[layer].reshape(1, D_MODEL)
    g_q = g_q_norm[layer].reshape(1, Q_LORA)
    g_kv = g_kv_norm[layer].reshape(1, KV_LORA)
    g_fin = g_final.reshape(1, D_MODEL)
    d_skip = ssm_d[layer].reshape(1, D_SSM)
    bglu = b_glu[layer].reshape(1, D_SSM)

    abr, abi, bbr_t, bbi_t = _ssm_prep(ssm_a_re[layer], ssm_a_im[layer], ssm_log_dt[layer],
                                       ssm_b_re[layer].transpose(0, 2, 1), ssm_b_im[layer].transpose(0, 2, 1))
    bre, bim = _blockdiag_in(bbr_t).astype(BF16), _blockdiag_in(bbi_t).astype(BF16)
    crt, cit = _blockdiag_out(ssm_c_re[layer]).astype(BF16), _blockdiag_out(ssm_c_im[layer]).astype(BF16)
    abr_f, abi_f = abr.reshape(1, D_STATE), abi.reshape(1, D_STATE)
    ssm_w = (bre, bim, crt, cit, d_skip, w_glu_b, bglu)

    pos = jnp.concatenate([jnp.arange(N_META, N_META + seq), jnp.full((dbs,), past_len), jnp.arange(N_META)])
    tabs = _rope_tables(pos)
    tabs_real = tuple(t[:seq] for t in tabs)
    tabs_small = tuple(t[seq:] for t in tabs)

    x_real = x_prompt.reshape(bsz * seq, D_MODEL)
    x_small = jnp.concatenate([x_sample.reshape(dbs, D_MODEL), meta_tokens.astype(x_prompt.dtype)], axis=0)
    n_small = dbs + N_META
    proj_real = _inproj(x_real, g_in, w_in_p, 1024, 768)
    proj_small = _inproj(x_small, g_in, w_in_p, n_small, 768)

    w_uv_t = w_uv_flat.T
    c_real, kr_real, kcat_real, vt_real = _kvprep(proj_real, g_kv, tabs_real, w_uk_flat, w_uv_t, 512, ATT_TILE)
    c_small, kr_small, kcat_small, vt_small = _kvprep(proj_small, g_kv, tabs_small, w_uk_flat, w_uv_t,
                                                      n_small, n_small)
    q_real = _qproj(proj_real, g_q, w_uq_p, tabs_real, 512)
    q_small = _qproj(proj_small, g_q, w_uq_p, tabs_small, n_small)

    proj_meta = jnp.broadcast_to(proj_small[dbs:][None], (bsz, N_META, N_PROJ))
    h_zero = jnp.zeros((N_SLABS, 2 * bsz, LANES), F32)
    _, h_meta = _s5_scan(proj_meta, h_zero, abr_f, abi_f, *ssm_w, tc=N_META)
    ssm_real, h_fin = _s5_scan(proj_real.reshape(bsz, seq, N_PROJ), h_meta, abr_f, abi_f, *ssm_w, tc=64)
    h_fin = _slab_minor(h_fin)
    ssm_smp, hr_s, hi_s = _s5_step(proj_small, state_ssm_re[layer].reshape(dbs, D_STATE),
                                   state_ssm_im[layer].reshape(dbs, D_STATE), abr_f, abi_f, *ssm_w)

    att_real = _attention(q_real, kcat_real, vt_real, kcat_small[dbs:], vt_small[0, :, dbs:], proj_real,
                          bsz, seq, ATT_TILE)

    qs = q_small[:dbs]
    qa = _qabsorb(qs, w_uk_t).reshape(dbs, N_HEADS, KV_LORA)
    qp = qs.reshape(dbs, N_HEADS, HEAD_Q)[:, :, QK_NOPE:QK_NOPE + QK_ROPE]
    c_new = c_small[:dbs].reshape(dbs, 1, KV_LORA)
    k_new = kr_small[:dbs, :QK_ROPE].reshape(dbs, 1, QK_ROPE)
    o_lat = _decode(page_table, qa, qp, c_new, k_new, cache_ckv, jnp.swapaxes(cache_kpe, 2, 3), 32)
    att_smp = _ouv(o_lat.reshape(dbs, N_HEADS * KV_LORA), w_uv_flat, proj_small)

    y_prompt = _outproj(ssm_real.reshape(bsz * seq, D_SSM), att_real, w_out_b, x_real, g_fin, 256)
    y_sample = _outproj(ssm_smp, att_smp, w_out_b, x_sample.reshape(dbs, D_MODEL), g_fin, dbs)

    c_meta, k_meta = c_small[dbs:], kr_small[dbs:, :QK_ROPE]
    ckv_p = jnp.concatenate([jnp.broadcast_to(c_meta[None], (bsz, N_META, KV_LORA)),
                             c_real.reshape(bsz, seq, KV_LORA)], axis=1)
    kpe_p = jnp.concatenate([jnp.broadcast_to(k_meta[None], (bsz, N_META, QK_ROPE)),
                             kr_real[:, :QK_ROPE].reshape(bsz, seq, QK_ROPE)], axis=1)
    return (y_prompt.reshape(bsz, seq, D_MODEL),
            y_sample.reshape(dbs, 1, D_MODEL),
            ckv_p[None], kpe_p[None],
            h_fin[:bsz].reshape(1, bsz, N_GROUPS, N_STATE),
            h_fin[bsz:].reshape(1, bsz, N_GROUPS, N_STATE),
            c_small[:dbs].reshape(1, dbs, 1, KV_LORA),
            kr_small[:dbs, :QK_ROPE].reshape(1, dbs, 1, QK_ROPE),
            hr_s.reshape(1, dbs, N_GROUPS, N_STATE),
            hi_s.reshape(1, dbs, N_GROUPS, N_STATE))
```

```python
import functools
import math

import numpy as np
import jax
import jax.numpy as jnp
from jax import lax
from jax.experimental import pallas as pl
from jax.experimental.pallas import tpu as pltpu

F32 = jnp.float32
BF16 = jnp.bfloat16

D_MODEL = 2048
N_META = 16
D_SSM = 1024
SSM_GROUP = 16
N_GROUPS = 64
N_STATE = 64
D_STATE = N_GROUPS * N_STATE
D_ATT = 1024
V_HEAD = 128
N_HEADS = 8
QK_NOPE = 128
QK_ROPE = 64
KV_LORA = 512
Q_LORA = 768
ROPE_THETA = 10000.0
SOFTMAX_SCALE = (QK_NOPE + QK_ROPE) ** -0.5
Q_SCALE = SOFTMAX_SCALE * math.log2(math.e)
EPS = 1e-6

LANES = 128
SUBLANES = 8
MIB = 1024 * 1024

C_CQ = 0
C_KPE = 768
C_U = 1024
C_ZS = 2048
C_ZA = 3072
C_CKV = 4096
N_PROJ = 4608
HEAD_Q = 256
N_SLABS = D_STATE // LANES
SLAB_GROUP = 8
ATT_TILE = 512
SSM_KT = 256
SSM_NT = SSM_KT * N_STATE // SSM_GROUP


def _params(sem, vmem_mib=None):
    return pltpu.CompilerParams(
        dimension_semantics=sem,
        vmem_limit_bytes=None if vmem_mib is None else vmem_mib * MIB)


def _rms(x, g):
    return x * lax.rsqrt(jnp.mean(x * x, axis=-1, keepdims=True) + EPS) * g


def _rope128(x, cos_t, sin_lo, sin_hi):
    return (x * cos_t + pltpu.roll(x, LANES - QK_ROPE // 2, axis=1) * sin_lo
            + pltpu.roll(x, QK_ROPE // 2, axis=1) * sin_hi)


def _rope_table_kernel(pos_ref, inv_ref, cos_ref, slo_ref, shi_ref):
    ang = pos_ref[...] * inv_ref[...]
    c, s = jnp.cos(ang), jnp.sin(ang)
    lane = lax.broadcasted_iota(jnp.int32, ang.shape, 1)
    half = QK_ROPE // 2
    cos_ref[...] = jnp.where(lane < QK_ROPE, c, 0.0)
    slo_ref[...] = jnp.where(lane < half, -s, 0.0)
    shi_ref[...] = jnp.where((lane >= half) & (lane < QK_ROPE), s, 0.0)


def _rope_tables(pos):
    n = pos.shape[0]
    half = QK_ROPE // 2
    inv = ROPE_THETA ** (-(np.arange(LANES) % half).astype(np.float64) / half)
    inv = jnp.asarray(inv.astype(np.float32)).reshape(1, LANES)
    shp = jax.ShapeDtypeStruct((n, LANES), F32)
    return pl.pallas_call(_rope_table_kernel, out_shape=(shp, shp, shp), name="rope_tables")(
        pos.reshape(n, 1).astype(F32), inv)


def _win_prep_kernel(w_ref, o_ref):
    w = w_ref[...]
    o_cq, o_ckv, o_kpe = 2 * D_SSM, 2 * D_SSM + Q_LORA, 2 * D_SSM + Q_LORA + KV_LORA
    o_za = o_kpe + QK_ROPE
    rows = w.shape[0]
    o_ref[:, C_CQ:C_CQ + Q_LORA] = w[:, o_cq:o_ckv].astype(BF16)
    o_ref[:, C_KPE:C_U] = jnp.concatenate(
        [w[:, o_kpe:o_za], jnp.zeros((rows, C_U - C_KPE - QK_ROPE), w.dtype)], axis=1).astype(BF16)
    o_ref[:, C_U:C_ZA] = w[:, :o_cq].astype(BF16)
    o_ref[:, C_ZA:C_CKV] = w[:, o_za:].astype(BF16)
    o_ref[:, C_CKV:] = w[:, o_ckv:o_kpe].astype(BF16)


def _win_prep(w, tr):
    k, n = w.shape
    return pl.pallas_call(
        _win_prep_kernel,
        grid=(k // tr,),
        in_specs=[pl.BlockSpec((tr, n), lambda i: (i, 0))],
        out_specs=pl.BlockSpec((tr, N_PROJ), lambda i: (i, 0)),
        out_shape=jax.ShapeDtypeStruct((k, N_PROJ), BF16),
        compiler_params=_params(("parallel",)),
        name="win_prep")(w)


def _inproj_kernel(x_ref, g_ref, w_ref, o_ref, xn_ref):
    @pl.when(pl.program_id(1) == 0)
    def _():
        xn_ref[...] = _rms(x_ref[...], g_ref[...]).astype(BF16)

    o_ref[...] = jnp.dot(xn_ref[...], w_ref[...], preferred_element_type=F32)


def _inproj(x, g, w, tm, tn):
    m = x.shape[0]
    return pl.pallas_call(
        _inproj_kernel,
        grid=(m // tm, N_PROJ // tn),
        in_specs=[pl.BlockSpec((tm, D_MODEL), lambda i, j: (i, 0)),
                  pl.BlockSpec((1, D_MODEL), lambda i, j: (0, 0)),
                  pl.BlockSpec((D_MODEL, tn), lambda i, j: (0, j))],
        out_specs=pl.BlockSpec((tm, tn), lambda i, j: (i, j)),
        out_shape=jax.ShapeDtypeStruct((m, N_PROJ), F32),
        scratch_shapes=[pltpu.VMEM((tm, D_MODEL), BF16)],
        compiler_params=_params(("parallel", "arbitrary"), 48),
        name="inproj")(x, g, w)


def _kvprep_kernel(ckv_ref, kpe_ref, g_ref, cos_ref, slo_ref, shi_ref, wuk_ref, wuvt_ref,
                   c_ref, kr_ref, kcat_ref, vt_ref, *, tk):
    c = _rms(ckv_ref[...], g_ref[...])
    c_ref[...] = c
    cb = c.astype(BF16)
    kr = _rope128(kpe_ref[...], cos_ref[...], slo_ref[...], shi_ref[...])
    kr_ref[...] = kr
    krb = kr.astype(BF16)
    knope = jnp.dot(cb, wuk_ref[...], preferred_element_type=F32).astype(BF16)
    for h in range(N_HEADS):
        kcat_ref[:, HEAD_Q * h:HEAD_Q * h + QK_NOPE] = knope[:, QK_NOPE * h:QK_NOPE * (h + 1)]
        kcat_ref[:, HEAD_Q * h + QK_NOPE:HEAD_Q * (h + 1)] = krb
    vt = lax.dot_general(wuvt_ref[...], cb, (((1,), (1,)), ((), ())),
                         preferred_element_type=F32).astype(BF16)
    for kb in range(vt_ref.shape[0]):
        vt_ref[kb] = vt[:, tk * kb:tk * (kb + 1)]


def _kvprep(proj, g, tabs, wuk, wuvt, tm, tk):
    m = proj.shape[0]
    nt = tabs[0].shape[0] // tm
    tab = pl.BlockSpec((tm, LANES), lambda i: (i % nt, 0))
    row = lambda w: pl.BlockSpec((tm, w), lambda i: (i, 0))
    return pl.pallas_call(
        functools.partial(_kvprep_kernel, tk=tk),
        grid=(m // tm,),
        in_specs=[pl.BlockSpec((tm, KV_LORA), lambda i: (i, C_CKV // KV_LORA)),
                  pl.BlockSpec((tm, LANES), lambda i: (i, C_KPE // LANES)),
                  pl.BlockSpec((1, KV_LORA), lambda i: (0, 0)),
                  tab, tab, tab,
                  pl.BlockSpec((KV_LORA, D_ATT), lambda i: (0, 0)),
                  pl.BlockSpec((D_ATT, KV_LORA), lambda i: (0, 0))],
        out_specs=(row(KV_LORA), row(LANES), row(N_HEADS * HEAD_Q),
                   pl.BlockSpec((tm // tk, D_ATT, tk), lambda i: (i, 0, 0))),
        out_shape=(jax.ShapeDtypeStruct((m, KV_LORA), F32),
                   jax.ShapeDtypeStruct((m, LANES), F32),
                   jax.ShapeDtypeStruct((m, N_HEADS * HEAD_Q), BF16),
                   jax.ShapeDtypeStruct((m // tk, D_ATT, tk), BF16)),
        compiler_params=_params(("parallel",)),
        name="kvprep")(proj, proj, g, *tabs, wuk, wuvt)


def _qproj_kernel(cq_ref, g_ref, w_ref, cos_ref, slo_ref, shi_ref, q_ref):
    cqn = _rms(cq_ref[...], g_ref[...]).astype(BF16)
    q = jnp.dot(cqn, w_ref[...], preferred_element_type=F32)
    cos_t, slo, shi = cos_ref[...], slo_ref[...], shi_ref[...]
    for h in range(N_HEADS):
        lo = HEAD_Q * h
        q_ref[:, lo:lo + QK_NOPE] = (q[:, lo:lo + QK_NOPE] * Q_SCALE).astype(BF16)
        pe = _rope128(q[:, lo + QK_NOPE:lo + HEAD_Q], cos_t, slo, shi)
        q_ref[:, lo + QK_NOPE:lo + HEAD_Q] = (pe * Q_SCALE).astype(BF16)


def _qproj(proj, g, w, tabs, tm):
    m = proj.shape[0]
    nt = tabs[0].shape[0] // tm
    tab = pl.BlockSpec((tm, LANES), lambda i: (i % nt, 0))
    return pl.pallas_call(
        _qproj_kernel,
        grid=(m // tm,),
        in_specs=[pl.BlockSpec((tm, Q_LORA), lambda i: (i, C_CQ // Q_LORA)),
                  pl.BlockSpec((1, Q_LORA), lambda i: (0, 0)),
                  pl.BlockSpec((Q_LORA, N_HEADS * HEAD_Q), lambda i: (0, 0)),
                  tab, tab, tab],
        out_specs=pl.BlockSpec((tm, N_HEADS * HEAD_Q), lambda i: (i, 0)),
        out_shape=jax.ShapeDtypeStruct((m, N_HEADS * HEAD_Q), BF16),
        compiler_params=_params(("parallel",)),
        name="qproj")(proj, g, w, *tabs)


def _ssm_prep_kernel(are_ref, aim_ref, ldt_ref, bre_ref, bim_ref, cre_ref, cim_ref,
                     abr_ref, abi_ref, wbr_ref, wbi_ref, wcr_ref, wci_ref):
    dt = jnp.exp(ldt_ref[...])
    ar, ai = are_ref[...], aim_ref[...]
    mag = jnp.exp(dt * ar)
    abr, abi = mag * jnp.cos(dt * ai), mag * jnp.sin(dt * ai)
    den = ar * ar + ai * ai
    nr, ni = abr - 1.0, abi
    f_re = (nr * ar + ni * ai) / den
    f_im = (ni * ar - nr * ai) / den
    br, bi = bre_ref[...], bim_ref[...]
    abr_ref[...] = abr
    abi_ref[...] = abi
    bbr = f_re * br - f_im * bi
    bbi = f_re * bi + f_im * br

    gpt = SSM_KT // SSM_GROUP
    n_kt = D_SSM // SSM_KT

    def spread(x, width, period):
        sel = (lax.broadcasted_iota(jnp.int32, (period, width), 1) % period
               == lax.broadcasted_iota(jnp.int32, (period, width), 0))
        return jnp.dot(x.astype(BF16), jnp.where(sel, 1.0, 0.0).astype(BF16), preferred_element_type=F32)

    r_in = lax.broadcasted_iota(jnp.int32, (SSM_KT, SSM_NT), 0) // SSM_GROUP
    c_in = lax.broadcasted_iota(jnp.int32, (SSM_KT, SSM_NT), 1) // N_STATE
    r_out = lax.broadcasted_iota(jnp.int32, (SSM_NT, SSM_KT), 0) // N_STATE
    c_out = lax.broadcasted_iota(jnp.int32, (SSM_NT, SSM_KT), 1) // SSM_GROUP
    for kt in range(n_kt):
        gs = slice(gpt * kt, gpt * (kt + 1))
        for src, dst in ((bbr, wbr_ref), (bbi, wbi_ref)):
            x = src[gs].reshape(SSM_KT, N_STATE)
            dst[kt] = jnp.where(r_in == c_in, spread(x, SSM_NT, N_STATE), 0.0).astype(BF16)
        for src, dst in ((cre_ref, wcr_ref), (cim_ref, wci_ref)):
            x = src[gs].reshape(SSM_NT, SSM_GROUP)
            dst[kt] = jnp.where(r_out == c_out, spread(x, SSM_KT, SSM_GROUP), 0.0).astype(BF16)


def _ssm_prep(a_re, a_im, log_dt, b_re, b_im, c_re, c_im):
    gn = jax.ShapeDtypeStruct((N_GROUPS, 1, N_STATE), F32)
    w_in_shape = jax.ShapeDtypeStruct((D_SSM // SSM_KT, SSM_KT, SSM_NT), BF16)
    w_out_shape = jax.ShapeDtypeStruct((D_SSM // SSM_KT, SSM_NT, SSM_KT), BF16)
    return pl.pallas_call(
        _ssm_prep_kernel, out_shape=(gn, gn, w_in_shape, w_in_shape, w_out_shape, w_out_shape),
        name="ssm_prep")(
            a_re.reshape(N_GROUPS, 1, N_STATE), a_im.reshape(N_GROUPS, 1, N_STATE),
            log_dt.reshape(N_GROUPS, 1, 1), b_re.transpose(0, 2, 1), b_im.transpose(0, 2, 1),
            c_re.transpose(0, 2, 1), c_im.transpose(0, 2, 1))


def _ssm_epilogue(y, uf, zs, d, wglu, bglu):
    ys = jax.nn.gelu(y + d * uf)
    gate = jax.nn.sigmoid(jnp.dot(ys.astype(BF16), wglu, preferred_element_type=F32) + bglu)
    return ys * gate * (zs * jax.nn.sigmoid(zs))


def _s5_scan_kernel(u_ref, zs_ref, h0_ref, abr_ref, abi_ref, bre_ref, bim_ref, crt_ref, cit_ref,
                    d_ref, wglu_ref, bglu_ref, out_ref, hfin_ref, buf_ref, hst_ref, y_ref,
                    *, nb, tc, pitch):
    i = pl.program_id(0)

    @pl.when(i == 0)
    def _():
        hst_ref[...] = h0_ref[...]

    uf = u_ref[...].reshape(nb * tc, D_SSM)
    ub = uf.astype(BF16)
    n_kt = D_SSM // SSM_KT
    slabs_per_kt = SSM_NT // LANES
    for kt in range(n_kt):
        lhs = ub[:, SSM_KT * kt:SSM_KT * (kt + 1)]
        for ri, w_ref in ((0, bre_ref), (1, bim_ref)):
            res = jnp.dot(lhs, w_ref[kt], preferred_element_type=F32)
            for s in range(slabs_per_kt):
                for b in range(nb):
                    r0 = (ri * nb + b) * pitch
                    buf_ref[kt * slabs_per_kt + s, r0:r0 + tc, :] = (
                        res[b * tc:(b + 1) * tc, LANES * s:LANES * (s + 1)])

    row = lax.broadcasted_iota(jnp.int32, (2 * nb, LANES), 0)

    def scan_group(g, carry):
        base = g * SLAB_GROUP
        a1, a2, h = [], [], []
        for s in range(SLAB_GROUP):
            a1.append(jnp.broadcast_to(abr_ref[base + s], (2 * nb, LANES)))
            ai = jnp.broadcast_to(abi_ref[base + s], (2 * nb, LANES))
            a2.append(jnp.where(row < nb, -ai, ai))
            h.append(hst_ref[base + s])
        for t in range(tc):
            for s in range(SLAB_GROUP):
                bu = buf_ref[base + s, pl.ds(t, 2 * nb, stride=pitch), :]
                h[s] = a1[s] * h[s] + a2[s] * pltpu.roll(h[s], nb, axis=0) + bu
                buf_ref[base + s, pl.ds(t, 2 * nb, stride=pitch), :] = h[s]
        for s in range(SLAB_GROUP):
            hst_ref[base + s] = h[s]
        return carry

    lax.fori_loop(0, N_SLABS // SLAB_GROUP, scan_group, 0)

    def states(jt, ri):
        rows = []
        for b in range(nb):
            r0 = (ri * nb + b) * pitch
            rows.append(jnp.concatenate(
                [buf_ref[jt * slabs_per_kt + s, r0:r0 + tc, :] for s in range(slabs_per_kt)], axis=1))
        return jnp.concatenate(rows, axis=0).astype(BF16)

    for jt in range(n_kt):
        y_ref[:, SSM_KT * jt:SSM_KT * (jt + 1)] = (
            jnp.dot(states(jt, 0), crt_ref[jt], preferred_element_type=F32)
            - jnp.dot(states(jt, 1), cit_ref[jt], preferred_element_type=F32))

    zs = zs_ref[...].reshape(nb * tc, D_SSM)
    out = _ssm_epilogue(y_ref[...], uf, zs, d_ref[...], wglu_ref[...], bglu_ref[...])
    out_ref[...] = out.reshape(nb, tc, D_SSM).astype(BF16)

    @pl.when(i == pl.num_programs(0) - 1)
    def _():
        hfin_ref[...] = hst_ref[...]


def _s5_scan(proj3, h0, abr, abi, bre, bim, crt, cit, d, wglu, bglu, tc):
    nb, t_len = proj3.shape[0], proj3.shape[1]
    pitch = tc + SUBLANES
    const = lambda shape: pl.BlockSpec(shape, lambda i: (0,) * len(shape))
    kern = functools.partial(_s5_scan_kernel, nb=nb, tc=tc, pitch=pitch)
    return pl.pallas_call(
        kern,
        grid=(t_len // tc,),
        in_specs=[pl.BlockSpec((nb, tc, D_SSM), lambda i: (0, i, C_U // D_SSM)),
                  pl.BlockSpec((nb, tc, D_SSM), lambda i: (0, i, C_ZS // D_SSM)),
                  const((N_SLABS, 2 * nb, LANES)),
                  const((N_SLABS, 1, LANES)), const((N_SLABS, 1, LANES)),
                  const(bre.shape), const(bim.shape), const(crt.shape), const(cit.shape),
                  const((1, D_SSM)), const((D_SSM, D_SSM)), const((1, D_SSM))],
        out_specs=(pl.BlockSpec((nb, tc, D_SSM), lambda i: (0, i, 0)),
                   const((N_SLABS, 2 * nb, LANES))),
        out_shape=(jax.ShapeDtypeStruct((nb, t_len, D_SSM), BF16),
                   jax.ShapeDtypeStruct((N_SLABS, 2 * nb, LANES), F32)),
        scratch_shapes=[pltpu.VMEM((N_SLABS, 2 * nb * pitch, LANES), F32),
                        pltpu.VMEM((N_SLABS, 2 * nb, LANES), F32),
                        pltpu.VMEM((nb * tc, D_SSM), F32)],
        compiler_params=_params(("arbitrary",), 56),
        name="s5_scan")(proj3, proj3, h0, abr.reshape(N_SLABS, 1, LANES), abi.reshape(N_SLABS, 1, LANES),
                        bre, bim, crt, cit, d, wglu, bglu)


def _s5_step_kernel(u_ref, zs_ref, h0r_ref, h0i_ref, abr_ref, abi_ref, bre_ref, bim_ref, crt_ref,
                    cit_ref, d_ref, wglu_ref, bglu_ref, out_ref, hr_ref, hi_ref, y_ref):
    uf = u_ref[...]
    ub = uf.astype(BF16)
    n_kt = D_SSM // SSM_KT
    for kt in range(n_kt):
        lhs = ub[:, SSM_KT * kt:SSM_KT * (kt + 1)]
        sl = slice(SSM_NT * kt, SSM_NT * (kt + 1))
        ar, ai = abr_ref[:, sl], abi_ref[:, sl]
        h0r, h0i = h0r_ref[:, sl], h0i_ref[:, sl]
        hr_ref[:, sl] = ar * h0r - ai * h0i + jnp.dot(lhs, bre_ref[kt], preferred_element_type=F32)
        hi_ref[:, sl] = ar * h0i + ai * h0r + jnp.dot(lhs, bim_ref[kt], preferred_element_type=F32)
    for jt in range(n_kt):
        sl = slice(SSM_NT * jt, SSM_NT * (jt + 1))
        y_ref[:, SSM_KT * jt:SSM_KT * (jt + 1)] = (
            jnp.dot(hr_ref[:, sl].astype(BF16), crt_ref[jt], preferred_element_type=F32)
            - jnp.dot(hi_ref[:, sl].astype(BF16), cit_ref[jt], preferred_element_type=F32))
    out = _ssm_epilogue(y_ref[...], uf, zs_ref[...], d_ref[...], wglu_ref[...], bglu_ref[...])
    out_ref[...] = out.astype(BF16)


def _s5_step(proj, h0r, h0i, abr, abi, bre, bim, crt, cit, d, wglu, bglu):
    m = h0r.shape[0]
    const = lambda shape: pl.BlockSpec(shape, lambda i: (0,) * len(shape))
    return pl.pallas_call(
        _s5_step_kernel,
        grid=(1,),
        in_specs=[pl.BlockSpec((m, D_SSM), lambda i: (0, C_U // D_SSM)),
                  pl.BlockSpec((m, D_SSM), lambda i: (0, C_ZS // D_SSM)),
                  const((m, D_STATE)), const((m, D_STATE)),
                  const((1, D_STATE)), const((1, D_STATE)),
                  const(bre.shape), const(bim.shape), const(crt.shape), const(cit.shape),
                  const((1, D_SSM)), const((D_SSM, D_SSM)), const((1, D_SSM))],
        out_specs=(const((m, D_SSM)), const((m, D_STATE)), const((m, D_STATE))),
        out_shape=(jax.ShapeDtypeStruct((m, D_SSM), BF16),
                   jax.ShapeDtypeStruct((m, D_STATE), F32),
                   jax.ShapeDtypeStruct((m, D_STATE), F32)),
        scratch_shapes=[pltpu.VMEM((m, D_SSM), F32)],
        compiler_params=_params(("arbitrary",), 48),
        name="s5_step")(proj, proj, h0r, h0i, abr, abi, bre, bim, crt, cit, d, wglu, bglu)


def _attn_kernel(q_ref, k_ref, vt_ref, km_ref, vmt_ref, za_ref, o_ref, *scratch, tq):
    m_ref, l_ref, acc_ref = (scratch[:N_HEADS], scratch[N_HEADS:2 * N_HEADS], scratch[2 * N_HEADS:])
    qi = pl.program_id(1)
    nt = (((1,), (1,)), ((), ()))
    heads = [(slice(HEAD_Q * h, HEAD_Q * (h + 1)), slice(V_HEAD * h, V_HEAD * (h + 1)))
             for h in range(N_HEADS)]

    for h, (qs, vs) in enumerate(heads):
        s = lax.dot_general(km_ref[:, qs], q_ref[:, qs], nt, preferred_element_type=F32)
        m = jnp.max(s, axis=0, keepdims=True)
        p = jnp.exp2(s - m)
        m_ref[h][...] = m
        l_ref[h][...] = jnp.sum(p, axis=0, keepdims=True)
        acc_ref[h][...] = jnp.dot(vmt_ref[vs, :], p.astype(BF16), preferred_element_type=F32)

    def kv_block(j, masked):
        off = pl.multiple_of(j * tq, tq)
        if masked:
            key = lax.broadcasted_iota(jnp.int32, (tq, tq), 0)
            qry = lax.broadcasted_iota(jnp.int32, (tq, tq), 1)
            keep = key <= qry
        for h, (qs, vs) in enumerate(heads):
            s = lax.dot_general(k_ref[pl.ds(off, tq), qs], q_ref[:, qs], nt,
                                preferred_element_type=F32)
            if masked:
                s = jnp.where(keep, s, -jnp.inf)
            m_prev = m_ref[h][...]
            m_new = jnp.maximum(m_prev, jnp.max(s, axis=0, keepdims=True))
            alpha = jnp.exp2(m_prev - m_new)
            p = jnp.exp2(s - m_new)
            l_ref[h][...] = alpha * l_ref[h][...] + jnp.sum(p, axis=0, keepdims=True)
            acc_ref[h][...] = alpha * acc_ref[h][...] + jnp.dot(
                vt_ref[j, vs, :], p.astype(BF16), preferred_element_type=F32)
            m_ref[h][...] = m_new

    def full_block(j, carry):
        kv_block(j, False)
        return carry

    lax.fori_loop(0, qi, full_block, 0)
    kv_block(qi, True)
    za = za_ref[...]
    gate = za * jax.nn.sigmoid(za)
    for h, (qs, vs) in enumerate(heads):
        o = (acc_ref[h][...] / l_ref[h][...]).T
        o_ref[:, vs] = (o * gate[:, vs]).astype(BF16)


def _attention(q, kcat, vt, kmeta, vmeta_t, proj, bsz, seq, tq):
    nq = seq // tq
    kern = functools.partial(_attn_kernel, tq=tq)
    return pl.pallas_call(
        kern,
        grid=(bsz, nq),
        in_specs=[pl.BlockSpec((tq, N_HEADS * HEAD_Q), lambda b, i: (b * nq + i, 0)),
                  pl.BlockSpec((None, seq, N_HEADS * HEAD_Q), lambda b, i: (b, 0, 0)),
                  pl.BlockSpec((nq, D_ATT, tq), lambda b, i: (b, 0, 0)),
                  pl.BlockSpec((N_META, N_HEADS * HEAD_Q), lambda b, i: (0, 0)),
                  pl.BlockSpec((D_ATT, N_META), lambda b, i: (0, 0)),
                  pl.BlockSpec((tq, D_ATT), lambda b, i: (b * nq + i, C_ZA // D_ATT))],
        out_specs=pl.BlockSpec((tq, D_ATT), lambda b, i: (b * nq + i, 0)),
        out_shape=jax.ShapeDtypeStruct((bsz * seq, D_ATT), BF16),
        scratch_shapes=([pltpu.VMEM((1, tq), F32)] * (2 * N_HEADS)
                        + [pltpu.VMEM((V_HEAD, tq), F32)] * N_HEADS),
        compiler_params=_params(("parallel", "arbitrary"), 48),
        name="attn_prompt")(q, kcat.reshape(bsz, seq, N_HEADS * HEAD_Q), vt, kmeta, vmeta_t, proj)


def _qabsorb_kernel(q_ref, wukt_ref, qa_ref):
    for h in range(N_HEADS):
        qn = q_ref[:, HEAD_Q * h:HEAD_Q * h + QK_NOPE]
        qa_ref[:, KV_LORA * h:KV_LORA * (h + 1)] = jnp.dot(
            qn, wukt_ref[h], preferred_element_type=F32).astype(BF16)


def _qabsorb(q, wukt):
    m = q.shape[0]
    return pl.pallas_call(
        _qabsorb_kernel,
        out_shape=jax.ShapeDtypeStruct((m, N_HEADS * KV_LORA), BF16),
        name="q_absorb")(q, wukt)


def _decode_kernel(pt_ref, qa_ref, qp_ref, cn_ref, kn_ref, ckv_hbm, kpe_hbm, o_ref,
                   cbuf, kbuf, sem, m_ref, l_ref, acc_ref, *, npg, n_pages):
    b, j = pl.program_id(0), pl.program_id(1)
    nj = pl.num_programs(1)
    t = b * nj + j
    last = pl.num_programs(0) * nj - 1
    slot = t % 2

    def page_copies(step, buf, k):
        pg = pt_ref[(step // nj) * n_pages + (step % nj) * npg + k]
        return (pltpu.make_async_copy(ckv_hbm.at[0, pg], cbuf.at[buf, k], sem.at[buf]),
                pltpu.make_async_copy(kpe_hbm.at[0, pg], kbuf.at[buf, k], sem.at[buf]))

    def start_step(step, buf):
        for k in range(npg):
            for cp in page_copies(step, buf, k):
                cp.start()

    def wait_step(step, buf):
        for k in range(npg):
            for cp in page_copies(step, buf, k):
                cp.wait()

    @pl.when(t == 0)
    def _():
        start_step(t, slot)

    t_next = jnp.minimum(t + 1, last)
    start_step(t_next, 1 - slot)
    wait_step(t, slot)
    c_refs = [cbuf.at[slot, k] for k in range(npg)]
    k_refs = [kbuf.at[slot, k] for k in range(npg)]

    @pl.when(j == 0)
    def _():
        m_ref[...] = jnp.full(m_ref.shape, -jnp.inf, F32)
        l_ref[...] = jnp.zeros(l_ref.shape, F32)
        acc_ref[...] = jnp.zeros(acc_ref.shape, F32)

    qa, qp = qa_ref[...], qp_ref[...]
    nt = (((1,), (1,)), ((), ()))
    page = c_refs[0].shape[0]
    cb = [r[...].astype(BF16) for r in c_refs]
    s = jnp.concatenate(
        [lax.dot_general(qa, cb[k], nt, preferred_element_type=F32)
         + jnp.dot(qp, k_refs[k][...].astype(BF16), preferred_element_type=F32)
         for k in range(npg)], axis=1)
    m_prev = m_ref[...]
    m_new = jnp.maximum(m_prev, jnp.max(s, axis=-1, keepdims=True))
    alpha = jnp.exp2(m_prev - m_new)
    p32 = jnp.exp2(s - m_new)
    p = p32.astype(BF16)
    pv = jnp.dot(p[:, :page], cb[0], preferred_element_type=F32)
    for k in range(1, npg):
        pv = pv + jnp.dot(p[:, page * k:page * (k + 1)], cb[k], preferred_element_type=F32)
    l_ref[...] = alpha * l_ref[...] + jnp.sum(p32, axis=-1, keepdims=True)
    acc_ref[...] = alpha * acc_ref[...] + pv
    m_ref[...] = m_new

    @pl.when(j == pl.num_programs(1) - 1)
    def _():
        cn, kn = cn_ref[...], kn_ref[...]
        s_new = (jnp.sum(qa.astype(F32) * cn, axis=-1, keepdims=True)
                 + jnp.sum(qp.astype(F32) * kn, axis=-1, keepdims=True))
        m_old = m_ref[...]
        m_fin = jnp.maximum(m_old, s_new)
        a = jnp.exp2(m_old - m_fin)
        pn = jnp.exp2(s_new - m_fin)
        l_fin = a * l_ref[...] + pn
        o_ref[...] = (a * acc_ref[...] + pn * cn) / l_fin

    @pl.when(t == last)
    def _():
        wait_step(t_next, 1 - slot)


def _decode(page_table, qa, qp, c_new, k_new, cache_ckv, cache_kpe_t, npg):
    dbs, n_pages = page_table.shape
    page = cache_ckv.shape[2]
    per_b = lambda shape: pl.BlockSpec((None,) + shape, lambda b, j, pt: (b, 0, 0))
    grid_spec = pltpu.PrefetchScalarGridSpec(
        num_scalar_prefetch=1,
        grid=(dbs, n_pages // npg),
        in_specs=[per_b((N_HEADS, KV_LORA)), per_b((N_HEADS, QK_ROPE)),
                  per_b((1, KV_LORA)), per_b((1, QK_ROPE)),
                  pl.BlockSpec(memory_space=pl.ANY), pl.BlockSpec(memory_space=pl.ANY)],
        out_specs=per_b((N_HEADS, KV_LORA)),
        scratch_shapes=[pltpu.VMEM((2, npg, page, KV_LORA), F32),
                        pltpu.VMEM((2, npg, QK_ROPE, page), F32),
                        pltpu.SemaphoreType.DMA((2,)),
                        pltpu.VMEM((N_HEADS, 1), F32), pltpu.VMEM((N_HEADS, 1), F32),
                        pltpu.VMEM((N_HEADS, KV_LORA), F32)])
    return pl.pallas_call(
        functools.partial(_decode_kernel, npg=npg, n_pages=n_pages),
        grid_spec=grid_spec,
        out_shape=jax.ShapeDtypeStruct((dbs, N_HEADS, KV_LORA), F32),
        compiler_params=_params(("arbitrary", "arbitrary"), 48),
        name="attn_decode")(page_table.reshape(-1), qa, qp, c_new, k_new, cache_ckv, cache_kpe_t)


def _ouv_kernel(ol_ref, wuv_ref, za_ref, o_ref):
    za = za_ref[...]
    gate = za * jax.nn.sigmoid(za)
    for h in range(N_HEADS):
        o = jnp.dot(ol_ref[:, KV_LORA * h:KV_LORA * (h + 1)].astype(BF16),
                    wuv_ref[:, V_HEAD * h:V_HEAD * (h + 1)], preferred_element_type=F32)
        o_ref[:, V_HEAD * h:V_HEAD * (h + 1)] = (o * gate[:, V_HEAD * h:V_HEAD * (h + 1)]).astype(BF16)


def _ouv(o_lat, wuv, proj):
    m = o_lat.shape[0]
    const = lambda shape: pl.BlockSpec(shape, lambda i: (0,) * len(shape))
    return pl.pallas_call(
        _ouv_kernel,
        grid=(1,),
        in_specs=[const((m, N_HEADS * KV_LORA)), const((KV_LORA, D_ATT)),
                  pl.BlockSpec((m, D_ATT), lambda i: (0, C_ZA // D_ATT))],
        out_specs=const((m, D_ATT)),
        out_shape=jax.ShapeDtypeStruct((m, D_ATT), BF16),
        name="o_uv")(o_lat, wuv, proj)


def _outproj_kernel(ssm_ref, att_ref, wtop_ref, wbot_ref, x_ref, g_ref, y_ref):
    h = (x_ref[...]
         + jnp.dot(ssm_ref[...], wtop_ref[...], preferred_element_type=F32)
         + jnp.dot(att_ref[...], wbot_ref[...], preferred_element_type=F32))
    y_ref[...] = _rms(h, g_ref[...])


def _outproj(ssm, att, w, x, g, tm):
    m = x.shape[0]
    return pl.pallas_call(
        _outproj_kernel,
        grid=(m // tm,),
        in_specs=[pl.BlockSpec((tm, D_SSM), lambda i: (i, 0)),
                  pl.BlockSpec((tm, D_ATT), lambda i: (i, 0)),
                  pl.BlockSpec((D_SSM, D_MODEL), lambda i: (0, 0)),
                  pl.BlockSpec((D_ATT, D_MODEL), lambda i: (1, 0)),
                  pl.BlockSpec((tm, D_MODEL), lambda i: (i, 0)),
                  pl.BlockSpec((1, D_MODEL), lambda i: (0, 0))],
        out_specs=pl.BlockSpec((tm, D_MODEL), lambda i: (i, 0)),
        out_shape=jax.ShapeDtypeStruct((m, D_MODEL), F32),
        compiler_params=_params(("parallel",), 48),
        name="outproj")(ssm, att, w, w, x, g)


def _slab_minor(h):
    return h.transpose(1, 0, 2).reshape(h.shape[1], D_STATE)


def kernel(x_prompt, x_sample, cache_ckv, cache_kpe, state_ssm_re, state_ssm_im, page_table, meta_tokens, g_norm, w_in, ssm_a_re, ssm_a_im, ssm_log_dt, ssm_b_re, ssm_b_im, ssm_c_re, ssm_c_im, ssm_d, w_glu, b_glu, g_q_norm, w_uq, g_kv_norm, w_uk, w_uv, w_out, g_final):
    bsz, seq, _ = x_prompt.shape
    dbs = x_sample.shape[0]
    n_pages, page = page_table.shape[1], cache_ckv.shape[2]
    past_len = n_pages * page
    layer = 0

    w_in_p = _win_prep(w_in[layer], 256)
    w_uq_p = jnp.pad(w_uq[layer].reshape(Q_LORA, N_HEADS, QK_NOPE + QK_ROPE),
                     ((0, 0), (0, 0), (0, HEAD_Q - QK_NOPE - QK_ROPE))
                     ).reshape(Q_LORA, N_HEADS * HEAD_Q).astype(BF16)
    w_uk_flat = w_uk[layer].reshape(KV_LORA, D_ATT).astype(BF16)
    w_uk_t = w_uk[layer].transpose(1, 2, 0).astype(BF16)
    w_uv_flat = w_uv[layer].reshape(KV_LORA, D_ATT).astype(BF16)
    w_glu_b = w_glu[layer].astype(BF16)
    w_out_b = w_out[layer].astype(BF16)
    g_in = g_norm[layer].reshape(1, D_MODEL)
    g_q = g_q_norm[layer].reshape(1, Q_LORA)
    g_kv = g_kv_norm[layer].reshape(1, KV_LORA)
    g_fin = g_final.reshape(1, D_MODEL)
    d_skip = ssm_d[layer].reshape(1, D_SSM)
    bglu = b_glu[layer].reshape(1, D_SSM)

    abr, abi, bre, bim, crt, cit = _ssm_prep(ssm_a_re[layer], ssm_a_im[layer], ssm_log_dt[layer],
                                             ssm_b_re[layer], ssm_b_im[layer], ssm_c_re[layer], ssm_c_im[layer])
    abr_f, abi_f = abr.reshape(1, D_STATE), abi.reshape(1, D_STATE)
    ssm_w = (bre, bim, crt, cit, d_skip, w_glu_b, bglu)

    pos = jnp.concatenate([jnp.arange(N_META, N_META + seq), jnp.full((dbs,), past_len), jnp.arange(N_META)])
    tabs = _rope_tables(pos)
    tabs_real = tuple(t[:seq] for t in tabs)
    tabs_small = tuple(t[seq:] for t in tabs)

    x_real = x_prompt.reshape(bsz * seq, D_MODEL)
    x_small = jnp.concatenate([x_sample.reshape(dbs, D_MODEL), meta_tokens.astype(x_prompt.dtype)], axis=0)
    n_small = dbs + N_META
    proj_real = _inproj(x_real, g_in, w_in_p, 1024, 768)
    proj_small = _inproj(x_small, g_in, w_in_p, n_small, 768)

    w_uv_t = w_uv_flat.T
    c_real, kr_real, kcat_real, vt_real = _kvprep(proj_real, g_kv, tabs_real, w_uk_flat, w_uv_t, 512, ATT_TILE)
    c_small, kr_small, kcat_small, vt_small = _kvprep(proj_small, g_kv, tabs_small, w_uk_flat, w_uv_t,
                                                      n_small, n_small)
    q_real = _qproj(proj_real, g_q, w_uq_p, tabs_real, 512)
    q_small = _qproj(proj_small, g_q, w_uq_p, tabs_small, n_small)

    proj_meta = jnp.broadcast_to(proj_small[dbs:][None], (bsz, N_META, N_PROJ))
    h_zero = jnp.zeros((N_SLABS, 2 * bsz, LANES), F32)
    _, h_meta = _s5_scan(proj_meta, h_zero, abr_f, abi_f, *ssm_w, tc=N_META)
    ssm_real, h_fin = _s5_scan(proj_real.reshape(bsz, seq, N_PROJ), h_meta, abr_f, abi_f, *ssm_w, tc=64)
    h_fin = _slab_minor(h_fin)
    ssm_smp, hr_s, hi_s = _s5_step(proj_small, state_ssm_re[layer].reshape(dbs, D_STATE),
                                   state_ssm_im[layer].reshape(dbs, D_STATE), abr_f, abi_f, *ssm_w)

    att_real = _attention(q_real, kcat_real, vt_real, kcat_small[dbs:], vt_small[0, :, dbs:], proj_real,
                          bsz, seq, ATT_TILE)

    qs = q_small[:dbs]
    qa = _qabsorb(qs, w_uk_t).reshape(dbs, N_HEADS, KV_LORA)
    qp = qs.reshape(dbs, N_HEADS, HEAD_Q)[:, :, QK_NOPE:QK_NOPE + QK_ROPE]
    c_new = c_small[:dbs].reshape(dbs, 1, KV_LORA)
    k_new = kr_small[:dbs, :QK_ROPE].reshape(dbs, 1, QK_ROPE)
    o_lat = _decode(page_table, qa, qp, c_new, k_new, cache_ckv, jnp.swapaxes(cache_kpe, 2, 3), 32)
    att_smp = _ouv(o_lat.reshape(dbs, N_HEADS * KV_LORA), w_uv_flat, proj_small)

    y_prompt = _outproj(ssm_real.reshape(bsz * seq, D_SSM), att_real, w_out_b, x_real, g_fin, 512)
    y_sample = _outproj(ssm_smp, att_smp, w_out_b, x_sample.reshape(dbs, D_MODEL), g_fin, dbs)

    c_meta, k_meta = c_small[dbs:], kr_small[dbs:, :QK_ROPE]
    ckv_p = jnp.concatenate([jnp.broadcast_to(c_meta[None], (bsz, N_META, KV_LORA)),
                             c_real.reshape(bsz, seq, KV_LORA)], axis=1)
    kpe_p = jnp.concatenate([jnp.broadcast_to(k_meta[None], (bsz, N_META, QK_ROPE)),
                             kr_real[:, :QK_ROPE].reshape(bsz, seq, QK_ROPE)], axis=1)
    return (y_prompt.reshape(bsz, seq, D_MODEL),
            y_sample.reshape(dbs, 1, D_MODEL),
            ckv_p[None], kpe_p[None],
            h_fin[:bsz].reshape(1, bsz, N_GROUPS, N_STATE),
            h_fin[bsz:].reshape(1, bsz, N_GROUPS, N_STATE),
            c_small[:dbs].reshape(1, dbs, 1, KV_LORA),
            kr_small[:dbs, :QK_ROPE].reshape(1, dbs, 1, QK_ROPE),
            hr_s.reshape(1, dbs, N_GROUPS, N_STATE),
            hi_s.reshape(1, dbs, N_GROUPS, N_STATE))
```

```python
import functools
import math

import numpy as np
import jax
import jax.numpy as jnp
from jax import lax
from jax.experimental import pallas as pl
from jax.experimental.pallas import tpu as pltpu

F32 = jnp.float32
BF16 = jnp.bfloat16

D_MODEL = 2048
N_META = 16
D_SSM = 1024
SSM_GROUP = 16
N_GROUPS = 64
N_STATE = 64
D_STATE = N_GROUPS * N_STATE
D_ATT = 1024
V_HEAD = 128
N_HEADS = 8
QK_NOPE = 128
QK_ROPE = 64
KV_LORA = 512
Q_LORA = 768
ROPE_THETA = 10000.0
SOFTMAX_SCALE = (QK_NOPE + QK_ROPE) ** -0.5
Q_SCALE = SOFTMAX_SCALE * math.log2(math.e)
EPS = 1e-6

LANES = 128
SUBLANES = 8
MIB = 1024 * 1024

C_CQ = 0
C_KPE = 768
C_U = 1024
C_ZS = 2048
C_ZA = 3072
C_CKV = 4096
N_PROJ = 4608
HEAD_Q = 256
N_SLABS = D_STATE // LANES
SLAB_GROUP = 8
ATT_TILE = 512
DECODE_SLOTS = 3
SSM_KT = 256
SSM_NT = SSM_KT * N_STATE // SSM_GROUP


def _params(sem, vmem_mib=None):
    return pltpu.CompilerParams(
        dimension_semantics=sem,
        vmem_limit_bytes=None if vmem_mib is None else vmem_mib * MIB)


def _rms(x, g):
    return x * lax.rsqrt(jnp.mean(x * x, axis=-1, keepdims=True) + EPS) * g


def _rope128(x, cos_t, sin_lo, sin_hi):
    return (x * cos_t + pltpu.roll(x, LANES - QK_ROPE // 2, axis=1) * sin_lo
            + pltpu.roll(x, QK_ROPE // 2, axis=1) * sin_hi)


def _rope_table_kernel(pos_ref, inv_ref, cos_ref, slo_ref, shi_ref):
    ang = pos_ref[...] * inv_ref[...]
    c, s = jnp.cos(ang), jnp.sin(ang)
    lane = lax.broadcasted_iota(jnp.int32, ang.shape, 1)
    half = QK_ROPE // 2
    cos_ref[...] = jnp.where(lane < QK_ROPE, c, 0.0)
    slo_ref[...] = jnp.where(lane < half, -s, 0.0)
    shi_ref[...] = jnp.where((lane >= half) & (lane < QK_ROPE), s, 0.0)


def _rope_tables(pos):
    n = pos.shape[0]
    half = QK_ROPE // 2
    inv = ROPE_THETA ** (-(np.arange(LANES) % half).astype(np.float64) / half)
    inv = jnp.asarray(inv.astype(np.float32)).reshape(1, LANES)
    shp = jax.ShapeDtypeStruct((n, LANES), F32)
    return pl.pallas_call(_rope_table_kernel, out_shape=(shp, shp, shp), name="rope_tables")(
        pos.reshape(n, 1).astype(F32), inv)


def _win_prep_kernel(wt_ref, o_ref):
    o_cq, o_ckv, o_kpe = 2 * D_SSM, 2 * D_SSM + Q_LORA, 2 * D_SSM + Q_LORA + KV_LORA
    o_za = o_kpe + QK_ROPE
    cols = o_ref.shape[1]
    o_ref[C_CQ:C_CQ + Q_LORA, :] = wt_ref[o_cq:o_ckv, :].astype(BF16)
    o_ref[C_KPE:C_KPE + QK_ROPE, :] = wt_ref[o_kpe:o_za, :].astype(BF16)
    o_ref[C_KPE + QK_ROPE:C_U, :] = jnp.zeros((C_U - C_KPE - QK_ROPE, cols), BF16)
    o_ref[C_U:C_ZA, :] = wt_ref[:o_cq, :].astype(BF16)
    o_ref[C_ZA:C_CKV, :] = wt_ref[o_za:, :].astype(BF16)
    o_ref[C_CKV:, :] = wt_ref[o_ckv:o_kpe, :].astype(BF16)


def _win_prep(w_t, tc):
    n, k = w_t.shape
    return pl.pallas_call(
        _win_prep_kernel,
        grid=(k // tc,),
        in_specs=[pl.BlockSpec((n, tc), lambda i: (0, i))],
        out_specs=pl.BlockSpec((N_PROJ, tc), lambda i: (0, i)),
        out_shape=jax.ShapeDtypeStruct((N_PROJ, k), BF16),
        compiler_params=_params(("parallel",)),
        name="win_prep")(w_t)


def _inproj_kernel(x_ref, g_ref, w_ref, o_ref, xn_ref):
    @pl.when(pl.program_id(1) == 0)
    def _():
        xn_ref[...] = _rms(x_ref[...], g_ref[...]).astype(BF16)

    o_ref[...] = lax.dot_general(xn_ref[...], w_ref[...], (((1,), (1,)), ((), ())),
                                 preferred_element_type=F32)


def _inproj(x, g, w_t, tm, tn):
    m = x.shape[0]
    return pl.pallas_call(
        _inproj_kernel,
        grid=(m // tm, N_PROJ // tn),
        in_specs=[pl.BlockSpec((tm, D_MODEL), lambda i, j: (i, 0)),
                  pl.BlockSpec((1, D_MODEL), lambda i, j: (0, 0)),
                  pl.BlockSpec((tn, D_MODEL), lambda i, j: (j, 0))],
        out_specs=pl.BlockSpec((tm, tn), lambda i, j: (i, j)),
        out_shape=jax.ShapeDtypeStruct((m, N_PROJ), F32),
        scratch_shapes=[pltpu.VMEM((tm, D_MODEL), BF16)],
        compiler_params=_params(("parallel", "arbitrary"), 48),
        name="inproj")(x, g, w_t)


def _kvprep_kernel(ckv_ref, kpe_ref, g_ref, cos_ref, slo_ref, shi_ref, wuk_ref, wuvt_ref,
                   c_ref, kr_ref, kcat_ref, vt_ref, *, tk):
    c = _rms(ckv_ref[...], g_ref[...])
    c_ref[...] = c
    cb = c.astype(BF16)
    kr = _rope128(kpe_ref[...], cos_ref[...], slo_ref[...], shi_ref[...])
    kr_ref[...] = kr
    krb = kr.astype(BF16)
    knope = jnp.dot(cb, wuk_ref[...], preferred_element_type=F32).astype(BF16)
    for h in range(N_HEADS):
        kcat_ref[:, HEAD_Q * h:HEAD_Q * h + QK_NOPE] = knope[:, QK_NOPE * h:QK_NOPE * (h + 1)]
        kcat_ref[:, HEAD_Q * h + QK_NOPE:HEAD_Q * (h + 1)] = krb
    vt = lax.dot_general(wuvt_ref[...], cb, (((1,), (1,)), ((), ())),
                         preferred_element_type=F32).astype(BF16)
    for kb in range(vt_ref.shape[0]):
        vt_ref[kb] = vt[:, tk * kb:tk * (kb + 1)]


def _kvprep(proj, g, tabs, wuk, wuvt, tm, tk):
    m = proj.shape[0]
    nt = tabs[0].shape[0] // tm
    tab = pl.BlockSpec((tm, LANES), lambda i: (i % nt, 0))
    row = lambda w: pl.BlockSpec((tm, w), lambda i: (i, 0))
    return pl.pallas_call(
        functools.partial(_kvprep_kernel, tk=tk),
        grid=(m // tm,),
        in_specs=[pl.BlockSpec((tm, KV_LORA), lambda i: (i, C_CKV // KV_LORA)),
                  pl.BlockSpec((tm, LANES), lambda i: (i, C_KPE // LANES)),
                  pl.BlockSpec((1, KV_LORA), lambda i: (0, 0)),
                  tab, tab, tab,
                  pl.BlockSpec((KV_LORA, D_ATT), lambda i: (0, 0)),
                  pl.BlockSpec((D_ATT, KV_LORA), lambda i: (0, 0))],
        out_specs=(row(KV_LORA), row(LANES), row(N_HEADS * HEAD_Q),
                   pl.BlockSpec((tm // tk, D_ATT, tk), lambda i: (i, 0, 0))),
        out_shape=(jax.ShapeDtypeStruct((m, KV_LORA), F32),
                   jax.ShapeDtypeStruct((m, LANES), F32),
                   jax.ShapeDtypeStruct((m, N_HEADS * HEAD_Q), BF16),
                   jax.ShapeDtypeStruct((m // tk, D_ATT, tk), BF16)),
        compiler_params=_params(("parallel",)),
        name="kvprep")(proj, proj, g, *tabs, wuk, wuvt)


def _qproj_kernel(cq_ref, g_ref, w_ref, cos_ref, slo_ref, shi_ref, q_ref):
    cqn = _rms(cq_ref[...], g_ref[...]).astype(BF16)
    q = jnp.dot(cqn, w_ref[...], preferred_element_type=F32)
    cos_t, slo, shi = cos_ref[...], slo_ref[...], shi_ref[...]
    for h in range(N_HEADS):
        lo = HEAD_Q * h
        q_ref[:, lo:lo + QK_NOPE] = (q[:, lo:lo + QK_NOPE] * Q_SCALE).astype(BF16)
        pe = _rope128(q[:, lo + QK_NOPE:lo + HEAD_Q], cos_t, slo, shi)
        q_ref[:, lo + QK_NOPE:lo + HEAD_Q] = (pe * Q_SCALE).astype(BF16)


def _qproj(proj, g, w, tabs, tm):
    m = proj.shape[0]
    nt = tabs[0].shape[0] // tm
    tab = pl.BlockSpec((tm, LANES), lambda i: (i % nt, 0))
    return pl.pallas_call(
        _qproj_kernel,
        grid=(m // tm,),
        in_specs=[pl.BlockSpec((tm, Q_LORA), lambda i: (i, C_CQ // Q_LORA)),
                  pl.BlockSpec((1, Q_LORA), lambda i: (0, 0)),
                  pl.BlockSpec((Q_LORA, N_HEADS * HEAD_Q), lambda i: (0, 0)),
                  tab, tab, tab],
        out_specs=pl.BlockSpec((tm, N_HEADS * HEAD_Q), lambda i: (i, 0)),
        out_shape=jax.ShapeDtypeStruct((m, N_HEADS * HEAD_Q), BF16),
        compiler_params=_params(("parallel",)),
        name="qproj")(proj, g, w, *tabs)


def _ssm_prep_kernel(are_ref, aim_ref, ldt_ref, bre_ref, bim_ref, cre_ref, cim_ref,
                     abr_ref, abi_ref, wbr_ref, wbi_ref, wcr_ref, wci_ref):
    dt = jnp.exp(ldt_ref[...])
    ar, ai = are_ref[...], aim_ref[...]
    mag = jnp.exp(dt * ar)
    abr, abi = mag * jnp.cos(dt * ai), mag * jnp.sin(dt * ai)
    den = ar * ar + ai * ai
    nr, ni = abr - 1.0, abi
    f_re = (nr * ar + ni * ai) / den
    f_im = (ni * ar - nr * ai) / den
    br, bi = bre_ref[...], bim_ref[...]
    abr_ref[...] = abr
    abi_ref[...] = abi
    bbr = f_re * br - f_im * bi
    bbi = f_re * bi + f_im * br

    gpt = SSM_KT // SSM_GROUP
    n_kt = D_SSM // SSM_KT

    def spread(x, width, period):
        sel = (lax.broadcasted_iota(jnp.int32, (period, width), 1) % period
               == lax.broadcasted_iota(jnp.int32, (period, width), 0))
        return jnp.dot(x.astype(BF16), jnp.where(sel, 1.0, 0.0).astype(BF16), preferred_element_type=F32)

    r_in = lax.broadcasted_iota(jnp.int32, (SSM_KT, SSM_NT), 0) // SSM_GROUP
    c_in = lax.broadcasted_iota(jnp.int32, (SSM_KT, SSM_NT), 1) // N_STATE
    r_out = lax.broadcasted_iota(jnp.int32, (SSM_NT, SSM_KT), 0) // N_STATE
    c_out = lax.broadcasted_iota(jnp.int32, (SSM_NT, SSM_KT), 1) // SSM_GROUP
    for kt in range(n_kt):
        gs = slice(gpt * kt, gpt * (kt + 1))
        for src, dst in ((bbr, wbr_ref), (bbi, wbi_ref)):
            x = src[gs].reshape(SSM_KT, N_STATE)
            dst[kt] = jnp.where(r_in == c_in, spread(x, SSM_NT, N_STATE), 0.0).astype(BF16)
        for src, dst in ((cre_ref, wcr_ref), (cim_ref, wci_ref)):
            x = src[gs].reshape(SSM_NT, SSM_GROUP)
            dst[kt] = jnp.where(r_out == c_out, spread(x, SSM_KT, SSM_GROUP), 0.0).astype(BF16)


def _ssm_prep(a_re, a_im, log_dt, b_re, b_im, c_re, c_im):
    gn = jax.ShapeDtypeStruct((N_GROUPS, 1, N_STATE), F32)
    w_in_shape = jax.ShapeDtypeStruct((D_SSM // SSM_KT, SSM_KT, SSM_NT), BF16)
    w_out_shape = jax.ShapeDtypeStruct((D_SSM // SSM_KT, SSM_NT, SSM_KT), BF16)
    return pl.pallas_call(
        _ssm_prep_kernel, out_shape=(gn, gn, w_in_shape, w_in_shape, w_out_shape, w_out_shape),
        name="ssm_prep")(
            a_re.reshape(N_GROUPS, 1, N_STATE), a_im.reshape(N_GROUPS, 1, N_STATE),
            log_dt.reshape(N_GROUPS, 1, 1), b_re.transpose(0, 2, 1), b_im.transpose(0, 2, 1),
            c_re.transpose(0, 2, 1), c_im.transpose(0, 2, 1))


def _ssm_epilogue(y, uf, zs, d, wglu, bglu):
    ys = jax.nn.gelu(y + d * uf)
    gate = jax.nn.sigmoid(jnp.dot(ys.astype(BF16), wglu, preferred_element_type=F32) + bglu)
    return ys * gate * (zs * jax.nn.sigmoid(zs))


def _s5_scan_kernel(u_ref, zs_ref, h0_ref, abr_ref, abi_ref, bre_ref, bim_ref, crt_ref, cit_ref,
                    d_ref, wglu_ref, bglu_ref, out_ref, hfin_ref, buf_ref, hst_ref, y_ref,
                    *, nb, tc, pitch):
    i = pl.program_id(0)

    @pl.when(i == 0)
    def _():
        hst_ref[...] = h0_ref[...]

    uf = u_ref[...].reshape(nb * tc, D_SSM)
    ub = uf.astype(BF16)
    n_kt = D_SSM // SSM_KT
    slabs_per_kt = SSM_NT // LANES
    for kt in range(n_kt):
        lhs = ub[:, SSM_KT * kt:SSM_KT * (kt + 1)]
        for ri, w_ref in ((0, bre_ref), (1, bim_ref)):
            res = jnp.dot(lhs, w_ref[kt], preferred_element_type=F32)
            for s in range(slabs_per_kt):
                for b in range(nb):
                    r0 = (ri * nb + b) * pitch
                    buf_ref[kt * slabs_per_kt + s, r0:r0 + tc, :] = (
                        res[b * tc:(b + 1) * tc, LANES * s:LANES * (s + 1)])

    row = lax.broadcasted_iota(jnp.int32, (2 * nb, LANES), 0)

    def scan_group(g, carry):
        base = g * SLAB_GROUP
        a1, a2, h = [], [], []
        for s in range(SLAB_GROUP):
            a1.append(jnp.broadcast_to(abr_ref[base + s], (2 * nb, LANES)))
            ai = jnp.broadcast_to(abi_ref[base + s], (2 * nb, LANES))
            a2.append(jnp.where(row < nb, -ai, ai))
            h.append(hst_ref[base + s])
        for t in range(tc):
            for s in range(SLAB_GROUP):
                bu = buf_ref[base + s, pl.ds(t, 2 * nb, stride=pitch), :]
                h[s] = a1[s] * h[s] + a2[s] * pltpu.roll(h[s], nb, axis=0) + bu
                buf_ref[base + s, pl.ds(t, 2 * nb, stride=pitch), :] = h[s]
        for s in range(SLAB_GROUP):
            hst_ref[base + s] = h[s]
        return carry

    lax.fori_loop(0, N_SLABS // SLAB_GROUP, scan_group, 0)

    def states(jt, ri):
        rows = []
        for b in range(nb):
            r0 = (ri * nb + b) * pitch
            rows.append(jnp.concatenate(
                [buf_ref[jt * slabs_per_kt + s, r0:r0 + tc, :] for s in range(slabs_per_kt)], axis=1))
        return jnp.concatenate(rows, axis=0).astype(BF16)

    for jt in range(n_kt):
        y_ref[:, SSM_KT * jt:SSM_KT * (jt + 1)] = (
            jnp.dot(states(jt, 0), crt_ref[jt], preferred_element_type=F32)
            - jnp.dot(states(jt, 1), cit_ref[jt], preferred_element_type=F32))

    zs = zs_ref[...].reshape(nb * tc, D_SSM)
    out = _ssm_epilogue(y_ref[...], uf, zs, d_ref[...], wglu_ref[...], bglu_ref[...])
    out_ref[...] = out.reshape(nb, tc, D_SSM).astype(BF16)

    @pl.when(i == pl.num_programs(0) - 1)
    def _():
        hfin_ref[...] = hst_ref[...]


def _s5_scan(proj3, h0, abr, abi, bre, bim, crt, cit, d, wglu, bglu, tc):
    nb, t_len = proj3.shape[0], proj3.shape[1]
    pitch = tc + SUBLANES
    const = lambda shape: pl.BlockSpec(shape, lambda i: (0,) * len(shape))
    kern = functools.partial(_s5_scan_kernel, nb=nb, tc=tc, pitch=pitch)
    return pl.pallas_call(
        kern,
        grid=(t_len // tc,),
        in_specs=[pl.BlockSpec((nb, tc, D_SSM), lambda i: (0, i, C_U // D_SSM)),
                  pl.BlockSpec((nb, tc, D_SSM), lambda i: (0, i, C_ZS // D_SSM)),
                  const((N_SLABS, 2 * nb, LANES)),
                  const((N_SLABS, 1, LANES)), const((N_SLABS, 1, LANES)),
                  const(bre.shape), const(bim.shape), const(crt.shape), const(cit.shape),
                  const((1, D_SSM)), const((D_SSM, D_SSM)), const((1, D_SSM))],
        out_specs=(pl.BlockSpec((nb, tc, D_SSM), lambda i: (0, i, 0)),
                   const((N_SLABS, 2 * nb, LANES))),
        out_shape=(jax.ShapeDtypeStruct((nb, t_len, D_SSM), BF16),
                   jax.ShapeDtypeStruct((N_SLABS, 2 * nb, LANES), F32)),
        scratch_shapes=[pltpu.VMEM((N_SLABS, 2 * nb * pitch, LANES), F32),
                        pltpu.VMEM((N_SLABS, 2 * nb, LANES), F32),
                        pltpu.VMEM((nb * tc, D_SSM), F32)],
        compiler_params=_params(("arbitrary",), 56),
        name="s5_scan")(proj3, proj3, h0, abr.reshape(N_SLABS, 1, LANES), abi.reshape(N_SLABS, 1, LANES),
                        bre, bim, crt, cit, d, wglu, bglu)


def _s5_step_kernel(u_ref, zs_ref, h0r_ref, h0i_ref, abr_ref, abi_ref, bre_ref, bim_ref, crt_ref,
                    cit_ref, d_ref, wglu_ref, bglu_ref, out_ref, hr_ref, hi_ref, y_ref):
    uf = u_ref[...]
    ub = uf.astype(BF16)
    n_kt = D_SSM // SSM_KT
    for kt in range(n_kt):
        lhs = ub[:, SSM_KT * kt:SSM_KT * (kt + 1)]
        sl = slice(SSM_NT * kt, SSM_NT * (kt + 1))
        ar, ai = abr_ref[:, sl], abi_ref[:, sl]
        h0r, h0i = h0r_ref[:, sl], h0i_ref[:, sl]
        hr_ref[:, sl] = ar * h0r - ai * h0i + jnp.dot(lhs, bre_ref[kt], preferred_element_type=F32)
        hi_ref[:, sl] = ar * h0i + ai * h0r + jnp.dot(lhs, bim_ref[kt], preferred_element_type=F32)
    for jt in range(n_kt):
        sl = slice(SSM_NT * jt, SSM_NT * (jt + 1))
        y_ref[:, SSM_KT * jt:SSM_KT * (jt + 1)] = (
            jnp.dot(hr_ref[:, sl].astype(BF16), crt_ref[jt], preferred_element_type=F32)
            - jnp.dot(hi_ref[:, sl].astype(BF16), cit_ref[jt], preferred_element_type=F32))
    out = _ssm_epilogue(y_ref[...], uf, zs_ref[...], d_ref[...], wglu_ref[...], bglu_ref[...])
    out_ref[...] = out.astype(BF16)


def _s5_step(proj, h0r, h0i, abr, abi, bre, bim, crt, cit, d, wglu, bglu):
    m = h0r.shape[0]
    const = lambda shape: pl.BlockSpec(shape, lambda i: (0,) * len(shape))
    return pl.pallas_call(
        _s5_step_kernel,
        grid=(1,),
        in_specs=[pl.BlockSpec((m, D_SSM), lambda i: (0, C_U // D_SSM)),
                  pl.BlockSpec((m, D_SSM), lambda i: (0, C_ZS // D_SSM)),
                  const((m, D_STATE)), const((m, D_STATE)),
                  const((1, D_STATE)), const((1, D_STATE)),
                  const(bre.shape), const(bim.shape), const(crt.shape), const(cit.shape),
                  const((1, D_SSM)), const((D_SSM, D_SSM)), const((1, D_SSM))],
        out_specs=(const((m, D_SSM)), const((m, D_STATE)), const((m, D_STATE))),
        out_shape=(jax.ShapeDtypeStruct((m, D_SSM), BF16),
                   jax.ShapeDtypeStruct((m, D_STATE), F32),
                   jax.ShapeDtypeStruct((m, D_STATE), F32)),
        scratch_shapes=[pltpu.VMEM((m, D_SSM), F32)],
        compiler_params=_params(("arbitrary",), 48),
        name="s5_step")(proj, proj, h0r, h0i, abr, abi, bre, bim, crt, cit, d, wglu, bglu)


def _attn_kernel(q_ref, k_ref, vt_ref, km_ref, vmt_ref, za_ref, o_ref, *scratch, tq):
    m_ref, l_ref, acc_ref = (scratch[:N_HEADS], scratch[N_HEADS:2 * N_HEADS], scratch[2 * N_HEADS:])
    qi = pl.program_id(1)
    nt = (((1,), (1,)), ((), ()))
    heads = [(slice(HEAD_Q * h, HEAD_Q * (h + 1)), slice(V_HEAD * h, V_HEAD * (h + 1)))
             for h in range(N_HEADS)]

    for h, (qs, vs) in enumerate(heads):
        s = lax.dot_general(km_ref[:, qs], q_ref[:, qs], nt, preferred_element_type=F32)
        m = jnp.max(s, axis=0, keepdims=True)
        p = jnp.exp2(s - m)
        m_ref[h][...] = m
        l_ref[h][...] = jnp.sum(p, axis=0, keepdims=True)
        acc_ref[h][...] = jnp.dot(vmt_ref[vs, :], p.astype(BF16), preferred_element_type=F32)

    def kv_block(j, masked):
        off = pl.multiple_of(j * tq, tq)
        if masked:
            key = lax.broadcasted_iota(jnp.int32, (tq, tq), 0)
            qry = lax.broadcasted_iota(jnp.int32, (tq, tq), 1)
            keep = key <= qry
        for h, (qs, vs) in enumerate(heads):
            s = lax.dot_general(k_ref[pl.ds(off, tq), qs], q_ref[:, qs], nt,
                                preferred_element_type=F32)
            if masked:
                s = jnp.where(keep, s, -jnp.inf)
            m_prev = m_ref[h][...]
            m_new = jnp.maximum(m_prev, jnp.max(s, axis=0, keepdims=True))
            alpha = jnp.exp2(m_prev - m_new)
            p = jnp.exp2(s - m_new)
            l_ref[h][...] = alpha * l_ref[h][...] + jnp.sum(p, axis=0, keepdims=True)
            acc_ref[h][...] = alpha * acc_ref[h][...] + jnp.dot(
                vt_ref[j, vs, :], p.astype(BF16), preferred_element_type=F32)
            m_ref[h][...] = m_new

    def full_block(j, carry):
        kv_block(j, False)
        return carry

    lax.fori_loop(0, qi, full_block, 0)
    kv_block(qi, True)
    za = za_ref[...]
    gate = za * jax.nn.sigmoid(za)
    for h, (qs, vs) in enumerate(heads):
        o = (acc_ref[h][...] / l_ref[h][...]).T
        o_ref[:, vs] = (o * gate[:, vs]).astype(BF16)


def _attention(q, kcat, vt, kmeta, vmeta_t, proj, bsz, seq, tq):
    nq = seq // tq
    kern = functools.partial(_attn_kernel, tq=tq)
    return pl.pallas_call(
        kern,
        grid=(bsz, nq),
        in_specs=[pl.BlockSpec((tq, N_HEADS * HEAD_Q), lambda b, i: (b * nq + i, 0)),
                  pl.BlockSpec((None, seq, N_HEADS * HEAD_Q), lambda b, i: (b, 0, 0)),
                  pl.BlockSpec((nq, D_ATT, tq), lambda b, i: (b, 0, 0)),
                  pl.BlockSpec((N_META, N_HEADS * HEAD_Q), lambda b, i: (0, 0)),
                  pl.BlockSpec((D_ATT, N_META), lambda b, i: (0, 0)),
                  pl.BlockSpec((tq, D_ATT), lambda b, i: (b * nq + i, C_ZA // D_ATT))],
        out_specs=pl.BlockSpec((tq, D_ATT), lambda b, i: (b * nq + i, 0)),
        out_shape=jax.ShapeDtypeStruct((bsz * seq, D_ATT), BF16),
        scratch_shapes=([pltpu.VMEM((1, tq), F32)] * (2 * N_HEADS)
                        + [pltpu.VMEM((V_HEAD, tq), F32)] * N_HEADS),
        compiler_params=_params(("parallel", "arbitrary"), 48),
        name="attn_prompt")(q, kcat.reshape(bsz, seq, N_HEADS * HEAD_Q), vt, kmeta, vmeta_t, proj)


def _qabsorb_kernel(q_ref, wukt_ref, qa_ref):
    for h in range(N_HEADS):
        qn = q_ref[:, HEAD_Q * h:HEAD_Q * h + QK_NOPE]
        qa_ref[:, KV_LORA * h:KV_LORA * (h + 1)] = jnp.dot(
            qn, wukt_ref[h], preferred_element_type=F32).astype(BF16)


def _qabsorb(q, wukt):
    m = q.shape[0]
    return pl.pallas_call(
        _qabsorb_kernel,
        out_shape=jax.ShapeDtypeStruct((m, N_HEADS * KV_LORA), BF16),
        name="q_absorb")(q, wukt)


def _decode_kernel(pt_ref, qa_ref, qp_ref, cn_ref, kn_ref, ckv_hbm, kpe_hbm, o_ref,
                   cbuf, kbuf, sem, m_ref, l_ref, acc_ref, *, npg, n_pages):
    b, j = pl.program_id(0), pl.program_id(1)
    nj = pl.num_programs(1)
    t = b * nj + j
    last = pl.num_programs(0) * nj - 1
    ahead = DECODE_SLOTS - 1
    slot = t % DECODE_SLOTS

    def page_copies(step, buf, k):
        pg = pt_ref[(step // nj) * n_pages + (step % nj) * npg + k]
        return (pltpu.make_async_copy(ckv_hbm.at[0, pg], cbuf.at[buf, k], sem.at[buf]),
                pltpu.make_async_copy(kpe_hbm.at[0, pg], kbuf.at[buf, k], sem.at[buf]))

    def start_step(step, buf):
        for k in range(npg):
            for cp in page_copies(step, buf, k):
                cp.start()

    def wait_step(step, buf):
        for k in range(npg):
            for cp in page_copies(step, buf, k):
                cp.wait()

    @pl.when(t == 0)
    def _():
        for d in range(ahead):
            start_step(jnp.minimum(d, last), d)

    t_next = jnp.minimum(t + ahead, last)
    start_step(t_next, (t + ahead) % DECODE_SLOTS)
    wait_step(t, slot)
    c_refs = [cbuf.at[slot, k] for k in range(npg)]
    k_refs = [kbuf.at[slot, k] for k in range(npg)]

    @pl.when(j == 0)
    def _():
        m_ref[...] = jnp.full(m_ref.shape, -jnp.inf, F32)
        l_ref[...] = jnp.zeros(l_ref.shape, F32)
        acc_ref[...] = jnp.zeros(acc_ref.shape, F32)

    qa, qp = qa_ref[...], qp_ref[...]
    nt = (((1,), (1,)), ((), ()))
    page = c_refs[0].shape[0]
    cb = [r[...].astype(BF16) for r in c_refs]
    s = jnp.concatenate(
        [lax.dot_general(qa, cb[k], nt, preferred_element_type=F32)
         + jnp.dot(qp, k_refs[k][...].astype(BF16), preferred_element_type=F32)
         for k in range(npg)], axis=1)
    m_prev = m_ref[...]
    m_new = jnp.maximum(m_prev, jnp.max(s, axis=-1, keepdims=True))
    alpha = jnp.exp2(m_prev - m_new)
    p32 = jnp.exp2(s - m_new)
    p = p32.astype(BF16)
    pv = jnp.dot(p[:, :page], cb[0], preferred_element_type=F32)
    for k in range(1, npg):
        pv = pv + jnp.dot(p[:, page * k:page * (k + 1)], cb[k], preferred_element_type=F32)
    l_ref[...] = alpha * l_ref[...] + jnp.sum(p32, axis=-1, keepdims=True)
    acc_ref[...] = alpha * acc_ref[...] + pv
    m_ref[...] = m_new

    @pl.when(j == pl.num_programs(1) - 1)
    def _():
        cn, kn = cn_ref[...], kn_ref[...]
        s_new = (jnp.sum(qa.astype(F32) * cn, axis=-1, keepdims=True)
                 + jnp.sum(qp.astype(F32) * kn, axis=-1, keepdims=True))
        m_old = m_ref[...]
        m_fin = jnp.maximum(m_old, s_new)
        a = jnp.exp2(m_old - m_fin)
        pn = jnp.exp2(s_new - m_fin)
        l_fin = a * l_ref[...] + pn
        o_ref[...] = (a * acc_ref[...] + pn * cn) / l_fin

    @pl.when(t == last)
    def _():
        for d in range(1, DECODE_SLOTS):
            wait_step(last, (t + d) % DECODE_SLOTS)


def _decode(page_table, qa, qp, c_new, k_new, cache_ckv, cache_kpe_t, npg):
    dbs, n_pages = page_table.shape
    page = cache_ckv.shape[2]
    per_b = lambda shape: pl.BlockSpec((None,) + shape, lambda b, j, pt: (b, 0, 0))
    grid_spec = pltpu.PrefetchScalarGridSpec(
        num_scalar_prefetch=1,
        grid=(dbs, n_pages // npg),
        in_specs=[per_b((N_HEADS, KV_LORA)), per_b((N_HEADS, QK_ROPE)),
                  per_b((1, KV_LORA)), per_b((1, QK_ROPE)),
                  pl.BlockSpec(memory_space=pl.ANY), pl.BlockSpec(memory_space=pl.ANY)],
        out_specs=per_b((N_HEADS, KV_LORA)),
        scratch_shapes=[pltpu.VMEM((DECODE_SLOTS, npg, page, KV_LORA), F32),
                        pltpu.VMEM((DECODE_SLOTS, npg, QK_ROPE, page), F32),
                        pltpu.SemaphoreType.DMA((DECODE_SLOTS,)),
                        pltpu.VMEM((N_HEADS, 1), F32), pltpu.VMEM((N_HEADS, 1), F32),
                        pltpu.VMEM((N_HEADS, KV_LORA), F32)])
    return pl.pallas_call(
        functools.partial(_decode_kernel, npg=npg, n_pages=n_pages),
        grid_spec=grid_spec,
        out_shape=jax.ShapeDtypeStruct((dbs, N_HEADS, KV_LORA), F32),
        compiler_params=_params(("arbitrary", "arbitrary"), 48),
        name="attn_decode")(page_table.reshape(-1), qa, qp, c_new, k_new, cache_ckv, cache_kpe_t)


def _ouv_kernel(ol_ref, wuv_ref, za_ref, o_ref):
    za = za_ref[...]
    gate = za * jax.nn.sigmoid(za)
    for h in range(N_HEADS):
        o = jnp.dot(ol_ref[:, KV_LORA * h:KV_LORA * (h + 1)].astype(BF16),
                    wuv_ref[:, V_HEAD * h:V_HEAD * (h + 1)], preferred_element_type=F32)
        o_ref[:, V_HEAD * h:V_HEAD * (h + 1)] = (o * gate[:, V_HEAD * h:V_HEAD * (h + 1)]).astype(BF16)


def _ouv(o_lat, wuv, proj):
    m = o_lat.shape[0]
    const = lambda shape: pl.BlockSpec(shape, lambda i: (0,) * len(shape))
    return pl.pallas_call(
        _ouv_kernel,
        grid=(1,),
        in_specs=[const((m, N_HEADS * KV_LORA)), const((KV_LORA, D_ATT)),
                  pl.BlockSpec((m, D_ATT), lambda i: (0, C_ZA // D_ATT))],
        out_specs=const((m, D_ATT)),
        out_shape=jax.ShapeDtypeStruct((m, D_ATT), BF16),
        name="o_uv")(o_lat, wuv, proj)


def _outproj_kernel(ssm_ref, att_ref, wtop_ref, wbot_ref, x_ref, g_ref, y_ref):
    h = (x_ref[...]
         + jnp.dot(ssm_ref[...], wtop_ref[...], preferred_element_type=F32)
         + jnp.dot(att_ref[...], wbot_ref[...], preferred_element_type=F32))
    y_ref[...] = _rms(h, g_ref[...])


def _outproj(ssm, att, w, x, g, tm):
    m = x.shape[0]
    return pl.pallas_call(
        _outproj_kernel,
        grid=(m // tm,),
        in_specs=[pl.BlockSpec((tm, D_SSM), lambda i: (i, 0)),
                  pl.BlockSpec((tm, D_ATT), lambda i: (i, 0)),
                  pl.BlockSpec((D_SSM, D_MODEL), lambda i: (0, 0)),
                  pl.BlockSpec((D_ATT, D_MODEL), lambda i: (1, 0)),
                  pl.BlockSpec((tm, D_MODEL), lambda i: (i, 0)),
                  pl.BlockSpec((1, D_MODEL), lambda i: (0, 0))],
        out_specs=pl.BlockSpec((tm, D_MODEL), lambda i: (i, 0)),
        out_shape=jax.ShapeDtypeStruct((m, D_MODEL), F32),
        compiler_params=_params(("parallel",), 48),
        name="outproj")(ssm, att, w, w, x, g)


def _slab_minor(h):
    return h.transpose(1, 0, 2).reshape(h.shape[1], D_STATE)


def kernel(x_prompt, x_sample, cache_ckv, cache_kpe, state_ssm_re, state_ssm_im, page_table, meta_tokens, g_norm, w_in, ssm_a_re, ssm_a_im, ssm_log_dt, ssm_b_re, ssm_b_im, ssm_c_re, ssm_c_im, ssm_d, w_glu, b_glu, g_q_norm, w_uq, g_kv_norm, w_uk, w_uv, w_out, g_final):
    bsz, seq, _ = x_prompt.shape
    dbs = x_sample.shape[0]
    n_pages, page = page_table.shape[1], cache_ckv.shape[2]
    past_len = n_pages * page
    layer = 0

    w_in_p = _win_prep(jnp.swapaxes(w_in[layer], 0, 1), 256)
    w_uq_p = jnp.pad(w_uq[layer].reshape(Q_LORA, N_HEADS, QK_NOPE + QK_ROPE),
                     ((0, 0), (0, 0), (0, HEAD_Q - QK_NOPE - QK_ROPE))
                     ).reshape(Q_LORA, N_HEADS * HEAD_Q).astype(BF16)
    w_uk_flat = w_uk[layer].reshape(KV_LORA, D_ATT).astype(BF16)
    w_uk_t = w_uk[layer].transpose(1, 2, 0).astype(BF16)
    w_uv_flat = w_uv[layer].reshape(KV_LORA, D_ATT).astype(BF16)
    w_glu_b = w_glu[layer].astype(BF16)
    w_out_b = w_out[layer].astype(BF16)
    g_in = g_norm[layer].reshape(1, D_MODEL)
    g_q = g_q_norm[layer].reshape(1, Q_LORA)
    g_kv = g_kv_norm[layer].reshape(1, KV_LORA)
    g_fin = g_final.reshape(1, D_MODEL)
    d_skip = ssm_d[layer].reshape(1, D_SSM)
    bglu = b_glu[layer].reshape(1, D_SSM)

    abr, abi, bre, bim, crt, cit = _ssm_prep(ssm_a_re[layer], ssm_a_im[layer], ssm_log_dt[layer],
                                             ssm_b_re[layer], ssm_b_im[layer], ssm_c_re[layer], ssm_c_im[layer])
    abr_f, abi_f = abr.reshape(1, D_STATE), abi.reshape(1, D_STATE)
    ssm_w = (bre, bim, crt, cit, d_skip, w_glu_b, bglu)

    pos = jnp.concatenate([jnp.arange(N_META, N_META + seq), jnp.full((dbs,), past_len), jnp.arange(N_META)])
    tabs = _rope_tables(pos)
    tabs_real = tuple(t[:seq] for t in tabs)
    tabs_small = tuple(t[seq:] for t in tabs)

    x_real = x_prompt.reshape(bsz * seq, D_MODEL)
    x_small = jnp.concatenate([x_sample.reshape(dbs, D_MODEL), meta_tokens.astype(x_prompt.dtype)], axis=0)
    n_small = dbs + N_META
    proj_real = _inproj(x_real, g_in, w_in_p, 1024, 768)
    proj_small = _inproj(x_small, g_in, w_in_p, n_small, 768)

    w_uv_t = w_uv_flat.T
    c_real, kr_real, kcat_real, vt_real = _kvprep(proj_real, g_kv, tabs_real, w_uk_flat, w_uv_t, 512, ATT_TILE)
    c_small, kr_small, kcat_small, vt_small = _kvprep(proj_small, g_kv, tabs_small, w_uk_flat, w_uv_t,
                                                      n_small, n_small)
    q_real = _qproj(proj_real, g_q, w_uq_p, tabs_real, 512)
    q_small = _qproj(proj_small, g_q, w_uq_p, tabs_small, n_small)

    proj_meta = jnp.broadcast_to(proj_small[dbs:][None], (bsz, N_META, N_PROJ))
    h_zero = jnp.zeros((N_SLABS, 2 * bsz, LANES), F32)
    _, h_meta = _s5_scan(proj_meta, h_zero, abr_f, abi_f, *ssm_w, tc=N_META)
    ssm_real, h_fin = _s5_scan(proj_real.reshape(bsz, seq, N_PROJ), h_meta, abr_f, abi_f, *ssm_w, tc=64)
    h_fin = _slab_minor(h_fin)
    ssm_smp, hr_s, hi_s = _s5_step(proj_small, state_ssm_re[layer].reshape(dbs, D_STATE),
                                   state_ssm_im[layer].reshape(dbs, D_STATE), abr_f, abi_f, *ssm_w)

    att_real = _attention(q_real, kcat_real, vt_real, kcat_small[dbs:], vt_small[0, :, dbs:], proj_real,
                          bsz, seq, ATT_TILE)

    qs = q_small[:dbs]
    qa = _qabsorb(qs, w_uk_t).reshape(dbs, N_HEADS, KV_LORA)
    qp = qs.reshape(dbs, N_HEADS, HEAD_Q)[:, :, QK_NOPE:QK_NOPE + QK_ROPE]
    c_new = c_small[:dbs].reshape(dbs, 1, KV_LORA)
    k_new = kr_small[:dbs, :QK_ROPE].reshape(dbs, 1, QK_ROPE)
    o_lat = _decode(page_table, qa, qp, c_new, k_new, cache_ckv, jnp.swapaxes(cache_kpe, 2, 3), 32)
    att_smp = _ouv(o_lat.reshape(dbs, N_HEADS * KV_LORA), w_uv_flat, proj_small)

    y_prompt = _outproj(ssm_real.reshape(bsz * seq, D_SSM), att_real, w_out_b, x_real, g_fin, 512)
    y_sample = _outproj(ssm_smp, att_smp, w_out_b, x_sample.reshape(dbs, D_MODEL), g_fin, dbs)

    c_meta, k_meta = c_small[dbs:], kr_small[dbs:, :QK_ROPE]
    ckv_p = jnp.concatenate([jnp.broadcast_to(c_meta[None], (bsz, N_META, KV_LORA)),
                             c_real.reshape(bsz, seq, KV_LORA)], axis=1)
    kpe_p = jnp.concatenate([jnp.broadcast_to(k_meta[None], (bsz, N_META, QK_ROPE)),
                             kr_real[:, :QK_ROPE].reshape(bsz, seq, QK_ROPE)], axis=1)
    return (y_prompt.reshape(bsz, seq, D_MODEL),
            y_sample.reshape(dbs, 1, D_MODEL),
            ckv_p[None], kpe_p[None],
            h_fin[:bsz].reshape(1, bsz, N_GROUPS, N_STATE),
            h_fin[bsz:].reshape(1, bsz, N_GROUPS, N_STATE),
            c_small[:dbs].reshape(1, dbs, 1, KV_LORA),
            kr_small[:dbs, :QK_ROPE].reshape(1, dbs, 1, QK_ROPE),
            hr_s.reshape(1, dbs, N_GROUPS, N_STATE),
            hi_s.reshape(1, dbs, N_GROUPS, N_STATE))
```

```python
import functools
import math

import numpy as np
import jax
import jax.numpy as jnp
from jax import lax
from jax.experimental import pallas as pl
from jax.experimental.pallas import tpu as pltpu

F32 = jnp.float32
BF16 = jnp.bfloat16

D_MODEL = 2048
N_META = 16
D_SSM = 1024
SSM_GROUP = 16
N_GROUPS = 64
N_STATE = 64
D_STATE = N_GROUPS * N_STATE
D_ATT = 1024
V_HEAD = 128
N_HEADS = 8
QK_NOPE = 128
QK_ROPE = 64
KV_LORA = 512
Q_LORA = 768
ROPE_THETA = 10000.0
SOFTMAX_SCALE = (QK_NOPE + QK_ROPE) ** -0.5
Q_SCALE = SOFTMAX_SCALE * math.log2(math.e)
EPS = 1e-6

LANES = 128
SUBLANES = 8
MIB = 1024 * 1024

C_CQ = 0
C_KPE = 768
C_U = 1024
C_ZS = 2048
C_ZA = 3072
C_CKV = 4096
N_PROJ = 4608
HEAD_Q = 256
N_SLABS = D_STATE // LANES
SLAB_GROUP = 8
ATT_TILE = 512
DECODE_SLOTS = 3
SSM_KT = 256
SSM_NT = SSM_KT * N_STATE // SSM_GROUP


def _params(sem, vmem_mib=None):
    return pltpu.CompilerParams(
        dimension_semantics=sem,
        vmem_limit_bytes=None if vmem_mib is None else vmem_mib * MIB)


def _rms(x, g):
    return x * lax.rsqrt(jnp.mean(x * x, axis=-1, keepdims=True) + EPS) * g


def _rope128(x, cos_t, sin_lo, sin_hi):
    return (x * cos_t + pltpu.roll(x, LANES - QK_ROPE // 2, axis=1) * sin_lo
            + pltpu.roll(x, QK_ROPE // 2, axis=1) * sin_hi)


def _rope_table_kernel(pos_ref, inv_ref, cos_ref, slo_ref, shi_ref):
    ang = pos_ref[...] * inv_ref[...]
    c, s = jnp.cos(ang), jnp.sin(ang)
    lane = lax.broadcasted_iota(jnp.int32, ang.shape, 1)
    half = QK_ROPE // 2
    cos_ref[...] = jnp.where(lane < QK_ROPE, c, 0.0)
    slo_ref[...] = jnp.where(lane < half, -s, 0.0)
    shi_ref[...] = jnp.where((lane >= half) & (lane < QK_ROPE), s, 0.0)


def _rope_tables(pos):
    n = pos.shape[0]
    half = QK_ROPE // 2
    inv = ROPE_THETA ** (-(np.arange(LANES) % half).astype(np.float64) / half)
    inv = jnp.asarray(inv.astype(np.float32)).reshape(1, LANES)
    shp = jax.ShapeDtypeStruct((n, LANES), F32)
    return pl.pallas_call(_rope_table_kernel, out_shape=(shp, shp, shp), name="rope_tables")(
        pos.reshape(n, 1).astype(F32), inv)


def _win_prep_kernel(wt_ref, o_ref):
    o_cq, o_ckv, o_kpe = 2 * D_SSM, 2 * D_SSM + Q_LORA, 2 * D_SSM + Q_LORA + KV_LORA
    o_za = o_kpe + QK_ROPE
    cols = o_ref.shape[1]
    o_ref[C_CQ:C_CQ + Q_LORA, :] = wt_ref[o_cq:o_ckv, :].astype(BF16)
    o_ref[C_KPE:C_KPE + QK_ROPE, :] = wt_ref[o_kpe:o_za, :].astype(BF16)
    o_ref[C_KPE + QK_ROPE:C_U, :] = jnp.zeros((C_U - C_KPE - QK_ROPE, cols), BF16)
    o_ref[C_U:C_ZA, :] = wt_ref[:o_cq, :].astype(BF16)
    o_ref[C_ZA:C_CKV, :] = wt_ref[o_za:, :].astype(BF16)
    o_ref[C_CKV:, :] = wt_ref[o_ckv:o_kpe, :].astype(BF16)


def _win_prep(w_t, tc):
    n, k = w_t.shape
    return pl.pallas_call(
        _win_prep_kernel,
        grid=(k // tc,),
        in_specs=[pl.BlockSpec((n, tc), lambda i: (0, i))],
        out_specs=pl.BlockSpec((N_PROJ, tc), lambda i: (0, i)),
        out_shape=jax.ShapeDtypeStruct((N_PROJ, k), BF16),
        compiler_params=_params(("parallel",)),
        name="win_prep")(w_t)


def _inproj_kernel(x_ref, g_ref, w_ref, o_ref, xn_ref):
    @pl.when(pl.program_id(1) == 0)
    def _():
        xn_ref[...] = _rms(x_ref[...], g_ref[...]).astype(BF16)

    o_ref[...] = lax.dot_general(xn_ref[...], w_ref[...], (((1,), (1,)), ((), ())),
                                 preferred_element_type=F32)


def _inproj(x, g, w_t, tm, tn):
    m = x.shape[0]
    return pl.pallas_call(
        _inproj_kernel,
        grid=(m // tm, N_PROJ // tn),
        in_specs=[pl.BlockSpec((tm, D_MODEL), lambda i, j: (i, 0)),
                  pl.BlockSpec((1, D_MODEL), lambda i, j: (0, 0)),
                  pl.BlockSpec((tn, D_MODEL), lambda i, j: (j, 0))],
        out_specs=pl.BlockSpec((tm, tn), lambda i, j: (i, j)),
        out_shape=jax.ShapeDtypeStruct((m, N_PROJ), F32),
        scratch_shapes=[pltpu.VMEM((tm, D_MODEL), BF16)],
        compiler_params=_params(("parallel", "arbitrary"), 48),
        name="inproj")(x, g, w_t)


def _kvprep_kernel(ckv_ref, kpe_ref, g_ref, cos_ref, slo_ref, shi_ref, wuk_ref, wuvt_ref,
                   c_ref, kr_ref, kcat_ref, vt_ref, *, tk):
    c = _rms(ckv_ref[...], g_ref[...])
    c_ref[...] = c
    cb = c.astype(BF16)
    kr = _rope128(kpe_ref[...], cos_ref[...], slo_ref[...], shi_ref[...])
    kr_ref[...] = kr
    krb = kr.astype(BF16)
    knope = jnp.dot(cb, wuk_ref[...], preferred_element_type=F32).astype(BF16)
    for h in range(N_HEADS):
        kcat_ref[:, HEAD_Q * h:HEAD_Q * h + QK_NOPE] = knope[:, QK_NOPE * h:QK_NOPE * (h + 1)]
        kcat_ref[:, HEAD_Q * h + QK_NOPE:HEAD_Q * (h + 1)] = krb
    vt = lax.dot_general(wuvt_ref[...], cb, (((1,), (1,)), ((), ())),
                         preferred_element_type=F32).astype(BF16)
    for kb in range(vt_ref.shape[0]):
        vt_ref[kb] = vt[:, tk * kb:tk * (kb + 1)]


def _kvprep(proj, g, tabs, wuk, wuvt, tm, tk):
    m = proj.shape[0]
    nt = tabs[0].shape[0] // tm
    tab = pl.BlockSpec((tm, LANES), lambda i: (i % nt, 0))
    row = lambda w: pl.BlockSpec((tm, w), lambda i: (i, 0))
    return pl.pallas_call(
        functools.partial(_kvprep_kernel, tk=tk),
        grid=(m // tm,),
        in_specs=[pl.BlockSpec((tm, KV_LORA), lambda i: (i, C_CKV // KV_LORA)),
                  pl.BlockSpec((tm, LANES), lambda i: (i, C_KPE // LANES)),
                  pl.BlockSpec((1, KV_LORA), lambda i: (0, 0)),
                  tab, tab, tab,
                  pl.BlockSpec((KV_LORA, D_ATT), lambda i: (0, 0)),
                  pl.BlockSpec((D_ATT, KV_LORA), lambda i: (0, 0))],
        out_specs=(row(KV_LORA), row(LANES), row(N_HEADS * HEAD_Q),
                   pl.BlockSpec((tm // tk, D_ATT, tk), lambda i: (i, 0, 0))),
        out_shape=(jax.ShapeDtypeStruct((m, KV_LORA), F32),
                   jax.ShapeDtypeStruct((m, LANES), F32),
                   jax.ShapeDtypeStruct((m, N_HEADS * HEAD_Q), BF16),
                   jax.ShapeDtypeStruct((m // tk, D_ATT, tk), BF16)),
        compiler_params=_params(("parallel",)),
        name="kvprep")(proj, proj, g, *tabs, wuk, wuvt)


def _qproj_kernel(cq_ref, g_ref, w_ref, cos_ref, slo_ref, shi_ref, q_ref):
    cqn = _rms(cq_ref[...], g_ref[...]).astype(BF16)
    q = jnp.dot(cqn, w_ref[...], preferred_element_type=F32)
    cos_t, slo, shi = cos_ref[...], slo_ref[...], shi_ref[...]
    for h in range(N_HEADS):
        lo = HEAD_Q * h
        q_ref[:, lo:lo + QK_NOPE] = (q[:, lo:lo + QK_NOPE] * Q_SCALE).astype(BF16)
        pe = _rope128(q[:, lo + QK_NOPE:lo + HEAD_Q], cos_t, slo, shi)
        q_ref[:, lo + QK_NOPE:lo + HEAD_Q] = (pe * Q_SCALE).astype(BF16)


def _qproj(proj, g, w, tabs, tm):
    m = proj.shape[0]
    nt = tabs[0].shape[0] // tm
    tab = pl.BlockSpec((tm, LANES), lambda i: (i % nt, 0))
    return pl.pallas_call(
        _qproj_kernel,
        grid=(m // tm,),
        in_specs=[pl.BlockSpec((tm, Q_LORA), lambda i: (i, C_CQ // Q_LORA)),
                  pl.BlockSpec((1, Q_LORA), lambda i: (0, 0)),
                  pl.BlockSpec((Q_LORA, N_HEADS * HEAD_Q), lambda i: (0, 0)),
                  tab, tab, tab],
        out_specs=pl.BlockSpec((tm, N_HEADS * HEAD_Q), lambda i: (i, 0)),
        out_shape=jax.ShapeDtypeStruct((m, N_HEADS * HEAD_Q), BF16),
        compiler_params=_params(("parallel",)),
        name="qproj")(proj, g, w, *tabs)


def _ssm_prep_kernel(are_ref, aim_ref, ldt_ref, bre_ref, bim_ref, cre_ref, cim_ref,
                     abr_ref, abi_ref, wbr_ref, wbi_ref, wcr_ref, wci_ref):
    dt = jnp.exp(ldt_ref[...])
    ar, ai = are_ref[...], aim_ref[...]
    mag = jnp.exp(dt * ar)
    abr, abi = mag * jnp.cos(dt * ai), mag * jnp.sin(dt * ai)
    den = ar * ar + ai * ai
    nr, ni = abr - 1.0, abi
    f_re = (nr * ar + ni * ai) / den
    f_im = (ni * ar - nr * ai) / den
    br, bi = bre_ref[...], bim_ref[...]
    abr_ref[...] = abr
    abi_ref[...] = abi
    bbr = f_re * br - f_im * bi
    bbi = f_re * bi + f_im * br

    gpt = SSM_KT // SSM_GROUP
    n_kt = D_SSM // SSM_KT

    def spread(x, width, period):
        sel = (lax.broadcasted_iota(jnp.int32, (period, width), 1) % period
               == lax.broadcasted_iota(jnp.int32, (period, width), 0))
        return jnp.dot(x.astype(BF16), jnp.where(sel, 1.0, 0.0).astype(BF16), preferred_element_type=F32)

    r_in = lax.broadcasted_iota(jnp.int32, (SSM_KT, SSM_NT), 0) // SSM_GROUP
    c_in = lax.broadcasted_iota(jnp.int32, (SSM_KT, SSM_NT), 1) // N_STATE
    r_out = lax.broadcasted_iota(jnp.int32, (SSM_NT, SSM_KT), 0) // N_STATE
    c_out = lax.broadcasted_iota(jnp.int32, (SSM_NT, SSM_KT), 1) // SSM_GROUP
    for kt in range(n_kt):
        gs = slice(gpt * kt, gpt * (kt + 1))
        for src, dst in ((bbr, wbr_ref), (bbi, wbi_ref)):
            x = src[gs].reshape(SSM_KT, N_STATE)
            dst[kt] = jnp.where(r_in == c_in, spread(x, SSM_NT, N_STATE), 0.0).astype(BF16)
        for src, dst in ((cre_ref, wcr_ref), (cim_ref, wci_ref)):
            x = src[gs].reshape(SSM_NT, SSM_GROUP)
            dst[kt] = jnp.where(r_out == c_out, spread(x, SSM_KT, SSM_GROUP), 0.0).astype(BF16)


def _ssm_prep(a_re, a_im, log_dt, b_re, b_im, c_re, c_im):
    gn = jax.ShapeDtypeStruct((N_GROUPS, 1, N_STATE), F32)
    w_in_shape = jax.ShapeDtypeStruct((D_SSM // SSM_KT, SSM_KT, SSM_NT), BF16)
    w_out_shape = jax.ShapeDtypeStruct((D_SSM // SSM_KT, SSM_NT, SSM_KT), BF16)
    return pl.pallas_call(
        _ssm_prep_kernel, out_shape=(gn, gn, w_in_shape, w_in_shape, w_out_shape, w_out_shape),
        name="ssm_prep")(
            a_re.reshape(N_GROUPS, 1, N_STATE), a_im.reshape(N_GROUPS, 1, N_STATE),
            log_dt.reshape(N_GROUPS, 1, 1), b_re.transpose(0, 2, 1), b_im.transpose(0, 2, 1),
            c_re.transpose(0, 2, 1), c_im.transpose(0, 2, 1))


def _ssm_epilogue(y, uf, zs, d, wglu, bglu):
    ys = jax.nn.gelu(y + d * uf)
    gate = jax.nn.sigmoid(jnp.dot(ys.astype(BF16), wglu, preferred_element_type=F32) + bglu)
    return ys * gate * (zs * jax.nn.sigmoid(zs))


def _s5_scan_kernel(u_ref, zs_ref, h0_ref, abr_ref, abi_ref, bre_ref, bim_ref, crt_ref, cit_ref,
                    d_ref, wglu_ref, bglu_ref, out_ref, hfin_ref, buf_ref, hst_ref, y_ref,
                    *, nb, tc, pitch):
    i = pl.program_id(0)

    @pl.when(i == 0)
    def _():
        hst_ref[...] = h0_ref[...]

    uf = u_ref[...].reshape(nb * tc, D_SSM)
    ub = uf.astype(BF16)
    n_kt = D_SSM // SSM_KT
    slabs_per_kt = SSM_NT // LANES
    for kt in range(n_kt):
        lhs = ub[:, SSM_KT * kt:SSM_KT * (kt + 1)]
        for ri, w_ref in ((0, bre_ref), (1, bim_ref)):
            res = jnp.dot(lhs, w_ref[kt], preferred_element_type=F32)
            for s in range(slabs_per_kt):
                for b in range(nb):
                    r0 = (ri * nb + b) * pitch
                    buf_ref[kt * slabs_per_kt + s, r0:r0 + tc, :] = (
                        res[b * tc:(b + 1) * tc, LANES * s:LANES * (s + 1)])

    row = lax.broadcasted_iota(jnp.int32, (2 * nb, LANES), 0)

    def scan_group(g, carry):
        base = g * SLAB_GROUP
        a1, a2, h = [], [], []
        for s in range(SLAB_GROUP):
            a1.append(jnp.broadcast_to(abr_ref[base + s], (2 * nb, LANES)))
            ai = jnp.broadcast_to(abi_ref[base + s], (2 * nb, LANES))
            a2.append(jnp.where(row < nb, -ai, ai))
            h.append(hst_ref[base + s])
        for t in range(tc):
            for s in range(SLAB_GROUP):
                bu = buf_ref[base + s, pl.ds(t, 2 * nb, stride=pitch), :]
                h[s] = a1[s] * h[s] + a2[s] * pltpu.roll(h[s], nb, axis=0) + bu
                buf_ref[base + s, pl.ds(t, 2 * nb, stride=pitch), :] = h[s]
        for s in range(SLAB_GROUP):
            hst_ref[base + s] = h[s]
        return carry

    lax.fori_loop(0, N_SLABS // SLAB_GROUP, scan_group, 0)

    def states(jt, ri):
        rows = []
        for b in range(nb):
            r0 = (ri * nb + b) * pitch
            rows.append(jnp.concatenate(
                [buf_ref[jt * slabs_per_kt + s, r0:r0 + tc, :] for s in range(slabs_per_kt)], axis=1))
        return jnp.concatenate(rows, axis=0).astype(BF16)

    for jt in range(n_kt):
        y_ref[:, SSM_KT * jt:SSM_KT * (jt + 1)] = (
            jnp.dot(states(jt, 0), crt_ref[jt], preferred_element_type=F32)
            - jnp.dot(states(jt, 1), cit_ref[jt], preferred_element_type=F32))

    zs = zs_ref[...].reshape(nb * tc, D_SSM)
    out = _ssm_epilogue(y_ref[...], uf, zs, d_ref[...], wglu_ref[...], bglu_ref[...])
    out_ref[...] = out.reshape(nb, tc, D_SSM).astype(BF16)

    @pl.when(i == pl.num_programs(0) - 1)
    def _():
        hfin_ref[...] = hst_ref[...]


def _s5_scan(proj3, h0, abr, abi, bre, bim, crt, cit, d, wglu, bglu, tc):
    nb, t_len = proj3.shape[0], proj3.shape[1]
    pitch = tc + SUBLANES
    const = lambda shape: pl.BlockSpec(shape, lambda i: (0,) * len(shape))
    kern = functools.partial(_s5_scan_kernel, nb=nb, tc=tc, pitch=pitch)
    return pl.pallas_call(
        kern,
        grid=(t_len // tc,),
        in_specs=[pl.BlockSpec((nb, tc, D_SSM), lambda i: (0, i, C_U // D_SSM)),
                  pl.BlockSpec((nb, tc, D_SSM), lambda i: (0, i, C_ZS // D_SSM)),
                  const((N_SLABS, 2 * nb, LANES)),
                  const((N_SLABS, 1, LANES)), const((N_SLABS, 1, LANES)),
                  const(bre.shape), const(bim.shape), const(crt.shape), const(cit.shape),
                  const((1, D_SSM)), const((D_SSM, D_SSM)), const((1, D_SSM))],
        out_specs=(pl.BlockSpec((nb, tc, D_SSM), lambda i: (0, i, 0)),
                   const((N_SLABS, 2 * nb, LANES))),
        out_shape=(jax.ShapeDtypeStruct((nb, t_len, D_SSM), BF16),
                   jax.ShapeDtypeStruct((N_SLABS, 2 * nb, LANES), F32)),
        scratch_shapes=[pltpu.VMEM((N_SLABS, 2 * nb * pitch, LANES), F32),
                        pltpu.VMEM((N_SLABS, 2 * nb, LANES), F32),
                        pltpu.VMEM((nb * tc, D_SSM), F32)],
        compiler_params=_params(("arbitrary",), 56),
        name="s5_scan")(proj3, proj3, h0, abr.reshape(N_SLABS, 1, LANES), abi.reshape(N_SLABS, 1, LANES),
                        bre, bim, crt, cit, d, wglu, bglu)


def _s5_step_kernel(u_ref, zs_ref, h0r_ref, h0i_ref, abr_ref, abi_ref, bre_ref, bim_ref, crt_ref,
                    cit_ref, d_ref, wglu_ref, bglu_ref, out_ref, hr_ref, hi_ref, y_ref):
    uf = u_ref[...]
    ub = uf.astype(BF16)
    n_kt = D_SSM // SSM_KT
    for kt in range(n_kt):
        lhs = ub[:, SSM_KT * kt:SSM_KT * (kt + 1)]
        sl = slice(SSM_NT * kt, SSM_NT * (kt + 1))
        ar, ai = abr_ref[:, sl], abi_ref[:, sl]
        h0r, h0i = h0r_ref[:, sl], h0i_ref[:, sl]
        hr_ref[:, sl] = ar * h0r - ai * h0i + jnp.dot(lhs, bre_ref[kt], preferred_element_type=F32)
        hi_ref[:, sl] = ar * h0i + ai * h0r + jnp.dot(lhs, bim_ref[kt], preferred_element_type=F32)
    for jt in range(n_kt):
        sl = slice(SSM_NT * jt, SSM_NT * (jt + 1))
        y_ref[:, SSM_KT * jt:SSM_KT * (jt + 1)] = (
            jnp.dot(hr_ref[:, sl].astype(BF16), crt_ref[jt], preferred_element_type=F32)
            - jnp.dot(hi_ref[:, sl].astype(BF16), cit_ref[jt], preferred_element_type=F32))
    out = _ssm_epilogue(y_ref[...], uf, zs_ref[...], d_ref[...], wglu_ref[...], bglu_ref[...])
    out_ref[...] = out.astype(BF16)


def _s5_step(proj, h0r, h0i, abr, abi, bre, bim, crt, cit, d, wglu, bglu):
    m = h0r.shape[0]
    const = lambda shape: pl.BlockSpec(shape, lambda i: (0,) * len(shape))
    return pl.pallas_call(
        _s5_step_kernel,
        grid=(1,),
        in_specs=[pl.BlockSpec((m, D_SSM), lambda i: (0, C_U // D_SSM)),
                  pl.BlockSpec((m, D_SSM), lambda i: (0, C_ZS // D_SSM)),
                  const((m, D_STATE)), const((m, D_STATE)),
                  const((1, D_STATE)), const((1, D_STATE)),
                  const(bre.shape), const(bim.shape), const(crt.shape), const(cit.shape),
                  const((1, D_SSM)), const((D_SSM, D_SSM)), const((1, D_SSM))],
        out_specs=(const((m, D_SSM)), const((m, D_STATE)), const((m, D_STATE))),
        out_shape=(jax.ShapeDtypeStruct((m, D_SSM), BF16),
                   jax.ShapeDtypeStruct((m, D_STATE), F32),
                   jax.ShapeDtypeStruct((m, D_STATE), F32)),
        scratch_shapes=[pltpu.VMEM((m, D_SSM), F32)],
        compiler_params=_params(("arbitrary",), 48),
        name="s5_step")(proj, proj, h0r, h0i, abr, abi, bre, bim, crt, cit, d, wglu, bglu)


def _attn_kernel(q_ref, k_ref, vt_ref, km_ref, vmt_ref, za_ref, o_ref, *scratch, tq):
    m_ref, l_ref, acc_ref = (scratch[:N_HEADS], scratch[N_HEADS:2 * N_HEADS], scratch[2 * N_HEADS:])
    qi = pl.program_id(1)
    nt = (((1,), (1,)), ((), ()))
    heads = [(slice(HEAD_Q * h, HEAD_Q * (h + 1)), slice(V_HEAD * h, V_HEAD * (h + 1)))
             for h in range(N_HEADS)]

    for h, (qs, vs) in enumerate(heads):
        s = lax.dot_general(km_ref[:, qs], q_ref[:, qs], nt, preferred_element_type=F32)
        m = jnp.max(s, axis=0, keepdims=True)
        p = jnp.exp2(s - m)
        m_ref[h][...] = m
        l_ref[h][...] = jnp.sum(p, axis=0, keepdims=True)
        acc_ref[h][...] = jnp.dot(vmt_ref[vs, :], p.astype(BF16), preferred_element_type=F32)

    def kv_block(j, masked):
        off = pl.multiple_of(j * tq, tq)
        if masked:
            key = lax.broadcasted_iota(jnp.int32, (tq, tq), 0)
            qry = lax.broadcasted_iota(jnp.int32, (tq, tq), 1)
            keep = key <= qry

        def scores(h):
            qs = heads[h][0]
            return lax.dot_general(k_ref[pl.ds(off, tq), qs], q_ref[:, qs], nt,
                                   preferred_element_type=F32)

        def accumulate(h, alpha, pb):
            vs = heads[h][1]
            acc_ref[h][...] = alpha * acc_ref[h][...] + jnp.dot(
                vt_ref[j, vs, :], pb, preferred_element_type=F32)

        s_next = scores(0)
        pending = None
        for h in range(N_HEADS):
            s = s_next
            if h + 1 < N_HEADS:
                s_next = scores(h + 1)
            if masked:
                s = jnp.where(keep, s, -jnp.inf)
            m_prev = m_ref[h][...]
            m_new = jnp.maximum(m_prev, jnp.max(s, axis=0, keepdims=True))
            alpha = jnp.exp2(m_prev - m_new)
            p = jnp.exp2(s - m_new)
            l_ref[h][...] = alpha * l_ref[h][...] + jnp.sum(p, axis=0, keepdims=True)
            m_ref[h][...] = m_new
            if pending is not None:
                accumulate(*pending)
            pending = (h, alpha, p.astype(BF16))
        accumulate(*pending)

    def full_block(j, carry):
        kv_block(j, False)
        return carry

    lax.fori_loop(0, qi, full_block, 0)
    kv_block(qi, True)
    za = za_ref[...]
    gate = za * jax.nn.sigmoid(za)
    for h, (qs, vs) in enumerate(heads):
        o = (acc_ref[h][...] / l_ref[h][...]).T
        o_ref[:, vs] = (o * gate[:, vs]).astype(BF16)


def _attention(q, kcat, vt, kmeta, vmeta_t, proj, bsz, seq, tq):
    nq = seq // tq
    kern = functools.partial(_attn_kernel, tq=tq)
    return pl.pallas_call(
        kern,
        grid=(bsz, nq),
        in_specs=[pl.BlockSpec((tq, N_HEADS * HEAD_Q), lambda b, i: (b * nq + i, 0)),
                  pl.BlockSpec((None, seq, N_HEADS * HEAD_Q), lambda b, i: (b, 0, 0)),
                  pl.BlockSpec((nq, D_ATT, tq), lambda b, i: (b, 0, 0)),
                  pl.BlockSpec((N_META, N_HEADS * HEAD_Q), lambda b, i: (0, 0)),
                  pl.BlockSpec((D_ATT, N_META), lambda b, i: (0, 0)),
                  pl.BlockSpec((tq, D_ATT), lambda b, i: (b * nq + i, C_ZA // D_ATT))],
        out_specs=pl.BlockSpec((tq, D_ATT), lambda b, i: (b * nq + i, 0)),
        out_shape=jax.ShapeDtypeStruct((bsz * seq, D_ATT), BF16),
        scratch_shapes=([pltpu.VMEM((1, tq), F32)] * (2 * N_HEADS)
                        + [pltpu.VMEM((V_HEAD, tq), F32)] * N_HEADS),
        compiler_params=_params(("parallel", "arbitrary"), 48),
        name="attn_prompt")(q, kcat.reshape(bsz, seq, N_HEADS * HEAD_Q), vt, kmeta, vmeta_t, proj)


def _qabsorb_kernel(q_ref, wukt_ref, qa_ref):
    for h in range(N_HEADS):
        qn = q_ref[:, HEAD_Q * h:HEAD_Q * h + QK_NOPE]
        qa_ref[:, KV_LORA * h:KV_LORA * (h + 1)] = jnp.dot(
            qn, wukt_ref[h], preferred_element_type=F32).astype(BF16)


def _qabsorb(q, wukt):
    m = q.shape[0]
    return pl.pallas_call(
        _qabsorb_kernel,
        out_shape=jax.ShapeDtypeStruct((m, N_HEADS * KV_LORA), BF16),
        name="q_absorb")(q, wukt)


def _decode_kernel(pt_ref, qa_ref, qp_ref, cn_ref, kn_ref, ckv_hbm, kpe_hbm, o_ref,
                   cbuf, kbuf, sem, m_ref, l_ref, acc_ref, *, npg, n_pages):
    b, j = pl.program_id(0), pl.program_id(1)
    nj = pl.num_programs(1)
    t = b * nj + j
    last = pl.num_programs(0) * nj - 1
    ahead = DECODE_SLOTS - 1
    slot = t % DECODE_SLOTS

    def page_copies(step, buf, k):
        pg = pt_ref[(step // nj) * n_pages + (step % nj) * npg + k]
        return (pltpu.make_async_copy(ckv_hbm.at[0, pg], cbuf.at[buf, k], sem.at[buf]),
                pltpu.make_async_copy(kpe_hbm.at[0, pg], kbuf.at[buf, k], sem.at[buf]))

    def start_step(step, buf):
        for k in range(npg):
            for cp in page_copies(step, buf, k):
                cp.start()

    def wait_step(step, buf):
        for k in range(npg):
            for cp in page_copies(step, buf, k):
                cp.wait()

    @pl.when(t == 0)
    def _():
        for d in range(ahead):
            start_step(jnp.minimum(d, last), d)

    t_next = jnp.minimum(t + ahead, last)
    start_step(t_next, (t + ahead) % DECODE_SLOTS)
    wait_step(t, slot)
    c_refs = [cbuf.at[slot, k] for k in range(npg)]
    k_refs = [kbuf.at[slot, k] for k in range(npg)]

    @pl.when(j == 0)
    def _():
        m_ref[...] = jnp.full(m_ref.shape, -jnp.inf, F32)
        l_ref[...] = jnp.zeros(l_ref.shape, F32)
        acc_ref[...] = jnp.zeros(acc_ref.shape, F32)

    qa, qp = qa_ref[...], qp_ref[...]
    nt = (((1,), (1,)), ((), ()))
    page = c_refs[0].shape[0]
    cb = [r[...].astype(BF16) for r in c_refs]
    s = jnp.concatenate(
        [lax.dot_general(qa, cb[k], nt, preferred_element_type=F32)
         + jnp.dot(qp, k_refs[k][...].astype(BF16), preferred_element_type=F32)
         for k in range(npg)], axis=1)
    m_prev = m_ref[...]
    m_new = jnp.maximum(m_prev, jnp.max(s, axis=-1, keepdims=True))
    alpha = jnp.exp2(m_prev - m_new)
    p32 = jnp.exp2(s - m_new)
    p = p32.astype(BF16)
    pv = jnp.dot(p[:, :page], cb[0], preferred_element_type=F32)
    for k in range(1, npg):
        pv = pv + jnp.dot(p[:, page * k:page * (k + 1)], cb[k], preferred_element_type=F32)
    l_ref[...] = alpha * l_ref[...] + jnp.sum(p32, axis=-1, keepdims=True)
    acc_ref[...] = alpha * acc_ref[...] + pv
    m_ref[...] = m_new

    @pl.when(j == pl.num_programs(1) - 1)
    def _():
        cn, kn = cn_ref[...], kn_ref[...]
        s_new = (jnp.sum(qa.astype(F32) * cn, axis=-1, keepdims=True)
                 + jnp.sum(qp.astype(F32) * kn, axis=-1, keepdims=True))
        m_old = m_ref[...]
        m_fin = jnp.maximum(m_old, s_new)
        a = jnp.exp2(m_old - m_fin)
        pn = jnp.exp2(s_new - m_fin)
        l_fin = a * l_ref[...] + pn
        o_ref[...] = (a * acc_ref[...] + pn * cn) / l_fin

    @pl.when(t == last)
    def _():
        for d in range(1, DECODE_SLOTS):
            wait_step(last, (t + d) % DECODE_SLOTS)


def _decode(page_table, qa, qp, c_new, k_new, cache_ckv, cache_kpe_t, npg):
    dbs, n_pages = page_table.shape
    page = cache_ckv.shape[2]
    per_b = lambda shape: pl.BlockSpec((None,) + shape, lambda b, j, pt: (b, 0, 0))
    grid_spec = pltpu.PrefetchScalarGridSpec(
        num_scalar_prefetch=1,
        grid=(dbs, n_pages // npg),
        in_specs=[per_b((N_HEADS, KV_LORA)), per_b((N_HEADS, QK_ROPE)),
                  per_b((1, KV_LORA)), per_b((1, QK_ROPE)),
                  pl.BlockSpec(memory_space=pl.ANY), pl.BlockSpec(memory_space=pl.ANY)],
        out_specs=per_b((N_HEADS, KV_LORA)),
        scratch_shapes=[pltpu.VMEM((DECODE_SLOTS, npg, page, KV_LORA), F32),
                        pltpu.VMEM((DECODE_SLOTS, npg, QK_ROPE, page), F32),
                        pltpu.SemaphoreType.DMA((DECODE_SLOTS,)),
                        pltpu.VMEM((N_HEADS, 1), F32), pltpu.VMEM((N_HEADS, 1), F32),
                        pltpu.VMEM((N_HEADS, KV_LORA), F32)])
    return pl.pallas_call(
        functools.partial(_decode_kernel, npg=npg, n_pages=n_pages),
        grid_spec=grid_spec,
        out_shape=jax.ShapeDtypeStruct((dbs, N_HEADS, KV_LORA), F32),
        compiler_params=_params(("arbitrary", "arbitrary"), 48),
        name="attn_decode")(page_table.reshape(-1), qa, qp, c_new, k_new, cache_ckv, cache_kpe_t)


def _ouv_kernel(ol_ref, wuv_ref, za_ref, o_ref):
    za = za_ref[...]
    gate = za * jax.nn.sigmoid(za)
    for h in range(N_HEADS):
        o = jnp.dot(ol_ref[:, KV_LORA * h:KV_LORA * (h + 1)].astype(BF16),
                    wuv_ref[:, V_HEAD * h:V_HEAD * (h + 1)], preferred_element_type=F32)
        o_ref[:, V_HEAD * h:V_HEAD * (h + 1)] = (o * gate[:, V_HEAD * h:V_HEAD * (h + 1)]).astype(BF16)


def _ouv(o_lat, wuv, proj):
    m = o_lat.shape[0]
    const = lambda shape: pl.BlockSpec(shape, lambda i: (0,) * len(shape))
    return pl.pallas_call(
        _ouv_kernel,
        grid=(1,),
        in_specs=[const((m, N_HEADS * KV_LORA)), const((KV_LORA, D_ATT)),
                  pl.BlockSpec((m, D_ATT), lambda i: (0, C_ZA // D_ATT))],
        out_specs=const((m, D_ATT)),
        out_shape=jax.ShapeDtypeStruct((m, D_ATT), BF16),
        name="o_uv")(o_lat, wuv, proj)


def _outproj_kernel(ssm_ref, att_ref, wtop_ref, wbot_ref, x_ref, g_ref, y_ref):
    h = (x_ref[...]
         + jnp.dot(ssm_ref[...], wtop_ref[...], preferred_element_type=F32)
         + jnp.dot(att_ref[...], wbot_ref[...], preferred_element_type=F32))
    y_ref[...] = _rms(h, g_ref[...])


def _outproj(ssm, att, w, x, g, tm):
    m = x.shape[0]
    return pl.pallas_call(
        _outproj_kernel,
        grid=(m // tm,),
        in_specs=[pl.BlockSpec((tm, D_SSM), lambda i: (i, 0)),
                  pl.BlockSpec((tm, D_ATT), lambda i: (i, 0)),
                  pl.BlockSpec((D_SSM, D_MODEL), lambda i: (0, 0)),
                  pl.BlockSpec((D_ATT, D_MODEL), lambda i: (1, 0)),
                  pl.BlockSpec((tm, D_MODEL), lambda i: (i, 0)),
                  pl.BlockSpec((1, D_MODEL), lambda i: (0, 0))],
        out_specs=pl.BlockSpec((tm, D_MODEL), lambda i: (i, 0)),
        out_shape=jax.ShapeDtypeStruct((m, D_MODEL), F32),
        compiler_params=_params(("parallel",), 48),
        name="outproj")(ssm, att, w, w, x, g)


def _slab_minor(h):
    return h.transpose(1, 0, 2).reshape(h.shape[1], D_STATE)


def kernel(x_prompt, x_sample, cache_ckv, cache_kpe, state_ssm_re, state_ssm_im, page_table, meta_tokens, g_norm, w_in, ssm_a_re, ssm_a_im, ssm_log_dt, ssm_b_re, ssm_b_im, ssm_c_re, ssm_c_im, ssm_d, w_glu, b_glu, g_q_norm, w_uq, g_kv_norm, w_uk, w_uv, w_out, g_final):
    bsz, seq, _ = x_prompt.shape
    dbs = x_sample.shape[0]
    n_pages, page = page_table.shape[1], cache_ckv.shape[2]
    past_len = n_pages * page
    layer = 0

    w_in_p = _win_prep(jnp.swapaxes(w_in[layer], 0, 1), 256)
    w_uq_p = jnp.pad(w_uq[layer].reshape(Q_LORA, N_HEADS, QK_NOPE + QK_ROPE),
                     ((0, 0), (0, 0), (0, HEAD_Q - QK_NOPE - QK_ROPE))
                     ).reshape(Q_LORA, N_HEADS * HEAD_Q).astype(BF16)
    w_uk_flat = w_uk[layer].reshape(KV_LORA, D_ATT).astype(BF16)
    w_uk_t = w_uk[layer].transpose(1, 2, 0).astype(BF16)
    w_uv_flat = w_uv[layer].reshape(KV_LORA, D_ATT).astype(BF16)
    w_glu_b = w_glu[layer].astype(BF16)
    w_out_b = w_out[layer].astype(BF16)
    g_in = g_norm[layer].reshape(1, D_MODEL)
    g_q = g_q_norm[layer].reshape(1, Q_LORA)
    g_kv = g_kv_norm[layer].reshape(1, KV_LORA)
    g_fin = g_final.reshape(1, D_MODEL)
    d_skip = ssm_d[layer].reshape(1, D_SSM)
    bglu = b_glu[layer].reshape(1, D_SSM)

    abr, abi, bre, bim, crt, cit = _ssm_prep(ssm_a_re[layer], ssm_a_im[layer], ssm_log_dt[layer],
                                             ssm_b_re[layer], ssm_b_im[layer], ssm_c_re[layer], ssm_c_im[layer])
    abr_f, abi_f = abr.reshape(1, D_STATE), abi.reshape(1, D_STATE)
    ssm_w = (bre, bim, crt, cit, d_skip, w_glu_b, bglu)

    pos = jnp.concatenate([jnp.arange(N_META, N_META + seq), jnp.full((dbs,), past_len), jnp.arange(N_META)])
    tabs = _rope_tables(pos)
    tabs_real = tuple(t[:seq] for t in tabs)
    tabs_small = tuple(t[seq:] for t in tabs)

    x_real = x_prompt.reshape(bsz * seq, D_MODEL)
    x_small = jnp.concatenate([x_sample.reshape(dbs, D_MODEL), meta_tokens.astype(x_prompt.dtype)], axis=0)
    n_small = dbs + N_META
    proj_real = _inproj(x_real, g_in, w_in_p, 1024, 1152)
    proj_small = _inproj(x_small, g_in, w_in_p, n_small, 768)

    w_uv_t = w_uv_flat.T
    c_real, kr_real, kcat_real, vt_real = _kvprep(proj_real, g_kv, tabs_real, w_uk_flat, w_uv_t, 512, ATT_TILE)
    c_small, kr_small, kcat_small, vt_small = _kvprep(proj_small, g_kv, tabs_small, w_uk_flat, w_uv_t,
                                                      n_small, n_small)
    q_real = _qproj(proj_real, g_q, w_uq_p, tabs_real, 512)
    q_small = _qproj(proj_small, g_q, w_uq_p, tabs_small, n_small)

    proj_meta = jnp.broadcast_to(proj_small[dbs:][None], (bsz, N_META, N_PROJ))
    h_zero = jnp.zeros((N_SLABS, 2 * bsz, LANES), F32)
    _, h_meta = _s5_scan(proj_meta, h_zero, abr_f, abi_f, *ssm_w, tc=N_META)
    ssm_real, h_fin = _s5_scan(proj_real.reshape(bsz, seq, N_PROJ), h_meta, abr_f, abi_f, *ssm_w, tc=64)
    h_fin = _slab_minor(h_fin)
    ssm_smp, hr_s, hi_s = _s5_step(proj_small, state_ssm_re[layer].reshape(dbs, D_STATE),
                                   state_ssm_im[layer].reshape(dbs, D_STATE), abr_f, abi_f, *ssm_w)

    att_real = _attention(q_real, kcat_real, vt_real, kcat_small[dbs:], vt_small[0, :, dbs:], proj_real,
                          bsz, seq, ATT_TILE)

    qs = q_small[:dbs]
    qa = _qabsorb(qs, w_uk_t).reshape(dbs, N_HEADS, KV_LORA)
    qp = qs.reshape(dbs, N_HEADS, HEAD_Q)[:, :, QK_NOPE:QK_NOPE + QK_ROPE]
    c_new = c_small[:dbs].reshape(dbs, 1, KV_LORA)
    k_new = kr_small[:dbs, :QK_ROPE].reshape(dbs, 1, QK_ROPE)
    o_lat = _decode(page_table, qa, qp, c_new, k_new, cache_ckv, jnp.swapaxes(cache_kpe, 2, 3), 32)
    att_smp = _ouv(o_lat.reshape(dbs, N_HEADS * KV_LORA), w_uv_flat, proj_small)

    y_prompt = _outproj(ssm_real.reshape(bsz * seq, D_SSM), att_real, w_out_b, x_real, g_fin, 512)
    y_sample = _outproj(ssm_smp, att_smp, w_out_b, x_sample.reshape(dbs, D_MODEL), g_fin, dbs)

    c_meta, k_meta = c_small[dbs:], kr_small[dbs:, :QK_ROPE]
    ckv_p = jnp.concatenate([jnp.broadcast_to(c_meta[None], (bsz, N_META, KV_LORA)),
                             c_real.reshape(bsz, seq, KV_LORA)], axis=1)
    kpe_p = jnp.concatenate([jnp.broadcast_to(k_meta[None], (bsz, N_META, QK_ROPE)),
                             kr_real[:, :QK_ROPE].reshape(bsz, seq, QK_ROPE)], axis=1)
    return (y_prompt.reshape(bsz, seq, D_MODEL),
            y_sample.reshape(dbs, 1, D_MODEL),
            ckv_p[None], kpe_p[None],
            h_fin[:bsz].reshape(1, bsz, N_GROUPS, N_STATE),
            h_fin[bsz:].reshape(1, bsz, N_GROUPS, N_STATE),
            c_small[:dbs].reshape(1, dbs, 1, KV_LORA),
            kr_small[:dbs, :QK_ROPE].reshape(1, dbs, 1, QK_ROPE),
            hr_s.reshape(1, dbs, N_GROUPS, N_STATE),
            hi_s.reshape(1, dbs, N_GROUPS, N_STATE))
```

```python
import functools
import math

import numpy as np
import jax
import jax.numpy as jnp
from jax import lax
from jax.experimental import pallas as pl
from jax.experimental.pallas import tpu as pltpu

F32 = jnp.float32
BF16 = jnp.bfloat16

D_MODEL = 2048
N_META = 16
D_SSM = 1024
SSM_GROUP = 16
N_GROUPS = 64
N_STATE = 64
D_STATE = N_GROUPS * N_STATE
D_ATT = 1024
V_HEAD = 128
N_HEADS = 8
QK_NOPE = 128
QK_ROPE = 64
KV_LORA = 512
Q_LORA = 768
ROPE_THETA = 10000.0
SOFTMAX_SCALE = (QK_NOPE + QK_ROPE) ** -0.5
Q_SCALE = SOFTMAX_SCALE * math.log2(math.e)
EPS = 1e-6

LANES = 128
SUBLANES = 8
MIB = 1024 * 1024

C_CQ = 0
C_KPE = 768
C_U = 1024
C_ZS = 2048
C_ZA = 3072
C_CKV = 4096
N_PROJ = 4608
HEAD_Q = 256
N_SLABS = D_STATE // LANES
SLAB_GROUP = 8

TM_INPROJ, TN_INPROJ = 1024, 1536
TM_PREP = 512
TM_OUTPROJ = 512
ATT_TILE = 512
TC_SCAN = 64
DECODE_PAGES = 32
DECODE_SLOTS = 3
VMEM_MIB = 48
VMEM_MIB_LARGE = 56
SSM_KT = 256
SSM_NT = SSM_KT * N_STATE // SSM_GROUP


def _params(sem, vmem_mib=None):
    return pltpu.CompilerParams(
        dimension_semantics=sem,
        vmem_limit_bytes=None if vmem_mib is None else vmem_mib * MIB)


def _rms(x, g):
    return x * lax.rsqrt(jnp.mean(x * x, axis=-1, keepdims=True) + EPS) * g


def _rope128(x, cos_t, sin_lo, sin_hi):
    return (x * cos_t + pltpu.roll(x, LANES - QK_ROPE // 2, axis=1) * sin_lo
            + pltpu.roll(x, QK_ROPE // 2, axis=1) * sin_hi)


def _rope_table_kernel(pos_ref, inv_ref, cos_ref, slo_ref, shi_ref):
    ang = pos_ref[...] * inv_ref[...]
    c, s = jnp.cos(ang), jnp.sin(ang)
    lane = lax.broadcasted_iota(jnp.int32, ang.shape, 1)
    half = QK_ROPE // 2
    cos_ref[...] = jnp.where(lane < QK_ROPE, c, 0.0)
    slo_ref[...] = jnp.where(lane < half, -s, 0.0)
    shi_ref[...] = jnp.where((lane >= half) & (lane < QK_ROPE), s, 0.0)


def _rope_tables(pos):
    n = pos.shape[0]
    half = QK_ROPE // 2
    inv = ROPE_THETA ** (-(np.arange(LANES) % half).astype(np.float64) / half)
    inv = jnp.asarray(inv.astype(np.float32)).reshape(1, LANES)
    shp = jax.ShapeDtypeStruct((n, LANES), F32)
    return pl.pallas_call(_rope_table_kernel, out_shape=(shp, shp, shp), name="rope_tables")(
        pos.reshape(n, 1).astype(F32), inv)


def _win_prep_kernel(wt_ref, o_ref):
    o_cq, o_ckv, o_kpe = 2 * D_SSM, 2 * D_SSM + Q_LORA, 2 * D_SSM + Q_LORA + KV_LORA
    o_za = o_kpe + QK_ROPE
    cols = o_ref.shape[1]
    o_ref[C_CQ:C_CQ + Q_LORA, :] = wt_ref[o_cq:o_ckv, :].astype(BF16)
    o_ref[C_KPE:C_KPE + QK_ROPE, :] = wt_ref[o_kpe:o_za, :].astype(BF16)
    o_ref[C_KPE + QK_ROPE:C_U, :] = jnp.zeros((C_U - C_KPE - QK_ROPE, cols), BF16)
    o_ref[C_U:C_ZA, :] = wt_ref[:o_cq, :].astype(BF16)
    o_ref[C_ZA:C_CKV, :] = wt_ref[o_za:, :].astype(BF16)
    o_ref[C_CKV:, :] = wt_ref[o_ckv:o_kpe, :].astype(BF16)


def _win_prep(w_t, tc):
    n, k = w_t.shape
    return pl.pallas_call(
        _win_prep_kernel,
        grid=(k // tc,),
        in_specs=[pl.BlockSpec((n, tc), lambda i: (0, i))],
        out_specs=pl.BlockSpec((N_PROJ, tc), lambda i: (0, i)),
        out_shape=jax.ShapeDtypeStruct((N_PROJ, k), BF16),
        compiler_params=_params(("parallel",)),
        name="win_prep")(w_t)


def _inproj_kernel(x_ref, g_ref, w_ref, o_ref, xn_ref):
    @pl.when(pl.program_id(1) == 0)
    def _():
        xn_ref[...] = _rms(x_ref[...], g_ref[...]).astype(BF16)

    o_ref[...] = lax.dot_general(xn_ref[...], w_ref[...], (((1,), (1,)), ((), ())),
                                 preferred_element_type=F32)


def _inproj(x, g, w_t, tm, tn):
    m = x.shape[0]
    return pl.pallas_call(
        _inproj_kernel,
        grid=(m // tm, N_PROJ // tn),
        in_specs=[pl.BlockSpec((tm, D_MODEL), lambda i, j: (i, 0)),
                  pl.BlockSpec((1, D_MODEL), lambda i, j: (0, 0)),
                  pl.BlockSpec((tn, D_MODEL), lambda i, j: (j, 0))],
        out_specs=pl.BlockSpec((tm, tn), lambda i, j: (i, j)),
        out_shape=jax.ShapeDtypeStruct((m, N_PROJ), F32),
        scratch_shapes=[pltpu.VMEM((tm, D_MODEL), BF16)],
        compiler_params=_params(("parallel", "arbitrary"), VMEM_MIB_LARGE),
        name="inproj")(x, g, w_t)


def _kvprep_kernel(ckv_ref, kpe_ref, g_ref, cos_ref, slo_ref, shi_ref, wuk_ref, wuvt_ref,
                   c_ref, kr_ref, kcat_ref, vt_ref, *, tk):
    c = _rms(ckv_ref[...], g_ref[...])
    c_ref[...] = c
    cb = c.astype(BF16)
    kr = _rope128(kpe_ref[...], cos_ref[...], slo_ref[...], shi_ref[...])
    kr_ref[...] = kr
    krb = kr.astype(BF16)
    knope = jnp.dot(cb, wuk_ref[...], preferred_element_type=F32).astype(BF16)
    for h in range(N_HEADS):
        kcat_ref[:, HEAD_Q * h:HEAD_Q * h + QK_NOPE] = knope[:, QK_NOPE * h:QK_NOPE * (h + 1)]
        kcat_ref[:, HEAD_Q * h + QK_NOPE:HEAD_Q * (h + 1)] = krb
    vt = lax.dot_general(wuvt_ref[...], cb, (((1,), (1,)), ((), ())),
                         preferred_element_type=F32).astype(BF16)
    for kb in range(vt_ref.shape[0]):
        vt_ref[kb] = vt[:, tk * kb:tk * (kb + 1)]


def _kvprep(proj, g, tabs, wuk, wuvt, tm, tk):
    m = proj.shape[0]
    nt = tabs[0].shape[0] // tm
    tab = pl.BlockSpec((tm, LANES), lambda i: (i % nt, 0))
    row = lambda w: pl.BlockSpec((tm, w), lambda i: (i, 0))
    return pl.pallas_call(
        functools.partial(_kvprep_kernel, tk=tk),
        grid=(m // tm,),
        in_specs=[pl.BlockSpec((tm, KV_LORA), lambda i: (i, C_CKV // KV_LORA)),
                  pl.BlockSpec((tm, LANES), lambda i: (i, C_KPE // LANES)),
                  pl.BlockSpec((1, KV_LORA), lambda i: (0, 0)),
                  tab, tab, tab,
                  pl.BlockSpec((KV_LORA, D_ATT), lambda i: (0, 0)),
                  pl.BlockSpec((D_ATT, KV_LORA), lambda i: (0, 0))],
        out_specs=(row(KV_LORA), row(LANES), row(N_HEADS * HEAD_Q),
                   pl.BlockSpec((tm // tk, D_ATT, tk), lambda i: (i, 0, 0))),
        out_shape=(jax.ShapeDtypeStruct((m, KV_LORA), F32),
                   jax.ShapeDtypeStruct((m, LANES), F32),
                   jax.ShapeDtypeStruct((m, N_HEADS * HEAD_Q), BF16),
                   jax.ShapeDtypeStruct((m // tk, D_ATT, tk), BF16)),
        compiler_params=_params(("parallel",)),
        name="kvprep")(proj, proj, g, *tabs, wuk, wuvt)


def _qproj_kernel(cq_ref, g_ref, w_ref, cos_ref, slo_ref, shi_ref, q_ref):
    cqn = _rms(cq_ref[...], g_ref[...]).astype(BF16)
    q = jnp.dot(cqn, w_ref[...], preferred_element_type=F32)
    cos_t, slo, shi = cos_ref[...], slo_ref[...], shi_ref[...]
    for h in range(N_HEADS):
        lo = HEAD_Q * h
        q_ref[:, lo:lo + QK_NOPE] = (q[:, lo:lo + QK_NOPE] * Q_SCALE).astype(BF16)
        pe = _rope128(q[:, lo + QK_NOPE:lo + HEAD_Q], cos_t, slo, shi)
        q_ref[:, lo + QK_NOPE:lo + HEAD_Q] = (pe * Q_SCALE).astype(BF16)


def _qproj(proj, g, w, tabs, tm):
    m = proj.shape[0]
    nt = tabs[0].shape[0] // tm
    tab = pl.BlockSpec((tm, LANES), lambda i: (i % nt, 0))
    return pl.pallas_call(
        _qproj_kernel,
        grid=(m // tm,),
        in_specs=[pl.BlockSpec((tm, Q_LORA), lambda i: (i, C_CQ // Q_LORA)),
                  pl.BlockSpec((1, Q_LORA), lambda i: (0, 0)),
                  pl.BlockSpec((Q_LORA, N_HEADS * HEAD_Q), lambda i: (0, 0)),
                  tab, tab, tab],
        out_specs=pl.BlockSpec((tm, N_HEADS * HEAD_Q), lambda i: (i, 0)),
        out_shape=jax.ShapeDtypeStruct((m, N_HEADS * HEAD_Q), BF16),
        compiler_params=_params(("parallel",)),
        name="qproj")(proj, g, w, *tabs)


def _ssm_prep_kernel(are_ref, aim_ref, ldt_ref, bre_ref, bim_ref, cre_ref, cim_ref,
                     abr_ref, abi_ref, wbr_ref, wbi_ref, wcr_ref, wci_ref):
    dt = jnp.exp(ldt_ref[...])
    ar, ai = are_ref[...], aim_ref[...]
    mag = jnp.exp(dt * ar)
    abr, abi = mag * jnp.cos(dt * ai), mag * jnp.sin(dt * ai)
    den = ar * ar + ai * ai
    nr, ni = abr - 1.0, abi
    f_re = (nr * ar + ni * ai) / den
    f_im = (ni * ar - nr * ai) / den
    br, bi = bre_ref[...], bim_ref[...]
    abr_ref[...] = abr
    abi_ref[...] = abi
    bbr = f_re * br - f_im * bi
    bbi = f_re * bi + f_im * br

    gpt = SSM_KT // SSM_GROUP
    n_kt = D_SSM // SSM_KT

    def spread(x, width, period):
        sel = (lax.broadcasted_iota(jnp.int32, (period, width), 1) % period
               == lax.broadcasted_iota(jnp.int32, (period, width), 0))
        return jnp.dot(x.astype(BF16), jnp.where(sel, 1.0, 0.0).astype(BF16), preferred_element_type=F32)

    r_in = lax.broadcasted_iota(jnp.int32, (SSM_KT, SSM_NT), 0) // SSM_GROUP
    c_in = lax.broadcasted_iota(jnp.int32, (SSM_KT, SSM_NT), 1) // N_STATE
    r_out = lax.broadcasted_iota(jnp.int32, (SSM_NT, SSM_KT), 0) // N_STATE
    c_out = lax.broadcasted_iota(jnp.int32, (SSM_NT, SSM_KT), 1) // SSM_GROUP
    for kt in range(n_kt):
        gs = slice(gpt * kt, gpt * (kt + 1))
        for src, dst in ((bbr, wbr_ref), (bbi, wbi_ref)):
            x = src[gs].reshape(SSM_KT, N_STATE)
            dst[kt] = jnp.where(r_in == c_in, spread(x, SSM_NT, N_STATE), 0.0).astype(BF16)
        for src, dst in ((cre_ref, wcr_ref), (cim_ref, wci_ref)):
            x = src[gs].reshape(SSM_NT, SSM_GROUP)
            dst[kt] = jnp.where(r_out == c_out, spread(x, SSM_KT, SSM_GROUP), 0.0).astype(BF16)


def _ssm_prep(a_re, a_im, log_dt, b_re, b_im, c_re, c_im):
    gn = jax.ShapeDtypeStruct((N_GROUPS, 1, N_STATE), F32)
    w_in_shape = jax.ShapeDtypeStruct((D_SSM // SSM_KT, SSM_KT, SSM_NT), BF16)
    w_out_shape = jax.ShapeDtypeStruct((D_SSM // SSM_KT, SSM_NT, SSM_KT), BF16)
    return pl.pallas_call(
        _ssm_prep_kernel, out_shape=(gn, gn, w_in_shape, w_in_shape, w_out_shape, w_out_shape),
        name="ssm_prep")(
            a_re.reshape(N_GROUPS, 1, N_STATE), a_im.reshape(N_GROUPS, 1, N_STATE),
            log_dt.reshape(N_GROUPS, 1, 1), b_re.transpose(0, 2, 1), b_im.transpose(0, 2, 1),
            c_re.transpose(0, 2, 1), c_im.transpose(0, 2, 1))


def _ssm_epilogue(y, uf, zs, d, wglu, bglu):
    ys = jax.nn.gelu(y + d * uf)
    gate = jax.nn.sigmoid(jnp.dot(ys.astype(BF16), wglu, preferred_element_type=F32) + bglu)
    return ys * gate * (zs * jax.nn.sigmoid(zs))


def _s5_scan_kernel(u_ref, zs_ref, h0_ref, abr_ref, abi_ref, bre_ref, bim_ref, crt_ref, cit_ref,
                    d_ref, wglu_ref, bglu_ref, out_ref, hfin_ref, buf_ref, hst_ref, y_ref,
                    *, nb, tc, pitch):
    i = pl.program_id(0)

    @pl.when(i == 0)
    def _():
        hst_ref[...] = h0_ref[...]

    uf = u_ref[...].reshape(nb * tc, D_SSM)
    ub = uf.astype(BF16)
    n_kt = D_SSM // SSM_KT
    slabs_per_kt = SSM_NT // LANES
    row = lax.broadcasted_iota(jnp.int32, (2 * nb, LANES), 0)

    def drive(kt):
        lhs = ub[:, SSM_KT * kt:SSM_KT * (kt + 1)]
        for ri, w_ref in ((0, bre_ref), (1, bim_ref)):
            res = jnp.dot(lhs, w_ref[kt], preferred_element_type=F32)
            for s in range(slabs_per_kt):
                for b in range(nb):
                    r0 = (ri * nb + b) * pitch
                    buf_ref[kt * slabs_per_kt + s, r0:r0 + tc, :] = (
                        res[b * tc:(b + 1) * tc, LANES * s:LANES * (s + 1)])

    def scan_group(g):
        base = g * SLAB_GROUP
        a1, a2, h = [], [], []
        for s in range(SLAB_GROUP):
            a1.append(jnp.broadcast_to(abr_ref[base + s], (2 * nb, LANES)))
            ai = jnp.broadcast_to(abi_ref[base + s], (2 * nb, LANES))
            a2.append(jnp.where(row < nb, -ai, ai))
            h.append(hst_ref[base + s])
        for t in range(tc):
            for s in range(SLAB_GROUP):
                bu = buf_ref[base + s, pl.ds(t, 2 * nb, stride=pitch), :]
                h[s] = a1[s] * h[s] + a2[s] * pltpu.roll(h[s], nb, axis=0) + bu
                buf_ref[base + s, pl.ds(t, 2 * nb, stride=pitch), :] = h[s]
        for s in range(SLAB_GROUP):
            hst_ref[base + s] = h[s]

    def states(jt, ri):
        rows = []
        for b in range(nb):
            r0 = (ri * nb + b) * pitch
            rows.append(jnp.concatenate(
                [buf_ref[jt * slabs_per_kt + s, r0:r0 + tc, :] for s in range(slabs_per_kt)], axis=1))
        return jnp.concatenate(rows, axis=0).astype(BF16)

    def readout(jt):
        y_ref[:, SSM_KT * jt:SSM_KT * (jt + 1)] = (
            jnp.dot(states(jt, 0), crt_ref[jt], preferred_element_type=F32)
            - jnp.dot(states(jt, 1), cit_ref[jt], preferred_element_type=F32))

    def scan_step(g, carry):
        scan_group(g)
        return carry

    for kt in range(n_kt):
        drive(kt)
    lax.fori_loop(0, N_SLABS // SLAB_GROUP, scan_step, 0)
    for jt in range(n_kt):
        readout(jt)

    zs = zs_ref[...].reshape(nb * tc, D_SSM)
    out = _ssm_epilogue(y_ref[...], uf, zs, d_ref[...], wglu_ref[...], bglu_ref[...])
    out_ref[...] = out.reshape(nb, tc, D_SSM).astype(BF16)

    @pl.when(i == pl.num_programs(0) - 1)
    def _():
        hfin_ref[...] = hst_ref[...]


def _s5_scan(proj3, h0, abr, abi, bre, bim, crt, cit, d, wglu, bglu, tc):
    nb, t_len = proj3.shape[0], proj3.shape[1]
    pitch = tc + SUBLANES
    const = lambda shape: pl.BlockSpec(shape, lambda i: (0,) * len(shape))
    kern = functools.partial(_s5_scan_kernel, nb=nb, tc=tc, pitch=pitch)
    return pl.pallas_call(
        kern,
        grid=(t_len // tc,),
        in_specs=[pl.BlockSpec((nb, tc, D_SSM), lambda i: (0, i, C_U // D_SSM)),
                  pl.BlockSpec((nb, tc, D_SSM), lambda i: (0, i, C_ZS // D_SSM)),
                  const((N_SLABS, 2 * nb, LANES)),
                  const((N_SLABS, 1, LANES)), const((N_SLABS, 1, LANES)),
                  const(bre.shape), const(bim.shape), const(crt.shape), const(cit.shape),
                  const((1, D_SSM)), const((D_SSM, D_SSM)), const((1, D_SSM))],
        out_specs=(pl.BlockSpec((nb, tc, D_SSM), lambda i: (0, i, 0)),
                   const((N_SLABS, 2 * nb, LANES))),
        out_shape=(jax.ShapeDtypeStruct((nb, t_len, D_SSM), BF16),
                   jax.ShapeDtypeStruct((N_SLABS, 2 * nb, LANES), F32)),
        scratch_shapes=[pltpu.VMEM((N_SLABS, 2 * nb * pitch, LANES), F32),
                        pltpu.VMEM((N_SLABS, 2 * nb, LANES), F32),
                        pltpu.VMEM((nb * tc, D_SSM), F32)],
        compiler_params=_params(("arbitrary",), VMEM_MIB_LARGE),
        name="s5_scan")(proj3, proj3, h0, abr.reshape(N_SLABS, 1, LANES), abi.reshape(N_SLABS, 1, LANES),
                        bre, bim, crt, cit, d, wglu, bglu)


def _s5_step_kernel(u_ref, zs_ref, h0r_ref, h0i_ref, abr_ref, abi_ref, bre_ref, bim_ref, crt_ref,
                    cit_ref, d_ref, wglu_ref, bglu_ref, out_ref, hr_ref, hi_ref, y_ref):
    uf = u_ref[...]
    ub = uf.astype(BF16)
    n_kt = D_SSM // SSM_KT
    for kt in range(n_kt):
        lhs = ub[:, SSM_KT * kt:SSM_KT * (kt + 1)]
        sl = slice(SSM_NT * kt, SSM_NT * (kt + 1))
        ar, ai = abr_ref[:, sl], abi_ref[:, sl]
        h0r, h0i = h0r_ref[:, sl], h0i_ref[:, sl]
        hr_ref[:, sl] = ar * h0r - ai * h0i + jnp.dot(lhs, bre_ref[kt], preferred_element_type=F32)
        hi_ref[:, sl] = ar * h0i + ai * h0r + jnp.dot(lhs, bim_ref[kt], preferred_element_type=F32)
    for jt in range(n_kt):
        sl = slice(SSM_NT * jt, SSM_NT * (jt + 1))
        y_ref[:, SSM_KT * jt:SSM_KT * (jt + 1)] = (
            jnp.dot(hr_ref[:, sl].astype(BF16), crt_ref[jt], preferred_element_type=F32)
            - jnp.dot(hi_ref[:, sl].astype(BF16), cit_ref[jt], preferred_element_type=F32))
    out = _ssm_epilogue(y_ref[...], uf, zs_ref[...], d_ref[...], wglu_ref[...], bglu_ref[...])
    out_ref[...] = out.astype(BF16)


def _s5_step(proj, h0r, h0i, abr, abi, bre, bim, crt, cit, d, wglu, bglu):
    m = h0r.shape[0]
    const = lambda shape: pl.BlockSpec(shape, lambda i: (0,) * len(shape))
    return pl.pallas_call(
        _s5_step_kernel,
        grid=(1,),
        in_specs=[pl.BlockSpec((m, D_SSM), lambda i: (0, C_U // D_SSM)),
                  pl.BlockSpec((m, D_SSM), lambda i: (0, C_ZS // D_SSM)),
                  const((m, D_STATE)), const((m, D_STATE)),
                  const((1, D_STATE)), const((1, D_STATE)),
                  const(bre.shape), const(bim.shape), const(crt.shape), const(cit.shape),
                  const((1, D_SSM)), const((D_SSM, D_SSM)), const((1, D_SSM))],
        out_specs=(const((m, D_SSM)), const((m, D_STATE)), const((m, D_STATE))),
        out_shape=(jax.ShapeDtypeStruct((m, D_SSM), BF16),
                   jax.ShapeDtypeStruct((m, D_STATE), F32),
                   jax.ShapeDtypeStruct((m, D_STATE), F32)),
        scratch_shapes=[pltpu.VMEM((m, D_SSM), F32)],
        compiler_params=_params(("arbitrary",), VMEM_MIB),
        name="s5_step")(proj, proj, h0r, h0i, abr, abi, bre, bim, crt, cit, d, wglu, bglu)


def _attn_kernel(q_ref, k_ref, vt_ref, km_ref, vmt_ref, za_ref, o_ref, *scratch, tq):
    m_ref, l_ref, acc_ref = (scratch[:N_HEADS], scratch[N_HEADS:2 * N_HEADS], scratch[2 * N_HEADS:])
    qi = pl.program_id(1)
    nt = (((1,), (1,)), ((), ()))
    heads = [(slice(HEAD_Q * h, HEAD_Q * (h + 1)), slice(V_HEAD * h, V_HEAD * (h + 1)))
             for h in range(N_HEADS)]

    for h, (qs, vs) in enumerate(heads):
        s = lax.dot_general(km_ref[:, qs], q_ref[:, qs], nt, preferred_element_type=F32)
        m = jnp.max(s, axis=0, keepdims=True)
        p = jnp.exp2(s - m)
        m_ref[h][...] = m
        l_ref[h][...] = jnp.sum(p, axis=0, keepdims=True)
        acc_ref[h][...] = jnp.dot(vmt_ref[vs, :], p.astype(BF16), preferred_element_type=F32)

    def kv_block(j, masked):
        off = pl.multiple_of(j * tq, tq)
        if masked:
            key = lax.broadcasted_iota(jnp.int32, (tq, tq), 0)
            qry = lax.broadcasted_iota(jnp.int32, (tq, tq), 1)
            keep = key <= qry

        def scores(h):
            qs = heads[h][0]
            return lax.dot_general(k_ref[pl.ds(off, tq), qs], q_ref[:, qs], nt,
                                   preferred_element_type=F32)

        def accumulate(h, alpha, pb):
            vs = heads[h][1]
            acc_ref[h][...] = alpha * acc_ref[h][...] + jnp.dot(
                vt_ref[j, vs, :], pb, preferred_element_type=F32)

        s_next = scores(0)
        pending = None
        for h in range(N_HEADS):
            s = s_next
            if h + 1 < N_HEADS:
                s_next = scores(h + 1)
            if masked:
                s = jnp.where(keep, s, -jnp.inf)
            m_prev = m_ref[h][...]
            m_new = jnp.maximum(m_prev, jnp.max(s, axis=0, keepdims=True))
            alpha = jnp.exp2(m_prev - m_new)
            p = jnp.exp2(s - m_new)
            l_ref[h][...] = alpha * l_ref[h][...] + jnp.sum(p, axis=0, keepdims=True)
            m_ref[h][...] = m_new
            if pending is not None:
                accumulate(*pending)
            pending = (h, alpha, p.astype(BF16))
        accumulate(*pending)

    def full_block(j, carry):
        kv_block(j, False)
        return carry

    lax.fori_loop(0, qi, full_block, 0)
    kv_block(qi, True)
    za = za_ref[...]
    gate = za * jax.nn.sigmoid(za)
    for h, (qs, vs) in enumerate(heads):
        o = (acc_ref[h][...] / l_ref[h][...]).T
        o_ref[:, vs] = (o * gate[:, vs]).astype(BF16)


def _attention(q, kcat, vt, kmeta, vmeta_t, proj, bsz, seq, tq):
    nq = seq // tq
    kern = functools.partial(_attn_kernel, tq=tq)
    return pl.pallas_call(
        kern,
        grid=(bsz, nq),
        in_specs=[pl.BlockSpec((tq, N_HEADS * HEAD_Q), lambda b, i: (b * nq + i, 0)),
                  pl.BlockSpec((None, seq, N_HEADS * HEAD_Q), lambda b, i: (b, 0, 0)),
                  pl.BlockSpec((nq, D_ATT, tq), lambda b, i: (b, 0, 0)),
                  pl.BlockSpec((N_META, N_HEADS * HEAD_Q), lambda b, i: (0, 0)),
                  pl.BlockSpec((D_ATT, N_META), lambda b, i: (0, 0)),
                  pl.BlockSpec((tq, D_ATT), lambda b, i: (b * nq + i, C_ZA // D_ATT))],
        out_specs=pl.BlockSpec((tq, D_ATT), lambda b, i: (b * nq + i, 0)),
        out_shape=jax.ShapeDtypeStruct((bsz * seq, D_ATT), BF16),
        scratch_shapes=([pltpu.VMEM((1, tq), F32)] * (2 * N_HEADS)
                        + [pltpu.VMEM((V_HEAD, tq), F32)] * N_HEADS),
        compiler_params=_params(("parallel", "arbitrary"), VMEM_MIB),
        name="attn_prompt")(q, kcat.reshape(bsz, seq, N_HEADS * HEAD_Q), vt, kmeta, vmeta_t, proj)


def _qabsorb_kernel(q_ref, wukt_ref, qa_ref):
    for h in range(N_HEADS):
        qn = q_ref[:, HEAD_Q * h:HEAD_Q * h + QK_NOPE]
        qa_ref[:, KV_LORA * h:KV_LORA * (h + 1)] = jnp.dot(
            qn, wukt_ref[h], preferred_element_type=F32).astype(BF16)


def _qabsorb(q, wukt):
    m = q.shape[0]
    return pl.pallas_call(
        _qabsorb_kernel,
        out_shape=jax.ShapeDtypeStruct((m, N_HEADS * KV_LORA), BF16),
        name="q_absorb")(q, wukt)


def _decode_kernel(pt_ref, qa_ref, qp_ref, cn_ref, kn_ref, ckv_hbm, kpe_hbm, o_ref,
                   cbuf, kbuf, sem, m_ref, l_ref, acc_ref, *, npg, n_pages):
    b, j = pl.program_id(0), pl.program_id(1)
    nj = pl.num_programs(1)
    t = b * nj + j
    last = pl.num_programs(0) * nj - 1
    ahead = DECODE_SLOTS - 1
    slot = t % DECODE_SLOTS

    def page_copies(step, buf, k):
        pg = pt_ref[(step // nj) * n_pages + (step % nj) * npg + k]
        return (pltpu.make_async_copy(ckv_hbm.at[0, pg], cbuf.at[buf, k], sem.at[buf]),
                pltpu.make_async_copy(kpe_hbm.at[0, pg], kbuf.at[buf, k], sem.at[buf]))

    def start_step(step, buf):
        for k in range(npg):
            for cp in page_copies(step, buf, k):
                cp.start()

    def wait_step(step, buf):
        for k in range(npg):
            for cp in page_copies(step, buf, k):
                cp.wait()

    @pl.when(t == 0)
    def _():
        for d in range(ahead):
            start_step(jnp.minimum(d, last), d)

    t_next = jnp.minimum(t + ahead, last)
    start_step(t_next, (t + ahead) % DECODE_SLOTS)
    wait_step(t, slot)
    c_refs = [cbuf.at[slot, k] for k in range(npg)]
    k_refs = [kbuf.at[slot, k] for k in range(npg)]

    @pl.when(j == 0)
    def _():
        m_ref[...] = jnp.full(m_ref.shape, -jnp.inf, F32)
        l_ref[...] = jnp.zeros(l_ref.shape, F32)
        acc_ref[...] = jnp.zeros(acc_ref.shape, F32)

    qa, qp = qa_ref[...], qp_ref[...]
    nt = (((1,), (1,)), ((), ()))
    page = c_refs[0].shape[0]
    cb = [r[...].astype(BF16) for r in c_refs]
    s = jnp.concatenate(
        [lax.dot_general(qa, cb[k], nt, preferred_element_type=F32)
         + jnp.dot(qp, k_refs[k][...].astype(BF16), preferred_element_type=F32)
         for k in range(npg)], axis=1)
    m_prev = m_ref[...]
    m_new = jnp.maximum(m_prev, jnp.max(s, axis=-1, keepdims=True))
    alpha = jnp.exp2(m_prev - m_new)
    p32 = jnp.exp2(s - m_new)
    p = p32.astype(BF16)
    pv = jnp.dot(p[:, :page], cb[0], preferred_element_type=F32)
    for k in range(1, npg):
        pv = pv + jnp.dot(p[:, page * k:page * (k + 1)], cb[k], preferred_element_type=F32)
    l_ref[...] = alpha * l_ref[...] + jnp.sum(p32, axis=-1, keepdims=True)
    acc_ref[...] = alpha * acc_ref[...] + pv
    m_ref[...] = m_new

    @pl.when(j == pl.num_programs(1) - 1)
    def _():
        cn, kn = cn_ref[...], kn_ref[...]
        s_new = (jnp.sum(qa.astype(F32) * cn, axis=-1, keepdims=True)
                 + jnp.sum(qp.astype(F32) * kn, axis=-1, keepdims=True))
        m_old = m_ref[...]
        m_fin = jnp.maximum(m_old, s_new)
        a = jnp.exp2(m_old - m_fin)
        pn = jnp.exp2(s_new - m_fin)
        l_fin = a * l_ref[...] + pn
        o_ref[...] = (a * acc_ref[...] + pn * cn) / l_fin

    @pl.when(t == last)
    def _():
        for d in range(1, DECODE_SLOTS):
            wait_step(last, (t + d) % DECODE_SLOTS)


def _decode(page_table, qa, qp, c_new, k_new, cache_ckv, cache_kpe_t, npg):
    dbs, n_pages = page_table.shape
    page = cache_ckv.shape[2]
    per_b = lambda shape: pl.BlockSpec((None,) + shape, lambda b, j, pt: (b, 0, 0))
    grid_spec = pltpu.PrefetchScalarGridSpec(
        num_scalar_prefetch=1,
        grid=(dbs, n_pages // npg),
        in_specs=[per_b((N_HEADS, KV_LORA)), per_b((N_HEADS, QK_ROPE)),
                  per_b((1, KV_LORA)), per_b((1, QK_ROPE)),
                  pl.BlockSpec(memory_space=pl.ANY), pl.BlockSpec(memory_space=pl.ANY)],
        out_specs=per_b((N_HEADS, KV_LORA)),
        scratch_shapes=[pltpu.VMEM((DECODE_SLOTS, npg, page, KV_LORA), F32),
                        pltpu.VMEM((DECODE_SLOTS, npg, QK_ROPE, page), F32),
                        pltpu.SemaphoreType.DMA((DECODE_SLOTS,)),
                        pltpu.VMEM((N_HEADS, 1), F32), pltpu.VMEM((N_HEADS, 1), F32),
                        pltpu.VMEM((N_HEADS, KV_LORA), F32)])
    return pl.pallas_call(
        functools.partial(_decode_kernel, npg=npg, n_pages=n_pages),
        grid_spec=grid_spec,
        out_shape=jax.ShapeDtypeStruct((dbs, N_HEADS, KV_LORA), F32),
        compiler_params=_params(("arbitrary", "arbitrary"), VMEM_MIB),
        name="attn_decode")(page_table.reshape(-1), qa, qp, c_new, k_new, cache_ckv, cache_kpe_t)


def _ouv_kernel(ol_ref, wuv_ref, za_ref, o_ref):
    za = za_ref[...]
    gate = za * jax.nn.sigmoid(za)
    for h in range(N_HEADS):
        o = jnp.dot(ol_ref[:, KV_LORA * h:KV_LORA * (h + 1)].astype(BF16),
                    wuv_ref[:, V_HEAD * h:V_HEAD * (h + 1)], preferred_element_type=F32)
        o_ref[:, V_HEAD * h:V_HEAD * (h + 1)] = (o * gate[:, V_HEAD * h:V_HEAD * (h + 1)]).astype(BF16)


def _ouv(o_lat, wuv, proj):
    m = o_lat.shape[0]
    const = lambda shape: pl.BlockSpec(shape, lambda i: (0,) * len(shape))
    return pl.pallas_call(
        _ouv_kernel,
        grid=(1,),
        in_specs=[const((m, N_HEADS * KV_LORA)), const((KV_LORA, D_ATT)),
                  pl.BlockSpec((m, D_ATT), lambda i: (0, C_ZA // D_ATT))],
        out_specs=const((m, D_ATT)),
        out_shape=jax.ShapeDtypeStruct((m, D_ATT), BF16),
        name="o_uv")(o_lat, wuv, proj)


def _outproj_kernel(ssm_ref, att_ref, wtop_ref, wbot_ref, x_ref, g_ref, y_ref):
    h = (x_ref[...]
         + jnp.dot(ssm_ref[...], wtop_ref[...], preferred_element_type=F32)
         + jnp.dot(att_ref[...], wbot_ref[...], preferred_element_type=F32))
    y_ref[...] = _rms(h, g_ref[...])


def _outproj(ssm, att, w, x, g, tm):
    m = x.shape[0]
    return pl.pallas_call(
        _outproj_kernel,
        grid=(m // tm,),
        in_specs=[pl.BlockSpec((tm, D_SSM), lambda i: (i, 0)),
                  pl.BlockSpec((tm, D_ATT), lambda i: (i, 0)),
                  pl.BlockSpec((D_SSM, D_MODEL), lambda i: (0, 0)),
                  pl.BlockSpec((D_ATT, D_MODEL), lambda i: (1, 0)),
                  pl.BlockSpec((tm, D_MODEL), lambda i: (i, 0)),
                  pl.BlockSpec((1, D_MODEL), lambda i: (0, 0))],
        out_specs=pl.BlockSpec((tm, D_MODEL), lambda i: (i, 0)),
        out_shape=jax.ShapeDtypeStruct((m, D_MODEL), F32),
        compiler_params=_params(("parallel",), VMEM_MIB),
        name="outproj")(ssm, att, w, w, x, g)


def _slab_minor(h):
    return h.transpose(1, 0, 2).reshape(h.shape[1], D_STATE)


def kernel(x_prompt, x_sample, cache_ckv, cache_kpe, state_ssm_re, state_ssm_im, page_table, meta_tokens, g_norm, w_in, ssm_a_re, ssm_a_im, ssm_log_dt, ssm_b_re, ssm_b_im, ssm_c_re, ssm_c_im, ssm_d, w_glu, b_glu, g_q_norm, w_uq, g_kv_norm, w_uk, w_uv, w_out, g_final):
    bsz, seq, _ = x_prompt.shape
    dbs = x_sample.shape[0]
    n_pages, page = page_table.shape[1], cache_ckv.shape[2]
    past_len = n_pages * page
    layer = 0

    w_in_p = _win_prep(jnp.swapaxes(w_in[layer], 0, 1), 256)
    w_uq_p = jnp.pad(w_uq[layer].reshape(Q_LORA, N_HEADS, QK_NOPE + QK_ROPE),
                     ((0, 0), (0, 0), (0, HEAD_Q - QK_NOPE - QK_ROPE))
                     ).reshape(Q_LORA, N_HEADS * HEAD_Q).astype(BF16)
    w_uk_flat = w_uk[layer].reshape(KV_LORA, D_ATT).astype(BF16)
    w_uk_t = w_uk[layer].transpose(1, 2, 0).astype(BF16)
    w_uv_flat = w_uv[layer].reshape(KV_LORA, D_ATT).astype(BF16)
    w_glu_b = w_glu[layer].astype(BF16)
    w_out_b = w_out[layer].astype(BF16)
    g_in = g_norm[layer].reshape(1, D_MODEL)
    g_q = g_q_norm[layer].reshape(1, Q_LORA)
    g_kv = g_kv_norm[layer].reshape(1, KV_LORA)
    g_fin = g_final.reshape(1, D_MODEL)
    d_skip = ssm_d[layer].reshape(1, D_SSM)
    bglu = b_glu[layer].reshape(1, D_SSM)

    abr, abi, bre, bim, crt, cit = _ssm_prep(ssm_a_re[layer], ssm_a_im[layer], ssm_log_dt[layer],
                                             ssm_b_re[layer], ssm_b_im[layer], ssm_c_re[layer], ssm_c_im[layer])
    abr_f, abi_f = abr.reshape(1, D_STATE), abi.reshape(1, D_STATE)
    ssm_w = (bre, bim, crt, cit, d_skip, w_glu_b, bglu)

    pos = jnp.concatenate([jnp.arange(N_META, N_META + seq), jnp.full((dbs,), past_len), jnp.arange(N_META)])
    tabs = _rope_tables(pos)
    tabs_real = tuple(t[:seq] for t in tabs)
    tabs_small = tuple(t[seq:] for t in tabs)

    x_real = x_prompt.reshape(bsz * seq, D_MODEL)
    x_small = jnp.concatenate([x_sample.reshape(dbs, D_MODEL), meta_tokens.astype(x_prompt.dtype)], axis=0)
    n_small = dbs + N_META
    proj_real = _inproj(x_real, g_in, w_in_p, TM_INPROJ, TN_INPROJ)
    proj_small = _inproj(x_small, g_in, w_in_p, n_small, TN_INPROJ)

    w_uv_t = w_uv_flat.T
    c_real, kr_real, kcat_real, vt_real = _kvprep(proj_real, g_kv, tabs_real, w_uk_flat, w_uv_t, TM_PREP, ATT_TILE)
    c_small, kr_small, kcat_small, vt_small = _kvprep(proj_small, g_kv, tabs_small, w_uk_flat, w_uv_t,
                                                      n_small, n_small)
    q_real = _qproj(proj_real, g_q, w_uq_p, tabs_real, TM_PREP)
    q_small = _qproj(proj_small, g_q, w_uq_p, tabs_small, n_small)

    proj_meta = jnp.broadcast_to(proj_small[dbs:][None], (bsz, N_META, N_PROJ))
    h_zero = jnp.zeros((N_SLABS, 2 * bsz, LANES), F32)
    _, h_meta = _s5_scan(proj_meta, h_zero, abr_f, abi_f, *ssm_w, tc=N_META)
    ssm_real, h_fin = _s5_scan(proj_real.reshape(bsz, seq, N_PROJ), h_meta, abr_f, abi_f, *ssm_w, tc=TC_SCAN)
    h_fin = _slab_minor(h_fin)
    ssm_smp, hr_s, hi_s = _s5_step(proj_small, state_ssm_re[layer].reshape(dbs, D_STATE),
                                   state_ssm_im[layer].reshape(dbs, D_STATE), abr_f, abi_f, *ssm_w)

    att_real = _attention(q_real, kcat_real, vt_real, kcat_small[dbs:], vt_small[0, :, dbs:], proj_real,
                          bsz, seq, ATT_TILE)

    qs = q_small[:dbs]
    qa = _qabsorb(qs, w_uk_t).reshape(dbs, N_HEADS, KV_LORA)
    qp = qs.reshape(dbs, N_HEADS, HEAD_Q)[:, :, QK_NOPE:QK_NOPE + QK_ROPE]
    c_new = c_small[:dbs].reshape(dbs, 1, KV_LORA)
    k_new = kr_small[:dbs, :QK_ROPE].reshape(dbs, 1, QK_ROPE)
    o_lat = _decode(page_table, qa, qp, c_new, k_new, cache_ckv, jnp.swapaxes(cache_kpe, 2, 3), DECODE_PAGES)
    att_smp = _ouv(o_lat.reshape(dbs, N_HEADS * KV_LORA), w_uv_flat, proj_small)

    y_prompt = _outproj(ssm_real.reshape(bsz * seq, D_SSM), att_real, w_out_b, x_real, g_fin, TM_OUTPROJ)
    y_sample = _outproj(ssm_smp, att_smp, w_out_b, x_sample.reshape(dbs, D_MODEL), g_fin, dbs)

    c_meta, k_meta = c_small[dbs:], kr_small[dbs:, :QK_ROPE]
    ckv_p = jnp.concatenate([jnp.broadcast_to(c_meta[None], (bsz, N_META, KV_LORA)),
                             c_real.reshape(bsz, seq, KV_LORA)], axis=1)
    kpe_p = jnp.concatenate([jnp.broadcast_to(k_meta[None], (bsz, N_META, QK_ROPE)),
                             kr_real[:, :QK_ROPE].reshape(bsz, seq, QK_ROPE)], axis=1)
    return (y_prompt.reshape(bsz, seq, D_MODEL),
            y_sample.reshape(dbs, 1, D_MODEL),
            ckv_p[None], kpe_p[None],
            h_fin[:bsz].reshape(1, bsz, N_GROUPS, N_STATE),
            h_fin[bsz:].reshape(1, bsz, N_GROUPS, N_STATE),
            c_small[:dbs].reshape(1, dbs, 1, KV_LORA),
            kr_small[:dbs, :QK_ROPE].reshape(1, dbs, 1, QK_ROPE),
            hr_s.reshape(1, dbs, N_GROUPS, N_STATE),
            hi_s.reshape(1, dbs, N_GROUPS, N_STATE))
```

```python
import functools
import math

import numpy as np
import jax
import jax.numpy as jnp
from jax import lax
from jax.experimental import pallas as pl
from jax.experimental.pallas import tpu as pltpu

F32 = jnp.float32
BF16 = jnp.bfloat16

D_MODEL = 2048
N_META = 16
D_SSM = 1024
SSM_GROUP = 16
N_GROUPS = 64
N_STATE = 64
D_STATE = N_GROUPS * N_STATE
D_ATT = 1024
V_HEAD = 128
N_HEADS = 8
QK_NOPE = 128
QK_ROPE = 64
KV_LORA = 512
Q_LORA = 768
ROPE_THETA = 10000.0
SOFTMAX_SCALE = (QK_NOPE + QK_ROPE) ** -0.5
Q_SCALE = SOFTMAX_SCALE * math.log2(math.e)
EPS = 1e-6

LANES = 128
SUBLANES = 8
MIB = 1024 * 1024

C_CQ = 0
C_KPE = 768
C_U = 1024
C_ZS = 2048
C_ZA = 3072
C_CKV = 4096
N_PROJ = 4608
HEAD_Q = 256
N_SLABS = D_STATE // LANES
SLAB_GROUP = 8

TM_INPROJ, TN_INPROJ = 1024, 1536
TM_PREP = 512
TM_OUTPROJ = 512
ATT_TILE = 512
TC_SCAN = 64
DECODE_PAGES = 32
DECODE_SLOTS = 3
VMEM_MIB = 48
VMEM_MIB_LARGE = 56
SSM_KT = 256
SSM_NT = SSM_KT * N_STATE // SSM_GROUP


def _params(sem, vmem_mib=None):
    return pltpu.CompilerParams(
        dimension_semantics=sem,
        vmem_limit_bytes=None if vmem_mib is None else vmem_mib * MIB)


def _rms(x, g):
    return x * lax.rsqrt(jnp.mean(x * x, axis=-1, keepdims=True) + EPS) * g


def _rope128(x, cos_t, sin_lo, sin_hi):
    return (x * cos_t + pltpu.roll(x, LANES - QK_ROPE // 2, axis=1) * sin_lo
            + pltpu.roll(x, QK_ROPE // 2, axis=1) * sin_hi)


def _rope_table_kernel(pos_ref, inv_ref, cos_ref, slo_ref, shi_ref):
    ang = pos_ref[...] * inv_ref[...]
    c, s = jnp.cos(ang), jnp.sin(ang)
    lane = lax.broadcasted_iota(jnp.int32, ang.shape, 1)
    half = QK_ROPE // 2
    cos_ref[...] = jnp.where(lane < QK_ROPE, c, 0.0)
    slo_ref[...] = jnp.where(lane < half, -s, 0.0)
    shi_ref[...] = jnp.where((lane >= half) & (lane < QK_ROPE), s, 0.0)


def _rope_tables(pos):
    n = pos.shape[0]
    half = QK_ROPE // 2
    inv = ROPE_THETA ** (-(np.arange(LANES) % half).astype(np.float64) / half)
    inv = jnp.asarray(inv.astype(np.float32)).reshape(1, LANES)
    shp = jax.ShapeDtypeStruct((n, LANES), F32)
    return pl.pallas_call(_rope_table_kernel, out_shape=(shp, shp, shp), name="rope_tables")(
        pos.reshape(n, 1).astype(F32), inv)


def _win_prep_kernel(wt_ref, o_ref):
    o_cq, o_ckv, o_kpe = 2 * D_SSM, 2 * D_SSM + Q_LORA, 2 * D_SSM + Q_LORA + KV_LORA
    o_za = o_kpe + QK_ROPE
    cols = o_ref.shape[1]
    o_ref[C_CQ:C_CQ + Q_LORA, :] = wt_ref[o_cq:o_ckv, :].astype(BF16)
    o_ref[C_KPE:C_KPE + QK_ROPE, :] = wt_ref[o_kpe:o_za, :].astype(BF16)
    o_ref[C_KPE + QK_ROPE:C_U, :] = jnp.zeros((C_U - C_KPE - QK_ROPE, cols), BF16)
    o_ref[C_U:C_ZA, :] = wt_ref[:o_cq, :].astype(BF16)
    o_ref[C_ZA:C_CKV, :] = wt_ref[o_za:, :].astype(BF16)
    o_ref[C_CKV:, :] = wt_ref[o_ckv:o_kpe, :].astype(BF16)


def _win_prep(w_t, tc):
    n, k = w_t.shape
    return pl.pallas_call(
        _win_prep_kernel,
        grid=(k // tc,),
        in_specs=[pl.BlockSpec((n, tc), lambda i: (0, i))],
        out_specs=pl.BlockSpec((N_PROJ, tc), lambda i: (0, i)),
        out_shape=jax.ShapeDtypeStruct((N_PROJ, k), BF16),
        compiler_params=_params(("parallel",)),
        name="win_prep")(w_t)


def _inproj_kernel(x_ref, g_ref, w_ref, o_ref, xn_ref):
    @pl.when(pl.program_id(1) == 0)
    def _():
        xn_ref[...] = _rms(x_ref[...], g_ref[...]).astype(BF16)

    o_ref[...] = lax.dot_general(xn_ref[...], w_ref[...], (((1,), (1,)), ((), ())),
                                 preferred_element_type=F32)


def _inproj(x, g, w_t, tm, tn):
    m = x.shape[0]
    return pl.pallas_call(
        _inproj_kernel,
        grid=(m // tm, N_PROJ // tn),
        in_specs=[pl.BlockSpec((tm, D_MODEL), lambda i, j: (i, 0)),
                  pl.BlockSpec((1, D_MODEL), lambda i, j: (0, 0)),
                  pl.BlockSpec((tn, D_MODEL), lambda i, j: (j, 0))],
        out_specs=pl.BlockSpec((tm, tn), lambda i, j: (i, j)),
        out_shape=jax.ShapeDtypeStruct((m, N_PROJ), F32),
        scratch_shapes=[pltpu.VMEM((tm, D_MODEL), BF16)],
        compiler_params=_params(("parallel", "arbitrary"), VMEM_MIB_LARGE),
        name="inproj")(x, g, w_t)


def _kvprep_kernel(ckv_ref, kpe_ref, g_ref, cos_ref, slo_ref, shi_ref, wuk_ref, wuvt_ref,
                   c_ref, kr_ref, kcat_ref, vt_ref, *, tk):
    c = _rms(ckv_ref[...], g_ref[...])
    c_ref[...] = c
    cb = c.astype(BF16)
    kr = _rope128(kpe_ref[...], cos_ref[...], slo_ref[...], shi_ref[...])
    kr_ref[...] = kr
    krb = kr.astype(BF16)
    knope = jnp.dot(cb, wuk_ref[...], preferred_element_type=F32).astype(BF16)
    for h in range(N_HEADS):
        kcat_ref[:, HEAD_Q * h:HEAD_Q * h + QK_NOPE] = knope[:, QK_NOPE * h:QK_NOPE * (h + 1)]
        kcat_ref[:, HEAD_Q * h + QK_NOPE:HEAD_Q * (h + 1)] = krb
    vt = lax.dot_general(wuvt_ref[...], cb, (((1,), (1,)), ((), ())),
                         preferred_element_type=F32).astype(BF16)
    for kb in range(vt_ref.shape[0]):
        vt_ref[kb] = vt[:, tk * kb:tk * (kb + 1)]


def _kvprep(proj, g, tabs, wuk, wuvt, tm, tk):
    m = proj.shape[0]
    nt = tabs[0].shape[0] // tm
    tab = pl.BlockSpec((tm, LANES), lambda i: (i % nt, 0))
    row = lambda w: pl.BlockSpec((tm, w), lambda i: (i, 0))
    return pl.pallas_call(
        functools.partial(_kvprep_kernel, tk=tk),
        grid=(m // tm,),
        in_specs=[pl.BlockSpec((tm, KV_LORA), lambda i: (i, C_CKV // KV_LORA)),
                  pl.BlockSpec((tm, LANES), lambda i: (i, C_KPE // LANES)),
                  pl.BlockSpec((1, KV_LORA), lambda i: (0, 0)),
                  tab, tab, tab,
                  pl.BlockSpec((KV_LORA, D_ATT), lambda i: (0, 0)),
                  pl.BlockSpec((D_ATT, KV_LORA), lambda i: (0, 0))],
        out_specs=(row(KV_LORA), row(LANES), row(N_HEADS * HEAD_Q),
                   pl.BlockSpec((tm // tk, D_ATT, tk), lambda i: (i, 0, 0))),
        out_shape=(jax.ShapeDtypeStruct((m, KV_LORA), F32),
                   jax.ShapeDtypeStruct((m, LANES), F32),
                   jax.ShapeDtypeStruct((m, N_HEADS * HEAD_Q), BF16),
                   jax.ShapeDtypeStruct((m // tk, D_ATT, tk), BF16)),
        compiler_params=_params(("parallel",)),
        name="kvprep")(proj, proj, g, *tabs, wuk, wuvt)


def _qproj_kernel(cq_ref, g_ref, w_ref, cos_ref, slo_ref, shi_ref, q_ref):
    cqn = _rms(cq_ref[...], g_ref[...]).astype(BF16)
    q = jnp.dot(cqn, w_ref[...], preferred_element_type=F32)
    cos_t, slo, shi = cos_ref[...], slo_ref[...], shi_ref[...]
    for h in range(N_HEADS):
        lo = HEAD_Q * h
        q_ref[:, lo:lo + QK_NOPE] = (q[:, lo:lo + QK_NOPE] * Q_SCALE).astype(BF16)
        pe = _rope128(q[:, lo + QK_NOPE:lo + HEAD_Q], cos_t, slo, shi)
        q_ref[:, lo + QK_NOPE:lo + HEAD_Q] = (pe * Q_SCALE).astype(BF16)


def _qproj(proj, g, w, tabs, tm):
    m = proj.shape[0]
    nt = tabs[0].shape[0] // tm
    tab = pl.BlockSpec((tm, LANES), lambda i: (i % nt, 0))
    return pl.pallas_call(
        _qproj_kernel,
        grid=(m // tm,),
        in_specs=[pl.BlockSpec((tm, Q_LORA), lambda i: (i, C_CQ // Q_LORA)),
                  pl.BlockSpec((1, Q_LORA), lambda i: (0, 0)),
                  pl.BlockSpec((Q_LORA, N_HEADS * HEAD_Q), lambda i: (0, 0)),
                  tab, tab, tab],
        out_specs=pl.BlockSpec((tm, N_HEADS * HEAD_Q), lambda i: (i, 0)),
        out_shape=jax.ShapeDtypeStruct((m, N_HEADS * HEAD_Q), BF16),
        compiler_params=_params(("parallel",)),
        name="qproj")(proj, g, w, *tabs)


def _ssm_prep_kernel(are_ref, aim_ref, ldt_ref, bre_ref, bim_ref, cre_ref, cim_ref,
                     abr_ref, abi_ref, wbr_ref, wbi_ref, wcr_ref, wci_ref):
    dt = jnp.exp(ldt_ref[...])
    ar, ai = are_ref[...], aim_ref[...]
    mag = jnp.exp(dt * ar)
    abr, abi = mag * jnp.cos(dt * ai), mag * jnp.sin(dt * ai)
    den = ar * ar + ai * ai
    nr, ni = abr - 1.0, abi
    f_re = (nr * ar + ni * ai) / den
    f_im = (ni * ar - nr * ai) / den
    br, bi = bre_ref[...], bim_ref[...]
    abr_ref[...] = abr
    abi_ref[...] = abi
    bbr = f_re * br - f_im * bi
    bbi = f_re * bi + f_im * br

    gpt = SSM_KT // SSM_GROUP
    n_kt = D_SSM // SSM_KT

    def spread(x, width, period):
        sel = (lax.broadcasted_iota(jnp.int32, (period, width), 1) % period
               == lax.broadcasted_iota(jnp.int32, (period, width), 0))
        return jnp.dot(x.astype(BF16), jnp.where(sel, 1.0, 0.0).astype(BF16), preferred_element_type=F32)

    r_in = lax.broadcasted_iota(jnp.int32, (SSM_KT, SSM_NT), 0) // SSM_GROUP
    c_in = lax.broadcasted_iota(jnp.int32, (SSM_KT, SSM_NT), 1) // N_STATE
    r_out = lax.broadcasted_iota(jnp.int32, (SSM_NT, SSM_KT), 0) // N_STATE
    c_out = lax.broadcasted_iota(jnp.int32, (SSM_NT, SSM_KT), 1) // SSM_GROUP
    for kt in range(n_kt):
        gs = slice(gpt * kt, gpt * (kt + 1))
        for src, dst in ((bbr, wbr_ref), (bbi, wbi_ref)):
            x = src[gs].reshape(SSM_KT, N_STATE)
            dst[kt] = jnp.where(r_in == c_in, spread(x, SSM_NT, N_STATE), 0.0).astype(BF16)
        for src, dst in ((cre_ref, wcr_ref), (cim_ref, wci_ref)):
            x = src[gs].reshape(SSM_NT, SSM_GROUP)
            dst[kt] = jnp.where(r_out == c_out, spread(x, SSM_KT, SSM_GROUP), 0.0).astype(BF16)


def _ssm_prep(a_re, a_im, log_dt, b_re, b_im, c_re, c_im):
    gn = jax.ShapeDtypeStruct((N_GROUPS, 1, N_STATE), F32)
    w_in_shape = jax.ShapeDtypeStruct((D_SSM // SSM_KT, SSM_KT, SSM_NT), BF16)
    w_out_shape = jax.ShapeDtypeStruct((D_SSM // SSM_KT, SSM_NT, SSM_KT), BF16)
    return pl.pallas_call(
        _ssm_prep_kernel, out_shape=(gn, gn, w_in_shape, w_in_shape, w_out_shape, w_out_shape),
        name="ssm_prep")(
            a_re.reshape(N_GROUPS, 1, N_STATE), a_im.reshape(N_GROUPS, 1, N_STATE),
            log_dt.reshape(N_GROUPS, 1, 1), b_re.transpose(0, 2, 1), b_im.transpose(0, 2, 1),
            c_re.transpose(0, 2, 1), c_im.transpose(0, 2, 1))


def _ssm_epilogue(y, uf, zs, d, wglu, bglu):
    ys = jax.nn.gelu(y + d * uf)
    gate = jax.nn.sigmoid(jnp.dot(ys.astype(BF16), wglu, preferred_element_type=F32) + bglu)
    return ys * gate * (zs * jax.nn.sigmoid(zs))


def _s5_scan_kernel(u_ref, zs_ref, h0_ref, abr_ref, abi_ref, bre_ref, bim_ref, crt_ref, cit_ref,
                    d_ref, wglu_ref, bglu_ref, out_ref, hfin_ref, buf_ref, hst_ref, y_ref,
                    *, nb, tc, pitch):
    i = pl.program_id(0)

    @pl.when(i == 0)
    def _():
        hst_ref[...] = h0_ref[...]

    uf = u_ref[...].reshape(nb * tc, D_SSM)
    ub = uf.astype(BF16)
    n_kt = D_SSM // SSM_KT
    slabs_per_kt = SSM_NT // LANES
    row = lax.broadcasted_iota(jnp.int32, (2 * nb, LANES), 0)

    def drive(kt):
        lhs = ub[:, SSM_KT * kt:SSM_KT * (kt + 1)]
        for ri, w_ref in ((0, bre_ref), (1, bim_ref)):
            res = jnp.dot(lhs, w_ref[kt], preferred_element_type=F32)
            for s in range(slabs_per_kt):
                for b in range(nb):
                    r0 = (ri * nb + b) * pitch
                    buf_ref[kt * slabs_per_kt + s, r0:r0 + tc, :] = (
                        res[b * tc:(b + 1) * tc, LANES * s:LANES * (s + 1)])

    def scan_group(g):
        base = g * SLAB_GROUP
        a1, a2, h = [], [], []
        for s in range(SLAB_GROUP):
            a1.append(jnp.broadcast_to(abr_ref[base + s], (2 * nb, LANES)))
            ai = jnp.broadcast_to(abi_ref[base + s], (2 * nb, LANES))
            a2.append(jnp.where(row < nb, -ai, ai))
            h.append(hst_ref[base + s])
        for t in range(tc):
            for s in range(SLAB_GROUP):
                bu = buf_ref[base + s, pl.ds(t, 2 * nb, stride=pitch), :]
                h[s] = a1[s] * h[s] + a2[s] * pltpu.roll(h[s], nb, axis=0) + bu
                buf_ref[base + s, pl.ds(t, 2 * nb, stride=pitch), :] = h[s]
        for s in range(SLAB_GROUP):
            hst_ref[base + s] = h[s]

    def states(jt, ri):
        rows = []
        for b in range(nb):
            r0 = (ri * nb + b) * pitch
            rows.append(jnp.concatenate(
                [buf_ref[jt * slabs_per_kt + s, r0:r0 + tc, :] for s in range(slabs_per_kt)], axis=1))
        return jnp.concatenate(rows, axis=0).astype(BF16)

    def readout(jt):
        y_ref[:, SSM_KT * jt:SSM_KT * (jt + 1)] = (
            jnp.dot(states(jt, 0), crt_ref[jt], preferred_element_type=F32)
            - jnp.dot(states(jt, 1), cit_ref[jt], preferred_element_type=F32))

    def scan_step(g, carry):
        scan_group(g)
        return carry

    for kt in range(n_kt):
        drive(kt)
    lax.fori_loop(0, N_SLABS // SLAB_GROUP, scan_step, 0)
    for jt in range(n_kt):
        readout(jt)

    zs = zs_ref[...].reshape(nb * tc, D_SSM)
    out = _ssm_epilogue(y_ref[...], uf, zs, d_ref[...], wglu_ref[...], bglu_ref[...])
    out_ref[...] = out.reshape(nb, tc, D_SSM).astype(BF16)

    @pl.when(i == pl.num_programs(0) - 1)
    def _():
        hfin_ref[...] = hst_ref[...]


def _s5_scan(proj3, h0, abr, abi, bre, bim, crt, cit, d, wglu, bglu, tc):
    nb, t_len = proj3.shape[0], proj3.shape[1]
    pitch = tc + SUBLANES
    const = lambda shape: pl.BlockSpec(shape, lambda i: (0,) * len(shape))
    kern = functools.partial(_s5_scan_kernel, nb=nb, tc=tc, pitch=pitch)
    return pl.pallas_call(
        kern,
        grid=(t_len // tc,),
        in_specs=[pl.BlockSpec((nb, tc, D_SSM), lambda i: (0, i, C_U // D_SSM)),
                  pl.BlockSpec((nb, tc, D_SSM), lambda i: (0, i, C_ZS // D_SSM)),
                  const((N_SLABS, 2 * nb, LANES)),
                  const((N_SLABS, 1, LANES)), const((N_SLABS, 1, LANES)),
                  const(bre.shape), const(bim.shape), const(crt.shape), const(cit.shape),
                  const((1, D_SSM)), const((D_SSM, D_SSM)), const((1, D_SSM))],
        out_specs=(pl.BlockSpec((nb, tc, D_SSM), lambda i: (0, i, 0)),
                   const((N_SLABS, 2 * nb, LANES))),
        out_shape=(jax.ShapeDtypeStruct((nb, t_len, D_SSM), BF16),
                   jax.ShapeDtypeStruct((N_SLABS, 2 * nb, LANES), F32)),
        scratch_shapes=[pltpu.VMEM((N_SLABS, 2 * nb * pitch, LANES), F32),
                        pltpu.VMEM((N_SLABS, 2 * nb, LANES), F32),
                        pltpu.VMEM((nb * tc, D_SSM), F32)],
        compiler_params=_params(("arbitrary",), VMEM_MIB_LARGE),
        name="s5_scan")(proj3, proj3, h0, abr.reshape(N_SLABS, 1, LANES), abi.reshape(N_SLABS, 1, LANES),
                        bre, bim, crt, cit, d, wglu, bglu)


def _s5_step_kernel(u_ref, zs_ref, h0r_ref, h0i_ref, abr_ref, abi_ref, bre_ref, bim_ref, crt_ref,
                    cit_ref, d_ref, wglu_ref, bglu_ref, out_ref, hr_ref, hi_ref, y_ref):
    uf = u_ref[...]
    ub = uf.astype(BF16)
    n_kt = D_SSM // SSM_KT
    for kt in range(n_kt):
        lhs = ub[:, SSM_KT * kt:SSM_KT * (kt + 1)]
        sl = slice(SSM_NT * kt, SSM_NT * (kt + 1))
        ar, ai = abr_ref[:, sl], abi_ref[:, sl]
        h0r, h0i = h0r_ref[:, sl], h0i_ref[:, sl]
        hr_ref[:, sl] = ar * h0r - ai * h0i + jnp.dot(lhs, bre_ref[kt], preferred_element_type=F32)
        hi_ref[:, sl] = ar * h0i + ai * h0r + jnp.dot(lhs, bim_ref[kt], preferred_element_type=F32)
    for jt in range(n_kt):
        sl = slice(SSM_NT * jt, SSM_NT * (jt + 1))
        y_ref[:, SSM_KT * jt:SSM_KT * (jt + 1)] = (
            jnp.dot(hr_ref[:, sl].astype(BF16), crt_ref[jt], preferred_element_type=F32)
            - jnp.dot(hi_ref[:, sl].astype(BF16), cit_ref[jt], preferred_element_type=F32))
    out = _ssm_epilogue(y_ref[...], uf, zs_ref[...], d_ref[...], wglu_ref[...], bglu_ref[...])
    out_ref[...] = out.astype(BF16)


def _s5_step(proj, h0r, h0i, abr, abi, bre, bim, crt, cit, d, wglu, bglu):
    m = h0r.shape[0]
    const = lambda shape: pl.BlockSpec(shape, lambda i: (0,) * len(shape))
    return pl.pallas_call(
        _s5_step_kernel,
        grid=(1,),
        in_specs=[pl.BlockSpec((m, D_SSM), lambda i: (0, C_U // D_SSM)),
                  pl.BlockSpec((m, D_SSM), lambda i: (0, C_ZS // D_SSM)),
                  const((m, D_STATE)), const((m, D_STATE)),
                  const((1, D_STATE)), const((1, D_STATE)),
                  const(bre.shape), const(bim.shape), const(crt.shape), const(cit.shape),
                  const((1, D_SSM)), const((D_SSM, D_SSM)), const((1, D_SSM))],
        out_specs=(const((m, D_SSM)), const((m, D_STATE)), const((m, D_STATE))),
        out_shape=(jax.ShapeDtypeStruct((m, D_SSM), BF16),
                   jax.ShapeDtypeStruct((m, D_STATE), F32),
                   jax.ShapeDtypeStruct((m, D_STATE), F32)),
        scratch_shapes=[pltpu.VMEM((m, D_SSM), F32)],
        compiler_params=_params(("arbitrary",), VMEM_MIB),
        name="s5_step")(proj, proj, h0r, h0i, abr, abi, bre, bim, crt, cit, d, wglu, bglu)


def _attn_kernel(q_ref, k_ref, vt_ref, km_ref, vmt_ref, za_ref, o_ref, *scratch, tq):
    m_ref, l_ref, acc_ref = (scratch[:N_HEADS], scratch[N_HEADS:2 * N_HEADS], scratch[2 * N_HEADS:])
    qi = pl.program_id(1)
    nt = (((1,), (1,)), ((), ()))
    heads = [(slice(HEAD_Q * h, HEAD_Q * (h + 1)), slice(V_HEAD * h, V_HEAD * (h + 1)))
             for h in range(N_HEADS)]

    s_meta = [lax.dot_general(km_ref[:, qs], q_ref[:, qs], nt, preferred_element_type=F32)
              for qs, _ in heads]
    p_meta = []
    for h, s in enumerate(s_meta):
        m = jnp.max(s, axis=0, keepdims=True)
        p = jnp.exp2(s - m)
        m_ref[h][...] = m
        l_ref[h][...] = jnp.sum(p, axis=0, keepdims=True)
        p_meta.append(p.astype(BF16))
    for h, (_, vs) in enumerate(heads):
        acc_ref[h][...] = jnp.dot(vmt_ref[vs, :], p_meta[h], preferred_element_type=F32)

    def kv_block(j, masked):
        off = pl.multiple_of(j * tq, tq)
        if masked:
            key = lax.broadcasted_iota(jnp.int32, (tq, tq), 0)
            qry = lax.broadcasted_iota(jnp.int32, (tq, tq), 1)
            keep = key <= qry

        def scores(h):
            qs = heads[h][0]
            return lax.dot_general(k_ref[pl.ds(off, tq), qs], q_ref[:, qs], nt,
                                   preferred_element_type=F32)

        def accumulate(h, alpha, pb):
            vs = heads[h][1]
            acc_ref[h][...] = alpha * acc_ref[h][...] + jnp.dot(
                vt_ref[j, vs, :], pb, preferred_element_type=F32)

        s_next = scores(0)
        pending = None
        for h in range(N_HEADS):
            s = s_next
            if h + 1 < N_HEADS:
                s_next = scores(h + 1)
            if masked:
                s = jnp.where(keep, s, -jnp.inf)
            m_prev = m_ref[h][...]
            m_new = jnp.maximum(m_prev, jnp.max(s, axis=0, keepdims=True))
            alpha = jnp.exp2(m_prev - m_new)
            p = jnp.exp2(s - m_new)
            l_ref[h][...] = alpha * l_ref[h][...] + jnp.sum(p, axis=0, keepdims=True)
            m_ref[h][...] = m_new
            if pending is not None:
                accumulate(*pending)
            pending = (h, alpha, p.astype(BF16))
        accumulate(*pending)

    def full_block(j, carry):
        kv_block(j, False)
        return carry

    lax.fori_loop(0, qi, full_block, 0)
    kv_block(qi, True)
    za = za_ref[...]
    gate = za * jax.nn.sigmoid(za)
    for h, (qs, vs) in enumerate(heads):
        o = (acc_ref[h][...] / l_ref[h][...]).T
        o_ref[:, vs] = (o * gate[:, vs]).astype(BF16)


def _attention(q, kcat, vt, kmeta, vmeta_t, proj, bsz, seq, tq):
    nq = seq // tq
    kern = functools.partial(_attn_kernel, tq=tq)
    return pl.pallas_call(
        kern,
        grid=(bsz, nq),
        in_specs=[pl.BlockSpec((tq, N_HEADS * HEAD_Q), lambda b, i: (b * nq + i, 0)),
                  pl.BlockSpec((None, seq, N_HEADS * HEAD_Q), lambda b, i: (b, 0, 0)),
                  pl.BlockSpec((nq, D_ATT, tq), lambda b, i: (b, 0, 0)),
                  pl.BlockSpec((N_META, N_HEADS * HEAD_Q), lambda b, i: (0, 0)),
                  pl.BlockSpec((D_ATT, N_META), lambda b, i: (0, 0)),
                  pl.BlockSpec((tq, D_ATT), lambda b, i: (b * nq + i, C_ZA // D_ATT))],
        out_specs=pl.BlockSpec((tq, D_ATT), lambda b, i: (b * nq + i, 0)),
        out_shape=jax.ShapeDtypeStruct((bsz * seq, D_ATT), BF16),
        scratch_shapes=([pltpu.VMEM((1, tq), F32)] * (2 * N_HEADS)
                        + [pltpu.VMEM((V_HEAD, tq), F32)] * N_HEADS),
        compiler_params=_params(("parallel", "arbitrary"), VMEM_MIB),
        name="attn_prompt")(q, kcat.reshape(bsz, seq, N_HEADS * HEAD_Q), vt, kmeta, vmeta_t, proj)


def _qabsorb_kernel(q_ref, wukt_ref, qa_ref):
    for h in range(N_HEADS):
        qn = q_ref[:, HEAD_Q * h:HEAD_Q * h + QK_NOPE]
        qa_ref[:, KV_LORA * h:KV_LORA * (h + 1)] = jnp.dot(
            qn, wukt_ref[h], preferred_element_type=F32).astype(BF16)


def _qabsorb(q, wukt):
    m = q.shape[0]
    return pl.pallas_call(
        _qabsorb_kernel,
        out_shape=jax.ShapeDtypeStruct((m, N_HEADS * KV_LORA), BF16),
        name="q_absorb")(q, wukt)


def _decode_kernel(pt_ref, qa_ref, qp_ref, cn_ref, kn_ref, ckv_hbm, kpe_hbm, o_ref,
                   cbuf, kbuf, sem, m_ref, l_ref, acc_ref, *, npg, n_pages):
    b, j = pl.program_id(0), pl.program_id(1)
    nj = pl.num_programs(1)
    t = b * nj + j
    last = pl.num_programs(0) * nj - 1
    ahead = DECODE_SLOTS - 1
    slot = t % DECODE_SLOTS

    def page_copies(step, buf, k):
        pg = pt_ref[(step // nj) * n_pages + (step % nj) * npg + k]
        return (pltpu.make_async_copy(ckv_hbm.at[0, pg], cbuf.at[buf, k], sem.at[buf]),
                pltpu.make_async_copy(kpe_hbm.at[0, pg], kbuf.at[buf, k], sem.at[buf]))

    def start_step(step, buf):
        for k in range(npg):
            for cp in page_copies(step, buf, k):
                cp.start()

    def wait_step(step, buf):
        for k in range(npg):
            for cp in page_copies(step, buf, k):
                cp.wait()

    @pl.when(t == 0)
    def _():
        for d in range(ahead):
            start_step(jnp.minimum(d, last), d)

    t_next = jnp.minimum(t + ahead, last)
    start_step(t_next, (t + ahead) % DECODE_SLOTS)
    wait_step(t, slot)
    c_refs = [cbuf.at[slot, k] for k in range(npg)]
    k_refs = [kbuf.at[slot, k] for k in range(npg)]

    @pl.when(j == 0)
    def _():
        m_ref[...] = jnp.full(m_ref.shape, -jnp.inf, F32)
        l_ref[...] = jnp.zeros(l_ref.shape, F32)
        acc_ref[...] = jnp.zeros(acc_ref.shape, F32)

    qa, qp = qa_ref[...], qp_ref[...]
    nt = (((1,), (1,)), ((), ()))
    page = c_refs[0].shape[0]
    cb = [r[...].astype(BF16) for r in c_refs]
    s = jnp.concatenate(
        [lax.dot_general(qa, cb[k], nt, preferred_element_type=F32)
         + jnp.dot(qp, k_refs[k][...].astype(BF16), preferred_element_type=F32)
         for k in range(npg)], axis=1)
    m_prev = m_ref[...]
    m_new = jnp.maximum(m_prev, jnp.max(s, axis=-1, keepdims=True))
    alpha = jnp.exp2(m_prev - m_new)
    p32 = jnp.exp2(s - m_new)
    p = p32.astype(BF16)
    pv = jnp.dot(p[:, :page], cb[0], preferred_element_type=F32)
    for k in range(1, npg):
        pv = pv + jnp.dot(p[:, page * k:page * (k + 1)], cb[k], preferred_element_type=F32)
    l_ref[...] = alpha * l_ref[...] + jnp.sum(p32, axis=-1, keepdims=True)
    acc_ref[...] = alpha * acc_ref[...] + pv
    m_ref[...] = m_new

    @pl.when(j == pl.num_programs(1) - 1)
    def _():
        cn, kn = cn_ref[...], kn_ref[...]
        s_new = (jnp.sum(qa.astype(F32) * cn, axis=-1, keepdims=True)
                 + jnp.sum(qp.astype(F32) * kn, axis=-1, keepdims=True))
        m_old = m_ref[...]
        m_fin = jnp.maximum(m_old, s_new)
        a = jnp.exp2(m_old - m_fin)
        pn = jnp.exp2(s_new - m_fin)
        l_fin = a * l_ref[...] + pn
        o_ref[...] = (a * acc_ref[...] + pn * cn) / l_fin

    @pl.when(t == last)
    def _():
        for d in range(1, DECODE_SLOTS):
            wait_step(last, (t + d) % DECODE_SLOTS)


def _decode(page_table, qa, qp, c_new, k_new, cache_ckv, cache_kpe_t, npg):
    dbs, n_pages = page_table.shape
    page = cache_ckv.shape[2]
    per_b = lambda shape: pl.BlockSpec((None,) + shape, lambda b, j, pt: (b, 0, 0))
    grid_spec = pltpu.PrefetchScalarGridSpec(
        num_scalar_prefetch=1,
        grid=(dbs, n_pages // npg),
        in_specs=[per_b((N_HEADS, KV_LORA)), per_b((N_HEADS, QK_ROPE)),
                  per_b((1, KV_LORA)), per_b((1, QK_ROPE)),
                  pl.BlockSpec(memory_space=pl.ANY), pl.BlockSpec(memory_space=pl.ANY)],
        out_specs=per_b((N_HEADS, KV_LORA)),
        scratch_shapes=[pltpu.VMEM((DECODE_SLOTS, npg, page, KV_LORA), F32),
                        pltpu.VMEM((DECODE_SLOTS, npg, QK_ROPE, page), F32),
                        pltpu.SemaphoreType.DMA((DECODE_SLOTS,)),
                        pltpu.VMEM((N_HEADS, 1), F32), pltpu.VMEM((N_HEADS, 1), F32),
                        pltpu.VMEM((N_HEADS, KV_LORA), F32)])
    return pl.pallas_call(
        functools.partial(_decode_kernel, npg=npg, n_pages=n_pages),
        grid_spec=grid_spec,
        out_shape=jax.ShapeDtypeStruct((dbs, N_HEADS, KV_LORA), F32),
        compiler_params=_params(("arbitrary", "arbitrary"), VMEM_MIB),
        name="attn_decode")(page_table.reshape(-1), qa, qp, c_new, k_new, cache_ckv, cache_kpe_t)


def _ouv_kernel(ol_ref, wuv_ref, za_ref, o_ref):
    za = za_ref[...]
    gate = za * jax.nn.sigmoid(za)
    for h in range(N_HEADS):
        o = jnp.dot(ol_ref[:, KV_LORA * h:KV_LORA * (h + 1)].astype(BF16),
                    wuv_ref[:, V_HEAD * h:V_HEAD * (h + 1)], preferred_element_type=F32)
        o_ref[:, V_HEAD * h:V_HEAD * (h + 1)] = (o * gate[:, V_HEAD * h:V_HEAD * (h + 1)]).astype(BF16)


def _ouv(o_lat, wuv, proj):
    m = o_lat.shape[0]
    const = lambda shape: pl.BlockSpec(shape, lambda i: (0,) * len(shape))
    return pl.pallas_call(
        _ouv_kernel,
        grid=(1,),
        in_specs=[const((m, N_HEADS * KV_LORA)), const((KV_LORA, D_ATT)),
                  pl.BlockSpec((m, D_ATT), lambda i: (0, C_ZA // D_ATT))],
        out_specs=const((m, D_ATT)),
        out_shape=jax.ShapeDtypeStruct((m, D_ATT), BF16),
        name="o_uv")(o_lat, wuv, proj)


def _outproj_kernel(ssm_ref, att_ref, wtop_ref, wbot_ref, x_ref, g_ref, y_ref):
    h = (x_ref[...]
         + jnp.dot(ssm_ref[...], wtop_ref[...], preferred_element_type=F32)
         + jnp.dot(att_ref[...], wbot_ref[...], preferred_element_type=F32))
    y_ref[...] = _rms(h, g_ref[...])


def _outproj(ssm, att, w, x, g, tm):
    m = x.shape[0]
    return pl.pallas_call(
        _outproj_kernel,
        grid=(m // tm,),
        in_specs=[pl.BlockSpec((tm, D_SSM), lambda i: (i, 0)),
                  pl.BlockSpec((tm, D_ATT), lambda i: (i, 0)),
                  pl.BlockSpec((D_SSM, D_MODEL), lambda i: (0, 0)),
                  pl.BlockSpec((D_ATT, D_MODEL), lambda i: (1, 0)),
                  pl.BlockSpec((tm, D_MODEL), lambda i: (i, 0)),
                  pl.BlockSpec((1, D_MODEL), lambda i: (0, 0))],
        out_specs=pl.BlockSpec((tm, D_MODEL), lambda i: (i, 0)),
        out_shape=jax.ShapeDtypeStruct((m, D_MODEL), F32),
        compiler_params=_params(("parallel",), VMEM_MIB),
        name="outproj")(ssm, att, w, w, x, g)


def _slab_minor(h):
    return h.transpose(1, 0, 2).reshape(h.shape[1], D_STATE)


def kernel(x_prompt, x_sample, cache_ckv, cache_kpe, state_ssm_re, state_ssm_im, page_table, meta_tokens, g_norm, w_in, ssm_a_re, ssm_a_im, ssm_log_dt, ssm_b_re, ssm_b_im, ssm_c_re, ssm_c_im, ssm_d, w_glu, b_glu, g_q_norm, w_uq, g_kv_norm, w_uk, w_uv, w_out, g_final):
    bsz, seq, _ = x_prompt.shape
    dbs = x_sample.shape[0]
    n_pages, page = page_table.shape[1], cache_ckv.shape[2]
    past_len = n_pages * page
    layer = 0

    w_in_p = _win_prep(jnp.swapaxes(w_in[layer], 0, 1), 256)
    w_uq_p = jnp.pad(w_uq[layer].reshape(Q_LORA, N_HEADS, QK_NOPE + QK_ROPE),
                     ((0, 0), (0, 0), (0, HEAD_Q - QK_NOPE - QK_ROPE))
                     ).reshape(Q_LORA, N_HEADS * HEAD_Q).astype(BF16)
    w_uk_flat = w_uk[layer].reshape(KV_LORA, D_ATT).astype(BF16)
    w_uk_t = w_uk[layer].transpose(1, 2, 0).astype(BF16)
    w_uv_flat = w_uv[layer].reshape(KV_LORA, D_ATT).astype(BF16)
    w_glu_b = w_glu[layer].astype(BF16)
    w_out_b = w_out[layer].astype(BF16)
    g_in = g_norm[layer].reshape(1, D_MODEL)
    g_q = g_q_norm[layer].reshape(1, Q_LORA)
    g_kv = g_kv_norm[layer].reshape(1, KV_LORA)
    g_fin = g_final.reshape(1, D_MODEL)
    d_skip = ssm_d[layer].reshape(1, D_SSM)
    bglu = b_glu[layer].reshape(1, D_SSM)

    abr, abi, bre, bim, crt, cit = _ssm_prep(ssm_a_re[layer], ssm_a_im[layer], ssm_log_dt[layer],
                                             ssm_b_re[layer], ssm_b_im[layer], ssm_c_re[layer], ssm_c_im[layer])
    abr_f, abi_f = abr.reshape(1, D_STATE), abi.reshape(1, D_STATE)
    ssm_w = (bre, bim, crt, cit, d_skip, w_glu_b, bglu)

    pos = jnp.concatenate([jnp.arange(N_META, N_META + seq), jnp.full((dbs,), past_len), jnp.arange(N_META)])
    tabs = _rope_tables(pos)
    tabs_real = tuple(t[:seq] for t in tabs)
    tabs_small = tuple(t[seq:] for t in tabs)

    x_real = x_prompt.reshape(bsz * seq, D_MODEL)
    x_small = jnp.concatenate([x_sample.reshape(dbs, D_MODEL), meta_tokens.astype(x_prompt.dtype)], axis=0)
    n_small = dbs + N_META
    proj_real = _inproj(x_real, g_in, w_in_p, TM_INPROJ, TN_INPROJ)
    proj_small = _inproj(x_small, g_in, w_in_p, n_small, TN_INPROJ)

    w_uv_t = w_uv_flat.T
    c_real, kr_real, kcat_real, vt_real = _kvprep(proj_real, g_kv, tabs_real, w_uk_flat, w_uv_t, TM_PREP, ATT_TILE)
    c_small, kr_small, kcat_small, vt_small = _kvprep(proj_small, g_kv, tabs_small, w_uk_flat, w_uv_t,
                                                      n_small, n_small)
    q_real = _qproj(proj_real, g_q, w_uq_p, tabs_real, TM_PREP)
    q_small = _qproj(proj_small, g_q, w_uq_p, tabs_small, n_small)

    proj_meta = jnp.broadcast_to(proj_small[dbs:][None], (bsz, N_META, N_PROJ))
    h_zero = jnp.zeros((N_SLABS, 2 * bsz, LANES), F32)
    _, h_meta = _s5_scan(proj_meta, h_zero, abr_f, abi_f, *ssm_w, tc=N_META)
    ssm_real, h_fin = _s5_scan(proj_real.reshape(bsz, seq, N_PROJ), h_meta, abr_f, abi_f, *ssm_w, tc=TC_SCAN)
    h_fin = _slab_minor(h_fin)
    ssm_smp, hr_s, hi_s = _s5_step(proj_small, state_ssm_re[layer].reshape(dbs, D_STATE),
                                   state_ssm_im[layer].reshape(dbs, D_STATE), abr_f, abi_f, *ssm_w)

    att_real = _attention(q_real, kcat_real, vt_real, kcat_small[dbs:], vt_small[0, :, dbs:], proj_real,
                          bsz, seq, ATT_TILE)

    qs = q_small[:dbs]
    qa = _qabsorb(qs, w_uk_t).reshape(dbs, N_HEADS, KV_LORA)
    qp = qs.reshape(dbs, N_HEADS, HEAD_Q)[:, :, QK_NOPE:QK_NOPE + QK_ROPE]
    c_new = c_small[:dbs].reshape(dbs, 1, KV_LORA)
    k_new = kr_small[:dbs, :QK_ROPE].reshape(dbs, 1, QK_ROPE)
    o_lat = _decode(page_table, qa, qp, c_new, k_new, cache_ckv, jnp.swapaxes(cache_kpe, 2, 3), DECODE_PAGES)
    att_smp = _ouv(o_lat.reshape(dbs, N_HEADS * KV_LORA), w_uv_flat, proj_small)

    y_prompt = _outproj(ssm_real.reshape(bsz * seq, D_SSM), att_real, w_out_b, x_real, g_fin, TM_OUTPROJ)
    y_sample = _outproj(ssm_smp, att_smp, w_out_b, x_sample.reshape(dbs, D_MODEL), g_fin, dbs)

    c_meta, k_meta = c_small[dbs:], kr_small[dbs:, :QK_ROPE]
    ckv_p = jnp.concatenate([jnp.broadcast_to(c_meta[None], (bsz, N_META, KV_LORA)),
                             c_real.reshape(bsz, seq, KV_LORA)], axis=1)
    kpe_p = jnp.concatenate([jnp.broadcast_to(k_meta[None], (bsz, N_META, QK_ROPE)),
                             kr_real[:, :QK_ROPE].reshape(bsz, seq, QK_ROPE)], axis=1)
    return (y_prompt.reshape(bsz, seq, D_MODEL),
            y_sample.reshape(dbs, 1, D_MODEL),
            ckv_p[None], kpe_p[None],
            h_fin[:bsz].reshape(1, bsz, N_GROUPS, N_STATE),
            h_fin[bsz:].reshape(1, bsz, N_GROUPS, N_STATE),
            c_small[:dbs].reshape(1, dbs, 1, KV_LORA),
            kr_small[:dbs, :QK_ROPE].reshape(1, dbs, 1, QK_ROPE),
            hr_s.reshape(1, dbs, N_GROUPS, N_STATE),
            hi_s.reshape(1, dbs, N_GROUPS, N_STATE))
```

```python
import functools
import math

import numpy as np
import jax
import jax.numpy as jnp
from jax import lax
from jax.experimental import pallas as pl
from jax.experimental.pallas import tpu as pltpu

F32 = jnp.float32
BF16 = jnp.bfloat16

D_MODEL = 2048
N_META = 16
D_SSM = 1024
SSM_GROUP = 16
N_GROUPS = 64
N_STATE = 64
D_STATE = N_GROUPS * N_STATE
D_ATT = 1024
V_HEAD = 128
N_HEADS = 8
QK_NOPE = 128
QK_ROPE = 64
KV_LORA = 512
Q_LORA = 768
ROPE_THETA = 10000.0
SOFTMAX_SCALE = (QK_NOPE + QK_ROPE) ** -0.5
Q_SCALE = SOFTMAX_SCALE * math.log2(math.e)
EPS = 1e-6

LANES = 128
SUBLANES = 8
MIB = 1024 * 1024

C_CQ = 0
C_KPE = 768
C_U = 1024
C_ZS = 2048
C_ZA = 3072
C_CKV = 4096
N_PROJ = 4608
HEAD_Q = 256
N_SLABS = D_STATE // LANES
SLAB_GROUP = 8

TM_INPROJ, TN_INPROJ = 1024, 1536
TM_PREP = 1024
TM_OUTPROJ = 512
ATT_TILE = 512
TC_SCAN = 64
DECODE_PAGES = 32
DECODE_SLOTS = 4
VMEM_MIB = 48
VMEM_MIB_LARGE = 56
SSM_KT = 256
SSM_NT = SSM_KT * N_STATE // SSM_GROUP


def _params(sem, vmem_mib=None):
    return pltpu.CompilerParams(
        dimension_semantics=sem,
        vmem_limit_bytes=None if vmem_mib is None else vmem_mib * MIB)


def _rms(x, g):
    return x * lax.rsqrt(jnp.mean(x * x, axis=-1, keepdims=True) + EPS) * g


def _rope128(x, cos_t, sin_lo, sin_hi):
    return (x * cos_t + pltpu.roll(x, LANES - QK_ROPE // 2, axis=1) * sin_lo
            + pltpu.roll(x, QK_ROPE // 2, axis=1) * sin_hi)


def _rope_table_kernel(pos_ref, inv_ref, cos_ref, slo_ref, shi_ref):
    ang = pos_ref[...] * inv_ref[...]
    c, s = jnp.cos(ang), jnp.sin(ang)
    lane = lax.broadcasted_iota(jnp.int32, ang.shape, 1)
    half = QK_ROPE // 2
    cos_ref[...] = jnp.where(lane < QK_ROPE, c, 0.0)
    slo_ref[...] = jnp.where(lane < half, -s, 0.0)
    shi_ref[...] = jnp.where((lane >= half) & (lane < QK_ROPE), s, 0.0)


def _rope_tables(pos):
    n = pos.shape[0]
    half = QK_ROPE // 2
    inv = ROPE_THETA ** (-(np.arange(LANES) % half).astype(np.float64) / half)
    inv = jnp.asarray(inv.astype(np.float32)).reshape(1, LANES)
    shp = jax.ShapeDtypeStruct((n, LANES), F32)
    return pl.pallas_call(_rope_table_kernel, out_shape=(shp, shp, shp), name="rope_tables")(
        pos.reshape(n, 1).astype(F32), inv)


def _win_prep_kernel(wt_ref, o_ref):
    o_cq, o_ckv, o_kpe = 2 * D_SSM, 2 * D_SSM + Q_LORA, 2 * D_SSM + Q_LORA + KV_LORA
    o_za = o_kpe + QK_ROPE
    cols = o_ref.shape[1]
    o_ref[C_CQ:C_CQ + Q_LORA, :] = wt_ref[o_cq:o_ckv, :].astype(BF16)
    o_ref[C_KPE:C_KPE + QK_ROPE, :] = wt_ref[o_kpe:o_za, :].astype(BF16)
    o_ref[C_KPE + QK_ROPE:C_U, :] = jnp.zeros((C_U - C_KPE - QK_ROPE, cols), BF16)
    o_ref[C_U:C_ZA, :] = wt_ref[:o_cq, :].astype(BF16)
    o_ref[C_ZA:C_CKV, :] = wt_ref[o_za:, :].astype(BF16)
    o_ref[C_CKV:, :] = wt_ref[o_ckv:o_kpe, :].astype(BF16)


def _win_prep(w_t, tc):
    n, k = w_t.shape
    return pl.pallas_call(
        _win_prep_kernel,
        grid=(k // tc,),
        in_specs=[pl.BlockSpec((n, tc), lambda i: (0, i))],
        out_specs=pl.BlockSpec((N_PROJ, tc), lambda i: (0, i)),
        out_shape=jax.ShapeDtypeStruct((N_PROJ, k), BF16),
        compiler_params=_params(("parallel",)),
        name="win_prep")(w_t)


def _inproj_kernel(x_ref, g_ref, w_ref, o_ref, xn_ref):
    @pl.when(pl.program_id(1) == 0)
    def _():
        xn_ref[...] = _rms(x_ref[...], g_ref[...]).astype(BF16)

    o_ref[...] = lax.dot_general(xn_ref[...], w_ref[...], (((1,), (1,)), ((), ())),
                                 preferred_element_type=F32)


def _inproj(x, g, w_t, tm, tn):
    m = x.shape[0]
    return pl.pallas_call(
        _inproj_kernel,
        grid=(m // tm, N_PROJ // tn),
        in_specs=[pl.BlockSpec((tm, D_MODEL), lambda i, j: (i, 0)),
                  pl.BlockSpec((1, D_MODEL), lambda i, j: (0, 0)),
                  pl.BlockSpec((tn, D_MODEL), lambda i, j: (j, 0))],
        out_specs=pl.BlockSpec((tm, tn), lambda i, j: (i, j)),
        out_shape=jax.ShapeDtypeStruct((m, N_PROJ), F32),
        scratch_shapes=[pltpu.VMEM((tm, D_MODEL), BF16)],
        compiler_params=_params(("parallel", "arbitrary"), VMEM_MIB_LARGE),
        name="inproj")(x, g, w_t)


def _kvprep_kernel(ckv_ref, kpe_ref, g_ref, cos_ref, slo_ref, shi_ref, wuk_ref, wuvt_ref,
                   c_ref, kr_ref, kcat_ref, vt_ref, *, tk):
    c = _rms(ckv_ref[...], g_ref[...])
    c_ref[...] = c
    cb = c.astype(BF16)
    kr = _rope128(kpe_ref[...], cos_ref[...], slo_ref[...], shi_ref[...])
    kr_ref[...] = kr
    krb = kr.astype(BF16)
    knope = jnp.dot(cb, wuk_ref[...], preferred_element_type=F32).astype(BF16)
    for h in range(N_HEADS):
        kcat_ref[:, HEAD_Q * h:HEAD_Q * h + QK_NOPE] = knope[:, QK_NOPE * h:QK_NOPE * (h + 1)]
        kcat_ref[:, HEAD_Q * h + QK_NOPE:HEAD_Q * (h + 1)] = krb
    vt = lax.dot_general(wuvt_ref[...], cb, (((1,), (1,)), ((), ())),
                         preferred_element_type=F32).astype(BF16)
    for kb in range(vt_ref.shape[0]):
        vt_ref[kb] = vt[:, tk * kb:tk * (kb + 1)]


def _kvprep(proj, g, tabs, wuk, wuvt, tm, tk):
    m = proj.shape[0]
    nt = tabs[0].shape[0] // tm
    tab = pl.BlockSpec((tm, LANES), lambda i: (i % nt, 0))
    row = lambda w: pl.BlockSpec((tm, w), lambda i: (i, 0))
    return pl.pallas_call(
        functools.partial(_kvprep_kernel, tk=tk),
        grid=(m // tm,),
        in_specs=[pl.BlockSpec((tm, KV_LORA), lambda i: (i, C_CKV // KV_LORA)),
                  pl.BlockSpec((tm, LANES), lambda i: (i, C_KPE // LANES)),
                  pl.BlockSpec((1, KV_LORA), lambda i: (0, 0)),
                  tab, tab, tab,
                  pl.BlockSpec((KV_LORA, D_ATT), lambda i: (0, 0)),
                  pl.BlockSpec((D_ATT, KV_LORA), lambda i: (0, 0))],
        out_specs=(row(KV_LORA), row(LANES), row(N_HEADS * HEAD_Q),
                   pl.BlockSpec((tm // tk, D_ATT, tk), lambda i: (i, 0, 0))),
        out_shape=(jax.ShapeDtypeStruct((m, KV_LORA), F32),
                   jax.ShapeDtypeStruct((m, LANES), F32),
                   jax.ShapeDtypeStruct((m, N_HEADS * HEAD_Q), BF16),
                   jax.ShapeDtypeStruct((m // tk, D_ATT, tk), BF16)),
        compiler_params=_params(("parallel",), VMEM_MIB),
        name="kvprep")(proj, proj, g, *tabs, wuk, wuvt)


def _qproj_kernel(cq_ref, g_ref, w_ref, cos_ref, slo_ref, shi_ref, q_ref):
    cqn = _rms(cq_ref[...], g_ref[...]).astype(BF16)
    q = jnp.dot(cqn, w_ref[...], preferred_element_type=F32)
    cos_t, slo, shi = cos_ref[...], slo_ref[...], shi_ref[...]
    for h in range(N_HEADS):
        lo = HEAD_Q * h
        q_ref[:, lo:lo + QK_NOPE] = (q[:, lo:lo + QK_NOPE] * Q_SCALE).astype(BF16)
        pe = _rope128(q[:, lo + QK_NOPE:lo + HEAD_Q], cos_t, slo, shi)
        q_ref[:, lo + QK_NOPE:lo + HEAD_Q] = (pe * Q_SCALE).astype(BF16)


def _qproj(proj, g, w, tabs, tm):
    m = proj.shape[0]
    nt = tabs[0].shape[0] // tm
    tab = pl.BlockSpec((tm, LANES), lambda i: (i % nt, 0))
    return pl.pallas_call(
        _qproj_kernel,
        grid=(m // tm,),
        in_specs=[pl.BlockSpec((tm, Q_LORA), lambda i: (i, C_CQ // Q_LORA)),
                  pl.BlockSpec((1, Q_LORA), lambda i: (0, 0)),
                  pl.BlockSpec((Q_LORA, N_HEADS * HEAD_Q), lambda i: (0, 0)),
                  tab, tab, tab],
        out_specs=pl.BlockSpec((tm, N_HEADS * HEAD_Q), lambda i: (i, 0)),
        out_shape=jax.ShapeDtypeStruct((m, N_HEADS * HEAD_Q), BF16),
        compiler_params=_params(("parallel",), VMEM_MIB),
        name="qproj")(proj, g, w, *tabs)


def _ssm_prep_kernel(are_ref, aim_ref, ldt_ref, bre_ref, bim_ref, cre_ref, cim_ref,
                     abr_ref, abi_ref, wbr_ref, wbi_ref, wcr_ref, wci_ref):
    dt = jnp.exp(ldt_ref[...])
    ar, ai = are_ref[...], aim_ref[...]
    mag = jnp.exp(dt * ar)
    abr, abi = mag * jnp.cos(dt * ai), mag * jnp.sin(dt * ai)
    den = ar * ar + ai * ai
    nr, ni = abr - 1.0, abi
    f_re = (nr * ar + ni * ai) / den
    f_im = (ni * ar - nr * ai) / den
    br, bi = bre_ref[...], bim_ref[...]
    abr_ref[...] = abr
    abi_ref[...] = abi
    bbr = f_re * br - f_im * bi
    bbi = f_re * bi + f_im * br

    gpt = SSM_KT // SSM_GROUP
    n_kt = D_SSM // SSM_KT

    def spread(x, width, period):
        sel = (lax.broadcasted_iota(jnp.int32, (period, width), 1) % period
               == lax.broadcasted_iota(jnp.int32, (period, width), 0))
        return jnp.dot(x.astype(BF16), jnp.where(sel, 1.0, 0.0).astype(BF16), preferred_element_type=F32)

    r_in = lax.broadcasted_iota(jnp.int32, (SSM_KT, SSM_NT), 0) // SSM_GROUP
    c_in = lax.broadcasted_iota(jnp.int32, (SSM_KT, SSM_NT), 1) // N_STATE
    r_out = lax.broadcasted_iota(jnp.int32, (SSM_NT, SSM_KT), 0) // N_STATE
    c_out = lax.broadcasted_iota(jnp.int32, (SSM_NT, SSM_KT), 1) // SSM_GROUP
    for kt in range(n_kt):
        gs = slice(gpt * kt, gpt * (kt + 1))
        for src, dst in ((bbr, wbr_ref), (bbi, wbi_ref)):
            x = src[gs].reshape(SSM_KT, N_STATE)
            dst[kt] = jnp.where(r_in == c_in, spread(x, SSM_NT, N_STATE), 0.0).astype(BF16)
        for src, dst in ((cre_ref, wcr_ref), (cim_ref, wci_ref)):
            x = src[gs].reshape(SSM_NT, SSM_GROUP)
            dst[kt] = jnp.where(r_out == c_out, spread(x, SSM_KT, SSM_GROUP), 0.0).astype(BF16)


def _ssm_prep(a_re, a_im, log_dt, b_re, b_im, c_re, c_im):
    gn = jax.ShapeDtypeStruct((N_GROUPS, 1, N_STATE), F32)
    w_in_shape = jax.ShapeDtypeStruct((D_SSM // SSM_KT, SSM_KT, SSM_NT), BF16)
    w_out_shape = jax.ShapeDtypeStruct((D_SSM // SSM_KT, SSM_NT, SSM_KT), BF16)
    return pl.pallas_call(
        _ssm_prep_kernel, out_shape=(gn, gn, w_in_shape, w_in_shape, w_out_shape, w_out_shape),
        name="ssm_prep")(
            a_re.reshape(N_GROUPS, 1, N_STATE), a_im.reshape(N_GROUPS, 1, N_STATE),
            log_dt.reshape(N_GROUPS, 1, 1), b_re.transpose(0, 2, 1), b_im.transpose(0, 2, 1),
            c_re.transpose(0, 2, 1), c_im.transpose(0, 2, 1))


def _ssm_epilogue(y, uf, zs, d, wglu, bglu):
    ys = jax.nn.gelu(y + d * uf)
    gate = jax.nn.sigmoid(jnp.dot(ys.astype(BF16), wglu, preferred_element_type=F32) + bglu)
    return ys * gate * (zs * jax.nn.sigmoid(zs))


def _s5_scan_kernel(u_ref, zs_ref, h0_ref, abr_ref, abi_ref, bre_ref, bim_ref, crt_ref, cit_ref,
                    d_ref, wglu_ref, bglu_ref, out_ref, hfin_ref, buf_ref, hst_ref, y_ref,
                    *, nb, tc, pitch):
    i = pl.program_id(0)

    @pl.when(i == 0)
    def _():
        hst_ref[...] = h0_ref[...]

    uf = u_ref[...].reshape(nb * tc, D_SSM)
    ub = uf.astype(BF16)
    n_kt = D_SSM // SSM_KT
    slabs_per_kt = SSM_NT // LANES
    row = lax.broadcasted_iota(jnp.int32, (2 * nb, LANES), 0)

    def drive(kt):
        lhs = ub[:, SSM_KT * kt:SSM_KT * (kt + 1)]
        for ri, w_ref in ((0, bre_ref), (1, bim_ref)):
            res = jnp.dot(lhs, w_ref[kt], preferred_element_type=F32)
            for s in range(slabs_per_kt):
                for b in range(nb):
                    r0 = (ri * nb + b) * pitch
                    buf_ref[kt * slabs_per_kt + s, r0:r0 + tc, :] = (
                        res[b * tc:(b + 1) * tc, LANES * s:LANES * (s + 1)])

    def scan_group(g):
        base = g * SLAB_GROUP
        a1, a2, h = [], [], []
        for s in range(SLAB_GROUP):
            a1.append(jnp.broadcast_to(abr_ref[base + s], (2 * nb, LANES)))
            ai = jnp.broadcast_to(abi_ref[base + s], (2 * nb, LANES))
            a2.append(jnp.where(row < nb, -ai, ai))
            h.append(hst_ref[base + s])
        for t in range(tc):
            for s in range(SLAB_GROUP):
                bu = buf_ref[base + s, pl.ds(t, 2 * nb, stride=pitch), :]
                h[s] = a1[s] * h[s] + a2[s] * pltpu.roll(h[s], nb, axis=0) + bu
                buf_ref[base + s, pl.ds(t, 2 * nb, stride=pitch), :] = h[s]
        for s in range(SLAB_GROUP):
            hst_ref[base + s] = h[s]

    def states(jt, ri):
        rows = []
        for b in range(nb):
            r0 = (ri * nb + b) * pitch
            rows.append(jnp.concatenate(
                [buf_ref[jt * slabs_per_kt + s, r0:r0 + tc, :] for s in range(slabs_per_kt)], axis=1))
        return jnp.concatenate(rows, axis=0).astype(BF16)

    def readout(jt):
        y_ref[:, SSM_KT * jt:SSM_KT * (jt + 1)] = (
            jnp.dot(states(jt, 0), crt_ref[jt], preferred_element_type=F32)
            - jnp.dot(states(jt, 1), cit_ref[jt], preferred_element_type=F32))

    def scan_step(g, carry):
        scan_group(g)
        return carry

    for kt in range(n_kt):
        drive(kt)
    lax.fori_loop(0, N_SLABS // SLAB_GROUP, scan_step, 0)
    for jt in range(n_kt):
        readout(jt)

    zs = zs_ref[...].reshape(nb * tc, D_SSM)
    out = _ssm_epilogue(y_ref[...], uf, zs, d_ref[...], wglu_ref[...], bglu_ref[...])
    out_ref[...] = out.reshape(nb, tc, D_SSM).astype(BF16)

    @pl.when(i == pl.num_programs(0) - 1)
    def _():
        hfin_ref[...] = hst_ref[...]


def _s5_scan(proj3, h0, abr, abi, bre, bim, crt, cit, d, wglu, bglu, tc):
    nb, t_len = proj3.shape[0], proj3.shape[1]
    pitch = tc + SUBLANES
    const = lambda shape: pl.BlockSpec(shape, lambda i: (0,) * len(shape))
    kern = functools.partial(_s5_scan_kernel, nb=nb, tc=tc, pitch=pitch)
    return pl.pallas_call(
        kern,
        grid=(t_len // tc,),
        in_specs=[pl.BlockSpec((nb, tc, D_SSM), lambda i: (0, i, C_U // D_SSM)),
                  pl.BlockSpec((nb, tc, D_SSM), lambda i: (0, i, C_ZS // D_SSM)),
                  const((N_SLABS, 2 * nb, LANES)),
                  const((N_SLABS, 1, LANES)), const((N_SLABS, 1, LANES)),
                  const(bre.shape), const(bim.shape), const(crt.shape), const(cit.shape),
                  const((1, D_SSM)), const((D_SSM, D_SSM)), const((1, D_SSM))],
        out_specs=(pl.BlockSpec((nb, tc, D_SSM), lambda i: (0, i, 0)),
                   const((N_SLABS, 2 * nb, LANES))),
        out_shape=(jax.ShapeDtypeStruct((nb, t_len, D_SSM), BF16),
                   jax.ShapeDtypeStruct((N_SLABS, 2 * nb, LANES), F32)),
        scratch_shapes=[pltpu.VMEM((N_SLABS, 2 * nb * pitch, LANES), F32),
                        pltpu.VMEM((N_SLABS, 2 * nb, LANES), F32),
                        pltpu.VMEM((nb * tc, D_SSM), F32)],
        compiler_params=_params(("arbitrary",), VMEM_MIB_LARGE),
        name="s5_scan")(proj3, proj3, h0, abr.reshape(N_SLABS, 1, LANES), abi.reshape(N_SLABS, 1, LANES),
                        bre, bim, crt, cit, d, wglu, bglu)


def _s5_step_kernel(u_ref, zs_ref, h0r_ref, h0i_ref, abr_ref, abi_ref, bre_ref, bim_ref, crt_ref,
                    cit_ref, d_ref, wglu_ref, bglu_ref, out_ref, hr_ref, hi_ref, y_ref):
    uf = u_ref[...]
    ub = uf.astype(BF16)
    n_kt = D_SSM // SSM_KT
    for kt in range(n_kt):
        lhs = ub[:, SSM_KT * kt:SSM_KT * (kt + 1)]
        sl = slice(SSM_NT * kt, SSM_NT * (kt + 1))
        ar, ai = abr_ref[:, sl], abi_ref[:, sl]
        h0r, h0i = h0r_ref[:, sl], h0i_ref[:, sl]
        hr_ref[:, sl] = ar * h0r - ai * h0i + jnp.dot(lhs, bre_ref[kt], preferred_element_type=F32)
        hi_ref[:, sl] = ar * h0i + ai * h0r + jnp.dot(lhs, bim_ref[kt], preferred_element_type=F32)
    for jt in range(n_kt):
        sl = slice(SSM_NT * jt, SSM_NT * (jt + 1))
        y_ref[:, SSM_KT * jt:SSM_KT * (jt + 1)] = (
            jnp.dot(hr_ref[:, sl].astype(BF16), crt_ref[jt], preferred_element_type=F32)
            - jnp.dot(hi_ref[:, sl].astype(BF16), cit_ref[jt], preferred_element_type=F32))
    out = _ssm_epilogue(y_ref[...], uf, zs_ref[...], d_ref[...], wglu_ref[...], bglu_ref[...])
    out_ref[...] = out.astype(BF16)


def _s5_step(proj, h0r, h0i, abr, abi, bre, bim, crt, cit, d, wglu, bglu):
    m = h0r.shape[0]
    const = lambda shape: pl.BlockSpec(shape, lambda i: (0,) * len(shape))
    return pl.pallas_call(
        _s5_step_kernel,
        grid=(1,),
        in_specs=[pl.BlockSpec((m, D_SSM), lambda i: (0, C_U // D_SSM)),
                  pl.BlockSpec((m, D_SSM), lambda i: (0, C_ZS // D_SSM)),
                  const((m, D_STATE)), const((m, D_STATE)),
                  const((1, D_STATE)), const((1, D_STATE)),
                  const(bre.shape), const(bim.shape), const(crt.shape), const(cit.shape),
                  const((1, D_SSM)), const((D_SSM, D_SSM)), const((1, D_SSM))],
        out_specs=(const((m, D_SSM)), const((m, D_STATE)), const((m, D_STATE))),
        out_shape=(jax.ShapeDtypeStruct((m, D_SSM), BF16),
                   jax.ShapeDtypeStruct((m, D_STATE), F32),
                   jax.ShapeDtypeStruct((m, D_STATE), F32)),
        scratch_shapes=[pltpu.VMEM((m, D_SSM), F32)],
        compiler_params=_params(("arbitrary",), VMEM_MIB),
        name="s5_step")(proj, proj, h0r, h0i, abr, abi, bre, bim, crt, cit, d, wglu, bglu)


def _attn_kernel(q_ref, k_ref, vt_ref, km_ref, vmt_ref, za_ref, o_ref, *scratch, tq):
    m_ref, l_ref, acc_ref = (scratch[:N_HEADS], scratch[N_HEADS:2 * N_HEADS], scratch[2 * N_HEADS:])
    qi = pl.program_id(1)
    nt = (((1,), (1,)), ((), ()))
    heads = [(slice(HEAD_Q * h, HEAD_Q * (h + 1)), slice(V_HEAD * h, V_HEAD * (h + 1)))
             for h in range(N_HEADS)]

    s_meta = [lax.dot_general(km_ref[:, qs], q_ref[:, qs], nt, preferred_element_type=F32)
              for qs, _ in heads]
    p_meta = []
    for h, s in enumerate(s_meta):
        m = jnp.max(s, axis=0, keepdims=True)
        p = jnp.exp2(s - m)
        m_ref[h][...] = m
        l_ref[h][...] = jnp.sum(p, axis=0, keepdims=True)
        p_meta.append(p.astype(BF16))
    for h, (_, vs) in enumerate(heads):
        acc_ref[h][...] = jnp.dot(vmt_ref[vs, :], p_meta[h], preferred_element_type=F32)

    def kv_block(j, masked):
        off = pl.multiple_of(j * tq, tq)
        if masked:
            key = lax.broadcasted_iota(jnp.int32, (tq, tq), 0)
            qry = lax.broadcasted_iota(jnp.int32, (tq, tq), 1)
            keep = key <= qry

        def scores(h):
            qs = heads[h][0]
            return lax.dot_general(k_ref[pl.ds(off, tq), qs], q_ref[:, qs], nt,
                                   preferred_element_type=F32)

        def accumulate(h, alpha, pb):
            vs = heads[h][1]
            acc_ref[h][...] = alpha * acc_ref[h][...] + jnp.dot(
                vt_ref[j, vs, :], pb, preferred_element_type=F32)

        s_next = scores(0)
        pending = None
        for h in range(N_HEADS):
            s = s_next
            if h + 1 < N_HEADS:
                s_next = scores(h + 1)
            if masked:
                s = jnp.where(keep, s, -jnp.inf)
            m_prev = m_ref[h][...]
            m_new = jnp.maximum(m_prev, jnp.max(s, axis=0, keepdims=True))
            alpha = jnp.exp2(m_prev - m_new)
            p = jnp.exp2(s - m_new)
            l_ref[h][...] = alpha * l_ref[h][...] + jnp.sum(p, axis=0, keepdims=True)
            m_ref[h][...] = m_new
            if pending is not None:
                accumulate(*pending)
            pending = (h, alpha, p.astype(BF16))
        accumulate(*pending)

    def full_block(j, carry):
        kv_block(j, False)
        return carry

    lax.fori_loop(0, qi, full_block, 0)
    kv_block(qi, True)
    za = za_ref[...]
    gate = za * jax.nn.sigmoid(za)
    for h, (qs, vs) in enumerate(heads):
        o = (acc_ref[h][...] / l_ref[h][...]).T
        o_ref[:, vs] = (o * gate[:, vs]).astype(BF16)


def _attention(q, kcat, vt, kmeta, vmeta_t, proj, bsz, seq, tq):
    nq = seq // tq
    kern = functools.partial(_attn_kernel, tq=tq)
    return pl.pallas_call(
        kern,
        grid=(bsz, nq),
        in_specs=[pl.BlockSpec((tq, N_HEADS * HEAD_Q), lambda b, i: (b * nq + i, 0)),
                  pl.BlockSpec((None, seq, N_HEADS * HEAD_Q), lambda b, i: (b, 0, 0)),
                  pl.BlockSpec((nq, D_ATT, tq), lambda b, i: (b, 0, 0)),
                  pl.BlockSpec((N_META, N_HEADS * HEAD_Q), lambda b, i: (0, 0)),
                  pl.BlockSpec((D_ATT, N_META), lambda b, i: (0, 0)),
                  pl.BlockSpec((tq, D_ATT), lambda b, i: (b * nq + i, C_ZA // D_ATT))],
        out_specs=pl.BlockSpec((tq, D_ATT), lambda b, i: (b * nq + i, 0)),
        out_shape=jax.ShapeDtypeStruct((bsz * seq, D_ATT), BF16),
        scratch_shapes=([pltpu.VMEM((1, tq), F32)] * (2 * N_HEADS)
                        + [pltpu.VMEM((V_HEAD, tq), F32)] * N_HEADS),
        compiler_params=_params(("parallel", "arbitrary"), VMEM_MIB),
        name="attn_prompt")(q, kcat.reshape(bsz, seq, N_HEADS * HEAD_Q), vt, kmeta, vmeta_t, proj)


def _qabsorb_kernel(q_ref, wukt_ref, qa_ref):
    for h in range(N_HEADS):
        qn = q_ref[:, HEAD_Q * h:HEAD_Q * h + QK_NOPE]
        qa_ref[:, KV_LORA * h:KV_LORA * (h + 1)] = jnp.dot(
            qn, wukt_ref[h], preferred_element_type=F32).astype(BF16)


def _qabsorb(q, wukt):
    m = q.shape[0]
    return pl.pallas_call(
        _qabsorb_kernel,
        out_shape=jax.ShapeDtypeStruct((m, N_HEADS * KV_LORA), BF16),
        name="q_absorb")(q, wukt)


def _decode_kernel(pt_ref, qa_ref, qp_ref, cn_ref, kn_ref, ckv_hbm, kpe_hbm, o_ref,
                   cbuf, kbuf, sem, m_ref, l_ref, acc_ref, *, npg, n_pages):
    b, j = pl.program_id(0), pl.program_id(1)
    nj = pl.num_programs(1)
    t = b * nj + j
    last = pl.num_programs(0) * nj - 1
    ahead = DECODE_SLOTS - 1
    slot = t % DECODE_SLOTS

    def page_copies(step, buf, k):
        pg = pt_ref[(step // nj) * n_pages + (step % nj) * npg + k]
        return (pltpu.make_async_copy(ckv_hbm.at[0, pg], cbuf.at[buf, k], sem.at[buf]),
                pltpu.make_async_copy(kpe_hbm.at[0, pg], kbuf.at[buf, k], sem.at[buf]))

    def start_step(step, buf):
        for k in range(npg):
            for cp in page_copies(step, buf, k):
                cp.start()

    def wait_step(step, buf):
        for k in range(npg):
            for cp in page_copies(step, buf, k):
                cp.wait()

    @pl.when(t == 0)
    def _():
        for d in range(ahead):
            start_step(jnp.minimum(d, last), d)

    t_next = jnp.minimum(t + ahead, last)
    start_step(t_next, (t + ahead) % DECODE_SLOTS)
    wait_step(t, slot)
    c_refs = [cbuf.at[slot, k] for k in range(npg)]
    k_refs = [kbuf.at[slot, k] for k in range(npg)]

    @pl.when(j == 0)
    def _():
        m_ref[...] = jnp.full(m_ref.shape, -jnp.inf, F32)
        l_ref[...] = jnp.zeros(l_ref.shape, F32)
        acc_ref[...] = jnp.zeros(acc_ref.shape, F32)

    qa, qp = qa_ref[...], qp_ref[...]
    nt = (((1,), (1,)), ((), ()))
    page = c_refs[0].shape[0]
    cb = [r[...].astype(BF16) for r in c_refs]
    s = jnp.concatenate(
        [lax.dot_general(qa, cb[k], nt, preferred_element_type=F32)
         + jnp.dot(qp, k_refs[k][...].astype(BF16), preferred_element_type=F32)
         for k in range(npg)], axis=1)
    m_prev = m_ref[...]
    m_new = jnp.maximum(m_prev, jnp.max(s, axis=-1, keepdims=True))
    alpha = jnp.exp2(m_prev - m_new)
    p32 = jnp.exp2(s - m_new)
    p = p32.astype(BF16)
    pv = jnp.dot(p[:, :page], cb[0], preferred_element_type=F32)
    for k in range(1, npg):
        pv = pv + jnp.dot(p[:, page * k:page * (k + 1)], cb[k], preferred_element_type=F32)
    l_ref[...] = alpha * l_ref[...] + jnp.sum(p32, axis=-1, keepdims=True)
    acc_ref[...] = alpha * acc_ref[...] + pv
    m_ref[...] = m_new

    @pl.when(j == pl.num_programs(1) - 1)
    def _():
        cn, kn = cn_ref[...], kn_ref[...]
        s_new = (jnp.sum(qa.astype(F32) * cn, axis=-1, keepdims=True)
                 + jnp.sum(qp.astype(F32) * kn, axis=-1, keepdims=True))
        m_old = m_ref[...]
        m_fin = jnp.maximum(m_old, s_new)
        a = jnp.exp2(m_old - m_fin)
        pn = jnp.exp2(s_new - m_fin)
        l_fin = a * l_ref[...] + pn
        o_ref[...] = (a * acc_ref[...] + pn * cn) / l_fin

    @pl.when(t == last)
    def _():
        for d in range(1, DECODE_SLOTS):
            wait_step(last, (t + d) % DECODE_SLOTS)


def _decode(page_table, qa, qp, c_new, k_new, cache_ckv, cache_kpe_t, npg):
    dbs, n_pages = page_table.shape
    page = cache_ckv.shape[2]
    per_b = lambda shape: pl.BlockSpec((None,) + shape, lambda b, j, pt: (b, 0, 0))
    grid_spec = pltpu.PrefetchScalarGridSpec(
        num_scalar_prefetch=1,
        grid=(dbs, n_pages // npg),
        in_specs=[per_b((N_HEADS, KV_LORA)), per_b((N_HEADS, QK_ROPE)),
                  per_b((1, KV_LORA)), per_b((1, QK_ROPE)),
                  pl.BlockSpec(memory_space=pl.ANY), pl.BlockSpec(memory_space=pl.ANY)],
        out_specs=per_b((N_HEADS, KV_LORA)),
        scratch_shapes=[pltpu.VMEM((DECODE_SLOTS, npg, page, KV_LORA), F32),
                        pltpu.VMEM((DECODE_SLOTS, npg, QK_ROPE, page), F32),
                        pltpu.SemaphoreType.DMA((DECODE_SLOTS,)),
                        pltpu.VMEM((N_HEADS, 1), F32), pltpu.VMEM((N_HEADS, 1), F32),
                        pltpu.VMEM((N_HEADS, KV_LORA), F32)])
    return pl.pallas_call(
        functools.partial(_decode_kernel, npg=npg, n_pages=n_pages),
        grid_spec=grid_spec,
        out_shape=jax.ShapeDtypeStruct((dbs, N_HEADS, KV_LORA), F32),
        compiler_params=_params(("arbitrary", "arbitrary"), VMEM_MIB),
        name="attn_decode")(page_table.reshape(-1), qa, qp, c_new, k_new, cache_ckv, cache_kpe_t)


def _ouv_kernel(ol_ref, wuv_ref, za_ref, o_ref):
    za = za_ref[...]
    gate = za * jax.nn.sigmoid(za)
    for h in range(N_HEADS):
        o = jnp.dot(ol_ref[:, KV_LORA * h:KV_LORA * (h + 1)].astype(BF16),
                    wuv_ref[:, V_HEAD * h:V_HEAD * (h + 1)], preferred_element_type=F32)
        o_ref[:, V_HEAD * h:V_HEAD * (h + 1)] = (o * gate[:, V_HEAD * h:V_HEAD * (h + 1)]).astype(BF16)


def _ouv(o_lat, wuv, proj):
    m = o_lat.shape[0]
    const = lambda shape: pl.BlockSpec(shape, lambda i: (0,) * len(shape))
    return pl.pallas_call(
        _ouv_kernel,
        grid=(1,),
        in_specs=[const((m, N_HEADS * KV_LORA)), const((KV_LORA, D_ATT)),
                  pl.BlockSpec((m, D_ATT), lambda i: (0, C_ZA // D_ATT))],
        out_specs=const((m, D_ATT)),
        out_shape=jax.ShapeDtypeStruct((m, D_ATT), BF16),
        name="o_uv")(o_lat, wuv, proj)


def _outproj_kernel(ssm_ref, att_ref, wtop_ref, wbot_ref, x_ref, g_ref, y_ref):
    h = (x_ref[...]
         + jnp.dot(ssm_ref[...], wtop_ref[...], preferred_element_type=F32)
         + jnp.dot(att_ref[...], wbot_ref[...], preferred_element_type=F32))
    y_ref[...] = _rms(h, g_ref[...])


def _outproj(ssm, att, w, x, g, tm):
    m = x.shape[0]
    return pl.pallas_call(
        _outproj_kernel,
        grid=(m // tm,),
        in_specs=[pl.BlockSpec((tm, D_SSM), lambda i: (i, 0)),
                  pl.BlockSpec((tm, D_ATT), lambda i: (i, 0)),
                  pl.BlockSpec((D_SSM, D_MODEL), lambda i: (0, 0)),
                  pl.BlockSpec((D_ATT, D_MODEL), lambda i: (1, 0)),
                  pl.BlockSpec((tm, D_MODEL), lambda i: (i, 0)),
                  pl.BlockSpec((1, D_MODEL), lambda i: (0, 0))],
        out_specs=pl.BlockSpec((tm, D_MODEL), lambda i: (i, 0)),
        out_shape=jax.ShapeDtypeStruct((m, D_MODEL), F32),
        compiler_params=_params(("parallel",), VMEM_MIB),
        name="outproj")(ssm, att, w, w, x, g)


def _slab_minor(h):
    return h.transpose(1, 0, 2).reshape(h.shape[1], D_STATE)


def kernel(x_prompt, x_sample, cache_ckv, cache_kpe, state_ssm_re, state_ssm_im, page_table, meta_tokens, g_norm, w_in, ssm_a_re, ssm_a_im, ssm_log_dt, ssm_b_re, ssm_b_im, ssm_c_re, ssm_c_im, ssm_d, w_glu, b_glu, g_q_norm, w_uq, g_kv_norm, w_uk, w_uv, w_out, g_final):
    bsz, seq, _ = x_prompt.shape
    dbs = x_sample.shape[0]
    n_pages, page = page_table.shape[1], cache_ckv.shape[2]
    past_len = n_pages * page
    layer = 0

    w_in_p = _win_prep(jnp.swapaxes(w_in[layer], 0, 1), 256)
    w_uq_p = jnp.pad(w_uq[layer].reshape(Q_LORA, N_HEADS, QK_NOPE + QK_ROPE),
                     ((0, 0), (0, 0), (0, HEAD_Q - QK_NOPE - QK_ROPE))
                     ).reshape(Q_LORA, N_HEADS * HEAD_Q).astype(BF16)
    w_uk_flat = w_uk[layer].reshape(KV_LORA, D_ATT).astype(BF16)
    w_uk_t = w_uk[layer].transpose(1, 2, 0).astype(BF16)
    w_uv_flat = w_uv[layer].reshape(KV_LORA, D_ATT).astype(BF16)
    w_glu_b = w_glu[layer].astype(BF16)
    w_out_b = w_out[layer].astype(BF16)
    g_in = g_norm[layer].reshape(1, D_MODEL)
    g_q = g_q_norm[layer].reshape(1, Q_LORA)
    g_kv = g_kv_norm[layer].reshape(1, KV_LORA)
    g_fin = g_final.reshape(1, D_MODEL)
    d_skip = ssm_d[layer].reshape(1, D_SSM)
    bglu = b_glu[layer].reshape(1, D_SSM)

    abr, abi, bre, bim, crt, cit = _ssm_prep(ssm_a_re[layer], ssm_a_im[layer], ssm_log_dt[layer],
                                             ssm_b_re[layer], ssm_b_im[layer], ssm_c_re[layer], ssm_c_im[layer])
    abr_f, abi_f = abr.reshape(1, D_STATE), abi.reshape(1, D_STATE)
    ssm_w = (bre, bim, crt, cit, d_skip, w_glu_b, bglu)

    pos = jnp.concatenate([jnp.arange(N_META, N_META + seq), jnp.full((dbs,), past_len), jnp.arange(N_META)])
    tabs = _rope_tables(pos)
    tabs_real = tuple(t[:seq] for t in tabs)
    tabs_small = tuple(t[seq:] for t in tabs)

    x_real = x_prompt.reshape(bsz * seq, D_MODEL)
    x_small = jnp.concatenate([x_sample.reshape(dbs, D_MODEL), meta_tokens.astype(x_prompt.dtype)], axis=0)
    n_small = dbs + N_META
    proj_real = _inproj(x_real, g_in, w_in_p, TM_INPROJ, TN_INPROJ)
    proj_small = _inproj(x_small, g_in, w_in_p, n_small, TN_INPROJ)

    w_uv_t = w_uv_flat.T
    c_real, kr_real, kcat_real, vt_real = _kvprep(proj_real, g_kv, tabs_real, w_uk_flat, w_uv_t, TM_PREP, ATT_TILE)
    c_small, kr_small, kcat_small, vt_small = _kvprep(proj_small, g_kv, tabs_small, w_uk_flat, w_uv_t,
                                                      n_small, n_small)
    q_real = _qproj(proj_real, g_q, w_uq_p, tabs_real, TM_PREP)
    q_small = _qproj(proj_small, g_q, w_uq_p, tabs_small, n_small)

    proj_meta = jnp.broadcast_to(proj_small[dbs:][None], (bsz, N_META, N_PROJ))
    h_zero = jnp.zeros((N_SLABS, 2 * bsz, LANES), F32)
    _, h_meta = _s5_scan(proj_meta, h_zero, abr_f, abi_f, *ssm_w, tc=N_META)
    ssm_real, h_fin = _s5_scan(proj_real.reshape(bsz, seq, N_PROJ), h_meta, abr_f, abi_f, *ssm_w, tc=TC_SCAN)
    h_fin = _slab_minor(h_fin)
    ssm_smp, hr_s, hi_s = _s5_step(proj_small, state_ssm_re[layer].reshape(dbs, D_STATE),
                                   state_ssm_im[layer].reshape(dbs, D_STATE), abr_f, abi_f, *ssm_w)

    att_real = _attention(q_real, kcat_real, vt_real, kcat_small[dbs:], vt_small[0, :, dbs:], proj_real,
                          bsz, seq, ATT_TILE)

    qs = q_small[:dbs]
    qa = _qabsorb(qs, w_uk_t).reshape(dbs, N_HEADS, KV_LORA)
    qp = qs.reshape(dbs, N_HEADS, HEAD_Q)[:, :, QK_NOPE:QK_NOPE + QK_ROPE]
    c_new = c_small[:dbs].reshape(dbs, 1, KV_LORA)
    k_new = kr_small[:dbs, :QK_ROPE].reshape(dbs, 1, QK_ROPE)
    o_lat = _decode(page_table, qa, qp, c_new, k_new, cache_ckv, jnp.swapaxes(cache_kpe, 2, 3), DECODE_PAGES)
    att_smp = _ouv(o_lat.reshape(dbs, N_HEADS * KV_LORA), w_uv_flat, proj_small)

    y_prompt = _outproj(ssm_real.reshape(bsz * seq, D_SSM), att_real, w_out_b, x_real, g_fin, TM_OUTPROJ)
    y_sample = _outproj(ssm_smp, att_smp, w_out_b, x_sample.reshape(dbs, D_MODEL), g_fin, dbs)

    c_meta, k_meta = c_small[dbs:], kr_small[dbs:, :QK_ROPE]
    ckv_p = jnp.concatenate([jnp.broadcast_to(c_meta[None], (bsz, N_META, KV_LORA)),
                             c_real.reshape(bsz, seq, KV_LORA)], axis=1)
    kpe_p = jnp.concatenate([jnp.broadcast_to(k_meta[None], (bsz, N_META, QK_ROPE)),
                             kr_real[:, :QK_ROPE].reshape(bsz, seq, QK_ROPE)], axis=1)
    return (y_prompt.reshape(bsz, seq, D_MODEL),
            y_sample.reshape(dbs, 1, D_MODEL),
            ckv_p[None], kpe_p[None],
            h_fin[:bsz].reshape(1, bsz, N_GROUPS, N_STATE),
            h_fin[bsz:].reshape(1, bsz, N_GROUPS, N_STATE),
            c_small[:dbs].reshape(1, dbs, 1, KV_LORA),
            kr_small[:dbs, :QK_ROPE].reshape(1, dbs, 1, QK_ROPE),
            hr_s.reshape(1, dbs, N_GROUPS, N_STATE),
            hi_s.reshape(1, dbs, N_GROUPS, N_STATE))
```

```python
import functools
import math

import numpy as np
import jax
import jax.numpy as jnp
from jax import lax
from jax.experimental import pallas as pl
from jax.experimental.pallas import tpu as pltpu

F32 = jnp.float32
BF16 = jnp.bfloat16

D_MODEL = 2048
N_META = 16
D_SSM = 1024
SSM_GROUP = 16
N_GROUPS = 64
N_STATE = 64
D_STATE = N_GROUPS * N_STATE
D_ATT = 1024
V_HEAD = 128
N_HEADS = 8
QK_NOPE = 128
QK_ROPE = 64
KV_LORA = 512
Q_LORA = 768
ROPE_THETA = 10000.0
SOFTMAX_SCALE = (QK_NOPE + QK_ROPE) ** -0.5
Q_SCALE = SOFTMAX_SCALE * math.log2(math.e)
EPS = 1e-6

LANES = 128
SUBLANES = 8
MIB = 1024 * 1024

C_CQ = 0
C_KPE = 768
C_U = 1024
C_ZS = 2048
C_ZA = 3072
C_CKV = 4096
N_PROJ = 4608
HEAD_Q = 256
N_SLABS = D_STATE // LANES
SLAB_GROUP = 8

TM_INPROJ, TN_INPROJ = 1024, 1536
TM_PREP = 1024
TM_OUTPROJ = 512
ATT_TILE = 512
TC_SCAN = 128
DECODE_PAGES = 32
DECODE_SLOTS = 3
VMEM_MIB = 48
VMEM_MIB_LARGE = 56
SSM_KT = 256
SSM_NT = SSM_KT * N_STATE // SSM_GROUP


def _params(sem, vmem_mib=None):
    return pltpu.CompilerParams(
        dimension_semantics=sem,
        vmem_limit_bytes=None if vmem_mib is None else vmem_mib * MIB)


def _rms(x, g):
    return x * lax.rsqrt(jnp.mean(x * x, axis=-1, keepdims=True) + EPS) * g


def _rope128(x, cos_t, sin_lo, sin_hi):
    return (x * cos_t + pltpu.roll(x, LANES - QK_ROPE // 2, axis=1) * sin_lo
            + pltpu.roll(x, QK_ROPE // 2, axis=1) * sin_hi)


def _rope_table_kernel(pos_ref, inv_ref, cos_ref, slo_ref, shi_ref):
    ang = pos_ref[...] * inv_ref[...]
    c, s = jnp.cos(ang), jnp.sin(ang)
    lane = lax.broadcasted_iota(jnp.int32, ang.shape, 1)
    half = QK_ROPE // 2
    cos_ref[...] = jnp.where(lane < QK_ROPE, c, 0.0)
    slo_ref[...] = jnp.where(lane < half, -s, 0.0)
    shi_ref[...] = jnp.where((lane >= half) & (lane < QK_ROPE), s, 0.0)


def _rope_tables(pos):
    n = pos.shape[0]
    half = QK_ROPE // 2
    inv = ROPE_THETA ** (-(np.arange(LANES) % half).astype(np.float64) / half)
    inv = jnp.asarray(inv.astype(np.float32)).reshape(1, LANES)
    shp = jax.ShapeDtypeStruct((n, LANES), F32)
    return pl.pallas_call(_rope_table_kernel, out_shape=(shp, shp, shp), name="rope_tables")(
        pos.reshape(n, 1).astype(F32), inv)


def _win_prep_kernel(wt_ref, o_ref):
    o_cq, o_ckv, o_kpe = 2 * D_SSM, 2 * D_SSM + Q_LORA, 2 * D_SSM + Q_LORA + KV_LORA
    o_za = o_kpe + QK_ROPE
    cols = o_ref.shape[1]
    o_ref[C_CQ:C_CQ + Q_LORA, :] = wt_ref[o_cq:o_ckv, :].astype(BF16)
    o_ref[C_KPE:C_KPE + QK_ROPE, :] = wt_ref[o_kpe:o_za, :].astype(BF16)
    o_ref[C_KPE + QK_ROPE:C_U, :] = jnp.zeros((C_U - C_KPE - QK_ROPE, cols), BF16)
    o_ref[C_U:C_ZA, :] = wt_ref[:o_cq, :].astype(BF16)
    o_ref[C_ZA:C_CKV, :] = wt_ref[o_za:, :].astype(BF16)
    o_ref[C_CKV:, :] = wt_ref[o_ckv:o_kpe, :].astype(BF16)


def _win_prep(w_t, tc):
    n, k = w_t.shape
    return pl.pallas_call(
        _win_prep_kernel,
        grid=(k // tc,),
        in_specs=[pl.BlockSpec((n, tc), lambda i: (0, i))],
        out_specs=pl.BlockSpec((N_PROJ, tc), lambda i: (0, i)),
        out_shape=jax.ShapeDtypeStruct((N_PROJ, k), BF16),
        compiler_params=_params(("parallel",)),
        name="win_prep")(w_t)


def _inproj_kernel(x_ref, g_ref, w_ref, o_ref, xn_ref):
    @pl.when(pl.program_id(1) == 0)
    def _():
        xn_ref[...] = _rms(x_ref[...], g_ref[...]).astype(BF16)

    o_ref[...] = lax.dot_general(xn_ref[...], w_ref[...], (((1,), (1,)), ((), ())),
                                 preferred_element_type=F32)


def _inproj(x, g, w_t, tm, tn):
    m = x.shape[0]
    return pl.pallas_call(
        _inproj_kernel,
        grid=(m // tm, N_PROJ // tn),
        in_specs=[pl.BlockSpec((tm, D_MODEL), lambda i, j: (i, 0)),
                  pl.BlockSpec((1, D_MODEL), lambda i, j: (0, 0)),
                  pl.BlockSpec((tn, D_MODEL), lambda i, j: (j, 0))],
        out_specs=pl.BlockSpec((tm, tn), lambda i, j: (i, j)),
        out_shape=jax.ShapeDtypeStruct((m, N_PROJ), F32),
        scratch_shapes=[pltpu.VMEM((tm, D_MODEL), BF16)],
        compiler_params=_params(("parallel", "arbitrary"), VMEM_MIB_LARGE),
        name="inproj")(x, g, w_t)


def _kvprep_kernel(ckv_ref, kpe_ref, g_ref, cos_ref, slo_ref, shi_ref, wuk_ref, wuvt_ref,
                   c_ref, kr_ref, kcat_ref, vt_ref, *, tk):
    c = _rms(ckv_ref[...], g_ref[...])
    c_ref[...] = c
    cb = c.astype(BF16)
    kr = _rope128(kpe_ref[...], cos_ref[...], slo_ref[...], shi_ref[...])
    kr_ref[...] = kr
    krb = kr.astype(BF16)
    knope = jnp.dot(cb, wuk_ref[...], preferred_element_type=F32).astype(BF16)
    for h in range(N_HEADS):
        kcat_ref[:, HEAD_Q * h:HEAD_Q * h + QK_NOPE] = knope[:, QK_NOPE * h:QK_NOPE * (h + 1)]
        kcat_ref[:, HEAD_Q * h + QK_NOPE:HEAD_Q * (h + 1)] = krb
    vt = lax.dot_general(wuvt_ref[...], cb, (((1,), (1,)), ((), ())),
                         preferred_element_type=F32).astype(BF16)
    for kb in range(vt_ref.shape[0]):
        vt_ref[kb] = vt[:, tk * kb:tk * (kb + 1)]


def _kvprep(proj, g, tabs, wuk, wuvt, tm, tk):
    m = proj.shape[0]
    nt = tabs[0].shape[0] // tm
    tab = pl.BlockSpec((tm, LANES), lambda i: (i % nt, 0))
    row = lambda w: pl.BlockSpec((tm, w), lambda i: (i, 0))
    return pl.pallas_call(
        functools.partial(_kvprep_kernel, tk=tk),
        grid=(m // tm,),
        in_specs=[pl.BlockSpec((tm, KV_LORA), lambda i: (i, C_CKV // KV_LORA)),
                  pl.BlockSpec((tm, LANES), lambda i: (i, C_KPE // LANES)),
                  pl.BlockSpec((1, KV_LORA), lambda i: (0, 0)),
                  tab, tab, tab,
                  pl.BlockSpec((KV_LORA, D_ATT), lambda i: (0, 0)),
                  pl.BlockSpec((D_ATT, KV_LORA), lambda i: (0, 0))],
        out_specs=(row(KV_LORA), row(LANES), row(N_HEADS * HEAD_Q),
                   pl.BlockSpec((tm // tk, D_ATT, tk), lambda i: (i, 0, 0))),
        out_shape=(jax.ShapeDtypeStruct((m, KV_LORA), F32),
                   jax.ShapeDtypeStruct((m, LANES), F32),
                   jax.ShapeDtypeStruct((m, N_HEADS * HEAD_Q), BF16),
                   jax.ShapeDtypeStruct((m // tk, D_ATT, tk), BF16)),
        compiler_params=_params(("parallel",), VMEM_MIB),
        name="kvprep")(proj, proj, g, *tabs, wuk, wuvt)


def _qproj_kernel(cq_ref, g_ref, w_ref, cos_ref, slo_ref, shi_ref, q_ref):
    cqn = _rms(cq_ref[...], g_ref[...]).astype(BF16)
    q = jnp.dot(cqn, w_ref[...], preferred_element_type=F32)
    cos_t, slo, shi = cos_ref[...], slo_ref[...], shi_ref[...]
    for h in range(N_HEADS):
        lo = HEAD_Q * h
        q_ref[:, lo:lo + QK_NOPE] = (q[:, lo:lo + QK_NOPE] * Q_SCALE).astype(BF16)
        pe = _rope128(q[:, lo + QK_NOPE:lo + HEAD_Q], cos_t, slo, shi)
        q_ref[:, lo + QK_NOPE:lo + HEAD_Q] = (pe * Q_SCALE).astype(BF16)


def _qproj(proj, g, w, tabs, tm):
    m = proj.shape[0]
    nt = tabs[0].shape[0] // tm
    tab = pl.BlockSpec((tm, LANES), lambda i: (i % nt, 0))
    return pl.pallas_call(
        _qproj_kernel,
        grid=(m // tm,),
        in_specs=[pl.BlockSpec((tm, Q_LORA), lambda i: (i, C_CQ // Q_LORA)),
                  pl.BlockSpec((1, Q_LORA), lambda i: (0, 0)),
                  pl.BlockSpec((Q_LORA, N_HEADS * HEAD_Q), lambda i: (0, 0)),
                  tab, tab, tab],
        out_specs=pl.BlockSpec((tm, N_HEADS * HEAD_Q), lambda i: (i, 0)),
        out_shape=jax.ShapeDtypeStruct((m, N_HEADS * HEAD_Q), BF16),
        compiler_params=_params(("parallel",), VMEM_MIB),
        name="qproj")(proj, g, w, *tabs)


def _ssm_prep_kernel(are_ref, aim_ref, ldt_ref, bre_ref, bim_ref, cre_ref, cim_ref,
                     abr_ref, abi_ref, wbr_ref, wbi_ref, wcr_ref, wci_ref):
    dt = jnp.exp(ldt_ref[...])
    ar, ai = are_ref[...], aim_ref[...]
    mag = jnp.exp(dt * ar)
    abr, abi = mag * jnp.cos(dt * ai), mag * jnp.sin(dt * ai)
    den = ar * ar + ai * ai
    nr, ni = abr - 1.0, abi
    f_re = (nr * ar + ni * ai) / den
    f_im = (ni * ar - nr * ai) / den
    br, bi = bre_ref[...], bim_ref[...]
    abr_ref[...] = abr
    abi_ref[...] = abi
    bbr = f_re * br - f_im * bi
    bbi = f_re * bi + f_im * br

    gpt = SSM_KT // SSM_GROUP
    n_kt = D_SSM // SSM_KT

    def spread(x, width, period):
        sel = (lax.broadcasted_iota(jnp.int32, (period, width), 1) % period
               == lax.broadcasted_iota(jnp.int32, (period, width), 0))
        return jnp.dot(x.astype(BF16), jnp.where(sel, 1.0, 0.0).astype(BF16), preferred_element_type=F32)

    r_in = lax.broadcasted_iota(jnp.int32, (SSM_KT, SSM_NT), 0) // SSM_GROUP
    c_in = lax.broadcasted_iota(jnp.int32, (SSM_KT, SSM_NT), 1) // N_STATE
    r_out = lax.broadcasted_iota(jnp.int32, (SSM_NT, SSM_KT), 0) // N_STATE
    c_out = lax.broadcasted_iota(jnp.int32, (SSM_NT, SSM_KT), 1) // SSM_GROUP
    for kt in range(n_kt):
        gs = slice(gpt * kt, gpt * (kt + 1))
        for src, dst in ((bbr, wbr_ref), (bbi, wbi_ref)):
            x = src[gs].reshape(SSM_KT, N_STATE)
            dst[kt] = jnp.where(r_in == c_in, spread(x, SSM_NT, N_STATE), 0.0).astype(BF16)
        for src, dst in ((cre_ref, wcr_ref), (cim_ref, wci_ref)):
            x = src[gs].reshape(SSM_NT, SSM_GROUP)
            dst[kt] = jnp.where(r_out == c_out, spread(x, SSM_KT, SSM_GROUP), 0.0).astype(BF16)


def _ssm_prep(a_re, a_im, log_dt, b_re, b_im, c_re, c_im):
    gn = jax.ShapeDtypeStruct((N_GROUPS, 1, N_STATE), F32)
    w_in_shape = jax.ShapeDtypeStruct((D_SSM // SSM_KT, SSM_KT, SSM_NT), BF16)
    w_out_shape = jax.ShapeDtypeStruct((D_SSM // SSM_KT, SSM_NT, SSM_KT), BF16)
    return pl.pallas_call(
        _ssm_prep_kernel, out_shape=(gn, gn, w_in_shape, w_in_shape, w_out_shape, w_out_shape),
        name="ssm_prep")(
            a_re.reshape(N_GROUPS, 1, N_STATE), a_im.reshape(N_GROUPS, 1, N_STATE),
            log_dt.reshape(N_GROUPS, 1, 1), b_re.transpose(0, 2, 1), b_im.transpose(0, 2, 1),
            c_re.transpose(0, 2, 1), c_im.transpose(0, 2, 1))


def _ssm_epilogue(y, uf, zs, d, wglu, bglu):
    ys = jax.nn.gelu(y + d * uf)
    gate = jax.nn.sigmoid(jnp.dot(ys.astype(BF16), wglu, preferred_element_type=F32) + bglu)
    return ys * gate * (zs * jax.nn.sigmoid(zs))


def _s5_scan_kernel(u_ref, zs_ref, h0_ref, abr_ref, abi_ref, bre_ref, bim_ref, crt_ref, cit_ref,
                    d_ref, wglu_ref, bglu_ref, out_ref, hfin_ref, buf_ref, hst_ref, y_ref,
                    *, nb, tc, pitch):
    i = pl.program_id(0)

    @pl.when(i == 0)
    def _():
        hst_ref[...] = h0_ref[...]

    uf = u_ref[...].reshape(nb * tc, D_SSM)
    ub = uf.astype(BF16)
    n_kt = D_SSM // SSM_KT
    slabs_per_kt = SSM_NT // LANES
    row = lax.broadcasted_iota(jnp.int32, (2 * nb, LANES), 0)

    def drive(kt):
        lhs = ub[:, SSM_KT * kt:SSM_KT * (kt + 1)]
        for ri, w_ref in ((0, bre_ref), (1, bim_ref)):
            res = jnp.dot(lhs, w_ref[kt], preferred_element_type=F32)
            for s in range(slabs_per_kt):
                for b in range(nb):
                    r0 = (ri * nb + b) * pitch
                    buf_ref[kt * slabs_per_kt + s, r0:r0 + tc, :] = (
                        res[b * tc:(b + 1) * tc, LANES * s:LANES * (s + 1)])

    def scan_group(g):
        base = g * SLAB_GROUP
        a1, a2, h = [], [], []
        for s in range(SLAB_GROUP):
            a1.append(jnp.broadcast_to(abr_ref[base + s], (2 * nb, LANES)))
            ai = jnp.broadcast_to(abi_ref[base + s], (2 * nb, LANES))
            a2.append(jnp.where(row < nb, -ai, ai))
            h.append(hst_ref[base + s])
        for t in range(tc):
            for s in range(SLAB_GROUP):
                bu = buf_ref[base + s, pl.ds(t, 2 * nb, stride=pitch), :]
                h[s] = a1[s] * h[s] + a2[s] * pltpu.roll(h[s], nb, axis=0) + bu
                buf_ref[base + s, pl.ds(t, 2 * nb, stride=pitch), :] = h[s]
        for s in range(SLAB_GROUP):
            hst_ref[base + s] = h[s]

    def states(jt, ri):
        rows = []
        for b in range(nb):
            r0 = (ri * nb + b) * pitch
            rows.append(jnp.concatenate(
                [buf_ref[jt * slabs_per_kt + s, r0:r0 + tc, :] for s in range(slabs_per_kt)], axis=1))
        return jnp.concatenate(rows, axis=0).astype(BF16)

    def readout(jt):
        y_ref[:, SSM_KT * jt:SSM_KT * (jt + 1)] = (
            jnp.dot(states(jt, 0), crt_ref[jt], preferred_element_type=F32)
            - jnp.dot(states(jt, 1), cit_ref[jt], preferred_element_type=F32))

    def scan_step(g, carry):
        scan_group(g)
        return carry

    for kt in range(n_kt):
        drive(kt)
    lax.fori_loop(0, N_SLABS // SLAB_GROUP, scan_step, 0)
    for jt in range(n_kt):
        readout(jt)

    zs = zs_ref[...].reshape(nb * tc, D_SSM)
    out = _ssm_epilogue(y_ref[...], uf, zs, d_ref[...], wglu_ref[...], bglu_ref[...])
    out_ref[...] = out.reshape(nb, tc, D_SSM).astype(BF16)

    @pl.when(i == pl.num_programs(0) - 1)
    def _():
        hfin_ref[...] = hst_ref[...]


def _s5_scan(proj3, h0, abr, abi, bre, bim, crt, cit, d, wglu, bglu, tc):
    nb, t_len = proj3.shape[0], proj3.shape[1]
    pitch = tc + SUBLANES
    const = lambda shape: pl.BlockSpec(shape, lambda i: (0,) * len(shape))
    kern = functools.partial(_s5_scan_kernel, nb=nb, tc=tc, pitch=pitch)
    return pl.pallas_call(
        kern,
        grid=(t_len // tc,),
        in_specs=[pl.BlockSpec((nb, tc, D_SSM), lambda i: (0, i, C_U // D_SSM)),
                  pl.BlockSpec((nb, tc, D_SSM), lambda i: (0, i, C_ZS // D_SSM)),
                  const((N_SLABS, 2 * nb, LANES)),
                  const((N_SLABS, 1, LANES)), const((N_SLABS, 1, LANES)),
                  const(bre.shape), const(bim.shape), const(crt.shape), const(cit.shape),
                  const((1, D_SSM)), const((D_SSM, D_SSM)), const((1, D_SSM))],
        out_specs=(pl.BlockSpec((nb, tc, D_SSM), lambda i: (0, i, 0)),
                   const((N_SLABS, 2 * nb, LANES))),
        out_shape=(jax.ShapeDtypeStruct((nb, t_len, D_SSM), BF16),
                   jax.ShapeDtypeStruct((N_SLABS, 2 * nb, LANES), F32)),
        scratch_shapes=[pltpu.VMEM((N_SLABS, 2 * nb * pitch, LANES), F32),
                        pltpu.VMEM((N_SLABS, 2 * nb, LANES), F32),
                        pltpu.VMEM((nb * tc, D_SSM), F32)],
        compiler_params=_params(("arbitrary",), VMEM_MIB_LARGE),
        name="s5_scan")(proj3, proj3, h0, abr.reshape(N_SLABS, 1, LANES), abi.reshape(N_SLABS, 1, LANES),
                        bre, bim, crt, cit, d, wglu, bglu)


def _s5_step_kernel(u_ref, zs_ref, h0r_ref, h0i_ref, abr_ref, abi_ref, bre_ref, bim_ref, crt_ref,
                    cit_ref, d_ref, wglu_ref, bglu_ref, out_ref, hr_ref, hi_ref, y_ref):
    uf = u_ref[...]
    ub = uf.astype(BF16)
    n_kt = D_SSM // SSM_KT
    for kt in range(n_kt):
        lhs = ub[:, SSM_KT * kt:SSM_KT * (kt + 1)]
        sl = slice(SSM_NT * kt, SSM_NT * (kt + 1))
        ar, ai = abr_ref[:, sl], abi_ref[:, sl]
        h0r, h0i = h0r_ref[:, sl], h0i_ref[:, sl]
        hr_ref[:, sl] = ar * h0r - ai * h0i + jnp.dot(lhs, bre_ref[kt], preferred_element_type=F32)
        hi_ref[:, sl] = ar * h0i + ai * h0r + jnp.dot(lhs, bim_ref[kt], preferred_element_type=F32)
    for jt in range(n_kt):
        sl = slice(SSM_NT * jt, SSM_NT * (jt + 1))
        y_ref[:, SSM_KT * jt:SSM_KT * (jt + 1)] = (
            jnp.dot(hr_ref[:, sl].astype(BF16), crt_ref[jt], preferred_element_type=F32)
            - jnp.dot(hi_ref[:, sl].astype(BF16), cit_ref[jt], preferred_element_type=F32))
    out = _ssm_epilogue(y_ref[...], uf, zs_ref[...], d_ref[...], wglu_ref[...], bglu_ref[...])
    out_ref[...] = out.astype(BF16)


def _s5_step(proj, h0r, h0i, abr, abi, bre, bim, crt, cit, d, wglu, bglu):
    m = h0r.shape[0]
    const = lambda shape: pl.BlockSpec(shape, lambda i: (0,) * len(shape))
    return pl.pallas_call(
        _s5_step_kernel,
        grid=(1,),
        in_specs=[pl.BlockSpec((m, D_SSM), lambda i: (0, C_U // D_SSM)),
                  pl.BlockSpec((m, D_SSM), lambda i: (0, C_ZS // D_SSM)),
                  const((m, D_STATE)), const((m, D_STATE)),
                  const((1, D_STATE)), const((1, D_STATE)),
                  const(bre.shape), const(bim.shape), const(crt.shape), const(cit.shape),
                  const((1, D_SSM)), const((D_SSM, D_SSM)), const((1, D_SSM))],
        out_specs=(const((m, D_SSM)), const((m, D_STATE)), const((m, D_STATE))),
        out_shape=(jax.ShapeDtypeStruct((m, D_SSM), BF16),
                   jax.ShapeDtypeStruct((m, D_STATE), F32),
                   jax.ShapeDtypeStruct((m, D_STATE), F32)),
        scratch_shapes=[pltpu.VMEM((m, D_SSM), F32)],
        compiler_params=_params(("arbitrary",), VMEM_MIB),
        name="s5_step")(proj, proj, h0r, h0i, abr, abi, bre, bim, crt, cit, d, wglu, bglu)


def _attn_kernel(q_ref, k_ref, vt_ref, km_ref, vmt_ref, za_ref, o_ref, *scratch, tq):
    m_ref, l_ref, acc_ref = (scratch[:N_HEADS], scratch[N_HEADS:2 * N_HEADS], scratch[2 * N_HEADS:])
    qi = pl.program_id(1)
    nt = (((1,), (1,)), ((), ()))
    heads = [(slice(HEAD_Q * h, HEAD_Q * (h + 1)), slice(V_HEAD * h, V_HEAD * (h + 1)))
             for h in range(N_HEADS)]

    s_meta = [lax.dot_general(km_ref[:, qs], q_ref[:, qs], nt, preferred_element_type=F32)
              for qs, _ in heads]
    p_meta = []
    for h, s in enumerate(s_meta):
        m = jnp.max(s, axis=0, keepdims=True)
        p = jnp.exp2(s - m)
        m_ref[h][...] = m
        l_ref[h][...] = jnp.sum(p, axis=0, keepdims=True)
        p_meta.append(p.astype(BF16))
    for h, (_, vs) in enumerate(heads):
        acc_ref[h][...] = jnp.dot(vmt_ref[vs, :], p_meta[h], preferred_element_type=F32)

    def kv_block(j, masked):
        off = pl.multiple_of(j * tq, tq)
        if masked:
            key = lax.broadcasted_iota(jnp.int32, (tq, tq), 0)
            qry = lax.broadcasted_iota(jnp.int32, (tq, tq), 1)
            keep = key <= qry

        def scores(h):
            qs = heads[h][0]
            return lax.dot_general(k_ref[pl.ds(off, tq), qs], q_ref[:, qs], nt,
                                   preferred_element_type=F32)

        def accumulate(h, alpha, pb):
            vs = heads[h][1]
            acc_ref[h][...] = alpha * acc_ref[h][...] + jnp.dot(
                vt_ref[j, vs, :], pb, preferred_element_type=F32)

        s_next = scores(0)
        pending = None
        for h in range(N_HEADS):
            s = s_next
            if h + 1 < N_HEADS:
                s_next = scores(h + 1)
            if masked:
                s = jnp.where(keep, s, -jnp.inf)
            m_prev = m_ref[h][...]
            m_new = jnp.maximum(m_prev, jnp.max(s, axis=0, keepdims=True))
            alpha = jnp.exp2(m_prev - m_new)
            p = jnp.exp2(s - m_new)
            l_ref[h][...] = alpha * l_ref[h][...] + jnp.sum(p, axis=0, keepdims=True)
            m_ref[h][...] = m_new
            if pending is not None:
                accumulate(*pending)
            pending = (h, alpha, p.astype(BF16))
        accumulate(*pending)

    def full_block(j, carry):
        kv_block(j, False)
        return carry

    lax.fori_loop(0, qi, full_block, 0)
    kv_block(qi, True)
    za = za_ref[...]
    gate = za * jax.nn.sigmoid(za)
    for h, (qs, vs) in enumerate(heads):
        o = (acc_ref[h][...] / l_ref[h][...]).T
        o_ref[:, vs] = (o * gate[:, vs]).astype(BF16)


def _attention(q, kcat, vt, kmeta, vmeta_t, proj, bsz, seq, tq):
    nq = seq // tq
    kern = functools.partial(_attn_kernel, tq=tq)
    return pl.pallas_call(
        kern,
        grid=(bsz, nq),
        in_specs=[pl.BlockSpec((tq, N_HEADS * HEAD_Q), lambda b, i: (b * nq + i, 0)),
                  pl.BlockSpec((None, seq, N_HEADS * HEAD_Q), lambda b, i: (b, 0, 0)),
                  pl.BlockSpec((nq, D_ATT, tq), lambda b, i: (b, 0, 0)),
                  pl.BlockSpec((N_META, N_HEADS * HEAD_Q), lambda b, i: (0, 0)),
                  pl.BlockSpec((D_ATT, N_META), lambda b, i: (0, 0)),
                  pl.BlockSpec((tq, D_ATT), lambda b, i: (b * nq + i, C_ZA // D_ATT))],
        out_specs=pl.BlockSpec((tq, D_ATT), lambda b, i: (b * nq + i, 0)),
        out_shape=jax.ShapeDtypeStruct((bsz * seq, D_ATT), BF16),
        scratch_shapes=([pltpu.VMEM((1, tq), F32)] * (2 * N_HEADS)
                        + [pltpu.VMEM((V_HEAD, tq), F32)] * N_HEADS),
        compiler_params=_params(("parallel", "arbitrary"), VMEM_MIB),
        name="attn_prompt")(q, kcat.reshape(bsz, seq, N_HEADS * HEAD_Q), vt, kmeta, vmeta_t, proj)


def _qabsorb_kernel(q_ref, wukt_ref, qa_ref):
    for h in range(N_HEADS):
        qn = q_ref[:, HEAD_Q * h:HEAD_Q * h + QK_NOPE]
        qa_ref[:, KV_LORA * h:KV_LORA * (h + 1)] = jnp.dot(
            qn, wukt_ref[h], preferred_element_type=F32).astype(BF16)


def _qabsorb(q, wukt):
    m = q.shape[0]
    return pl.pallas_call(
        _qabsorb_kernel,
        out_shape=jax.ShapeDtypeStruct((m, N_HEADS * KV_LORA), BF16),
        name="q_absorb")(q, wukt)


def _decode_kernel(pt_ref, qa_ref, qp_ref, cn_ref, kn_ref, ckv_hbm, kpe_hbm, o_ref,
                   cbuf, kbuf, sem, m_ref, l_ref, acc_ref, *, npg, n_pages):
    b, j = pl.program_id(0), pl.program_id(1)
    nj = pl.num_programs(1)
    t = b * nj + j
    last = pl.num_programs(0) * nj - 1
    ahead = DECODE_SLOTS - 1
    slot = t % DECODE_SLOTS

    def page_copies(step, buf, k):
        pg = pt_ref[(step // nj) * n_pages + (step % nj) * npg + k]
        return (pltpu.make_async_copy(ckv_hbm.at[0, pg], cbuf.at[buf, k], sem.at[buf]),
                pltpu.make_async_copy(kpe_hbm.at[0, pg], kbuf.at[buf, k], sem.at[buf]))

    def start_step(step, buf):
        for k in range(npg):
            for cp in page_copies(step, buf, k):
                cp.start()

    def wait_step(step, buf):
        for k in range(npg):
            for cp in page_copies(step, buf, k):
                cp.wait()

    @pl.when(t == 0)
    def _():
        for d in range(ahead):
            start_step(jnp.minimum(d, last), d)

    t_next = jnp.minimum(t + ahead, last)
    start_step(t_next, (t + ahead) % DECODE_SLOTS)
    wait_step(t, slot)
    c_refs = [cbuf.at[slot, k] for k in range(npg)]
    k_refs = [kbuf.at[slot, k] for k in range(npg)]

    @pl.when(j == 0)
    def _():
        m_ref[...] = jnp.full(m_ref.shape, -jnp.inf, F32)
        l_ref[...] = jnp.zeros(l_ref.shape, F32)
        acc_ref[...] = jnp.zeros(acc_ref.shape, F32)

    qa, qp = qa_ref[...], qp_ref[...]
    nt = (((1,), (1,)), ((), ()))
    page = c_refs[0].shape[0]
    cb = [r[...].astype(BF16) for r in c_refs]
    s = jnp.concatenate(
        [lax.dot_general(qa, cb[k], nt, preferred_element_type=F32)
         + jnp.dot(qp, k_refs[k][...].astype(BF16), preferred_element_type=F32)
         for k in range(npg)], axis=1)
    m_prev = m_ref[...]
    m_new = jnp.maximum(m_prev, jnp.max(s, axis=-1, keepdims=True))
    alpha = jnp.exp2(m_prev - m_new)
    p32 = jnp.exp2(s - m_new)
    p = p32.astype(BF16)
    pv = jnp.dot(p[:, :page], cb[0], preferred_element_type=F32)
    for k in range(1, npg):
        pv = pv + jnp.dot(p[:, page * k:page * (k + 1)], cb[k], preferred_element_type=F32)
    l_ref[...] = alpha * l_ref[...] + jnp.sum(p32, axis=-1, keepdims=True)
    acc_ref[...] = alpha * acc_ref[...] + pv
    m_ref[...] = m_new

    @pl.when(j == pl.num_programs(1) - 1)
    def _():
        cn, kn = cn_ref[...], kn_ref[...]
        s_new = (jnp.sum(qa.astype(F32) * cn, axis=-1, keepdims=True)
                 + jnp.sum(qp.astype(F32) * kn, axis=-1, keepdims=True))
        m_old = m_ref[...]
        m_fin = jnp.maximum(m_old, s_new)
        a = jnp.exp2(m_old - m_fin)
        pn = jnp.exp2(s_new - m_fin)
        l_fin = a * l_ref[...] + pn
        o_ref[...] = (a * acc_ref[...] + pn * cn) / l_fin

    @pl.when(t == last)
    def _():
        for d in range(1, DECODE_SLOTS):
            wait_step(last, (t + d) % DECODE_SLOTS)


def _decode(page_table, qa, qp, c_new, k_new, cache_ckv, cache_kpe_t, npg):
    dbs, n_pages = page_table.shape
    page = cache_ckv.shape[2]
    per_b = lambda shape: pl.BlockSpec((None,) + shape, lambda b, j, pt: (b, 0, 0))
    grid_spec = pltpu.PrefetchScalarGridSpec(
        num_scalar_prefetch=1,
        grid=(dbs, n_pages // npg),
        in_specs=[per_b((N_HEADS, KV_LORA)), per_b((N_HEADS, QK_ROPE)),
                  per_b((1, KV_LORA)), per_b((1, QK_ROPE)),
                  pl.BlockSpec(memory_space=pl.ANY), pl.BlockSpec(memory_space=pl.ANY)],
        out_specs=per_b((N_HEADS, KV_LORA)),
        scratch_shapes=[pltpu.VMEM((DECODE_SLOTS, npg, page, KV_LORA), F32),
                        pltpu.VMEM((DECODE_SLOTS, npg, QK_ROPE, page), F32),
                        pltpu.SemaphoreType.DMA((DECODE_SLOTS,)),
                        pltpu.VMEM((N_HEADS, 1), F32), pltpu.VMEM((N_HEADS, 1), F32),
                        pltpu.VMEM((N_HEADS, KV_LORA), F32)])
    return pl.pallas_call(
        functools.partial(_decode_kernel, npg=npg, n_pages=n_pages),
        grid_spec=grid_spec,
        out_shape=jax.ShapeDtypeStruct((dbs, N_HEADS, KV_LORA), F32),
        compiler_params=_params(("arbitrary", "arbitrary"), VMEM_MIB),
        name="attn_decode")(page_table.reshape(-1), qa, qp, c_new, k_new, cache_ckv, cache_kpe_t)


def _ouv_kernel(ol_ref, wuv_ref, za_ref, o_ref):
    za = za_ref[...]
    gate = za * jax.nn.sigmoid(za)
    for h in range(N_HEADS):
        o = jnp.dot(ol_ref[:, KV_LORA * h:KV_LORA * (h + 1)].astype(BF16),
                    wuv_ref[:, V_HEAD * h:V_HEAD * (h + 1)], preferred_element_type=F32)
        o_ref[:, V_HEAD * h:V_HEAD * (h + 1)] = (o * gate[:, V_HEAD * h:V_HEAD * (h + 1)]).astype(BF16)


def _ouv(o_lat, wuv, proj):
    m = o_lat.shape[0]
    const = lambda shape: pl.BlockSpec(shape, lambda i: (0,) * len(shape))
    return pl.pallas_call(
        _ouv_kernel,
        grid=(1,),
        in_specs=[const((m, N_HEADS * KV_LORA)), const((KV_LORA, D_ATT)),
                  pl.BlockSpec((m, D_ATT), lambda i: (0, C_ZA // D_ATT))],
        out_specs=const((m, D_ATT)),
        out_shape=jax.ShapeDtypeStruct((m, D_ATT), BF16),
        name="o_uv")(o_lat, wuv, proj)


def _outproj_kernel(ssm_ref, att_ref, wtop_ref, wbot_ref, x_ref, g_ref, y_ref):
    h = (x_ref[...]
         + jnp.dot(ssm_ref[...], wtop_ref[...], preferred_element_type=F32)
         + jnp.dot(att_ref[...], wbot_ref[...], preferred_element_type=F32))
    y_ref[...] = _rms(h, g_ref[...])


def _outproj(ssm, att, w, x, g, tm):
    m = x.shape[0]
    return pl.pallas_call(
        _outproj_kernel,
        grid=(m // tm,),
        in_specs=[pl.BlockSpec((tm, D_SSM), lambda i: (i, 0)),
                  pl.BlockSpec((tm, D_ATT), lambda i: (i, 0)),
                  pl.BlockSpec((D_SSM, D_MODEL), lambda i: (0, 0)),
                  pl.BlockSpec((D_ATT, D_MODEL), lambda i: (1, 0)),
                  pl.BlockSpec((tm, D_MODEL), lambda i: (i, 0)),
                  pl.BlockSpec((1, D_MODEL), lambda i: (0, 0))],
        out_specs=pl.BlockSpec((tm, D_MODEL), lambda i: (i, 0)),
        out_shape=jax.ShapeDtypeStruct((m, D_MODEL), F32),
        compiler_params=_params(("parallel",), VMEM_MIB),
        name="outproj")(ssm, att, w, w, x, g)


def _slab_minor(h):
    return h.transpose(1, 0, 2).reshape(h.shape[1], D_STATE)


def kernel(x_prompt, x_sample, cache_ckv, cache_kpe, state_ssm_re, state_ssm_im, page_table, meta_tokens, g_norm, w_in, ssm_a_re, ssm_a_im, ssm_log_dt, ssm_b_re, ssm_b_im, ssm_c_re, ssm_c_im, ssm_d, w_glu, b_glu, g_q_norm, w_uq, g_kv_norm, w_uk, w_uv, w_out, g_final):
    bsz, seq, _ = x_prompt.shape
    dbs = x_sample.shape[0]
    n_pages, page = page_table.shape[1], cache_ckv.shape[2]
    past_len = n_pages * page
    layer = 0

    w_in_p = _win_prep(jnp.swapaxes(w_in[layer], 0, 1), 256)
    w_uq_p = jnp.pad(w_uq[layer].reshape(Q_LORA, N_HEADS, QK_NOPE + QK_ROPE),
                     ((0, 0), (0, 0), (0, HEAD_Q - QK_NOPE - QK_ROPE))
                     ).reshape(Q_LORA, N_HEADS * HEAD_Q).astype(BF16)
    w_uk_flat = w_uk[layer].reshape(KV_LORA, D_ATT).astype(BF16)
    w_uk_t = w_uk[layer].transpose(1, 2, 0).astype(BF16)
    w_uv_flat = w_uv[layer].reshape(KV_LORA, D_ATT).astype(BF16)
    w_glu_b = w_glu[layer].astype(BF16)
    w_out_b = w_out[layer].astype(BF16)
    g_in = g_norm[layer].reshape(1, D_MODEL)
    g_q = g_q_norm[layer].reshape(1, Q_LORA)
    g_kv = g_kv_norm[layer].reshape(1, KV_LORA)
    g_fin = g_final.reshape(1, D_MODEL)
    d_skip = ssm_d[layer].reshape(1, D_SSM)
    bglu = b_glu[layer].reshape(1, D_SSM)

    abr, abi, bre, bim, crt, cit = _ssm_prep(ssm_a_re[layer], ssm_a_im[layer], ssm_log_dt[layer],
                                             ssm_b_re[layer], ssm_b_im[layer], ssm_c_re[layer], ssm_c_im[layer])
    abr_f, abi_f = abr.reshape(1, D_STATE), abi.reshape(1, D_STATE)
    ssm_w = (bre, bim, crt, cit, d_skip, w_glu_b, bglu)

    pos = jnp.concatenate([jnp.arange(N_META, N_META + seq), jnp.full((dbs,), past_len), jnp.arange(N_META)])
    tabs = _rope_tables(pos)
    tabs_real = tuple(t[:seq] for t in tabs)
    tabs_small = tuple(t[seq:] for t in tabs)

    x_real = x_prompt.reshape(bsz * seq, D_MODEL)
    x_small = jnp.concatenate([x_sample.reshape(dbs, D_MODEL), meta_tokens.astype(x_prompt.dtype)], axis=0)
    n_small = dbs + N_META
    proj_real = _inproj(x_real, g_in, w_in_p, TM_INPROJ, TN_INPROJ)
    proj_small = _inproj(x_small, g_in, w_in_p, n_small, TN_INPROJ)

    w_uv_t = w_uv_flat.T
    c_real, kr_real, kcat_real, vt_real = _kvprep(proj_real, g_kv, tabs_real, w_uk_flat, w_uv_t, TM_PREP, ATT_TILE)
    c_small, kr_small, kcat_small, vt_small = _kvprep(proj_small, g_kv, tabs_small, w_uk_flat, w_uv_t,
                                                      n_small, n_small)
    q_real = _qproj(proj_real, g_q, w_uq_p, tabs_real, TM_PREP)
    q_small = _qproj(proj_small, g_q, w_uq_p, tabs_small, n_small)

    proj_meta = jnp.broadcast_to(proj_small[dbs:][None], (bsz, N_META, N_PROJ))
    h_zero = jnp.zeros((N_SLABS, 2 * bsz, LANES), F32)
    _, h_meta = _s5_scan(proj_meta, h_zero, abr_f, abi_f, *ssm_w, tc=N_META)
    ssm_real, h_fin = _s5_scan(proj_real.reshape(bsz, seq, N_PROJ), h_meta, abr_f, abi_f, *ssm_w, tc=TC_SCAN)
    h_fin = _slab_minor(h_fin)
    ssm_smp, hr_s, hi_s = _s5_step(proj_small, state_ssm_re[layer].reshape(dbs, D_STATE),
                                   state_ssm_im[layer].reshape(dbs, D_STATE), abr_f, abi_f, *ssm_w)

    att_real = _attention(q_real, kcat_real, vt_real, kcat_small[dbs:], vt_small[0, :, dbs:], proj_real,
                          bsz, seq, ATT_TILE)

    qs = q_small[:dbs]
    qa = _qabsorb(qs, w_uk_t).reshape(dbs, N_HEADS, KV_LORA)
    qp = qs.reshape(dbs, N_HEADS, HEAD_Q)[:, :, QK_NOPE:QK_NOPE + QK_ROPE]
    c_new = c_small[:dbs].reshape(dbs, 1, KV_LORA)
    k_new = kr_small[:dbs, :QK_ROPE].reshape(dbs, 1, QK_ROPE)
    o_lat = _decode(page_table, qa, qp, c_new, k_new, cache_ckv, jnp.swapaxes(cache_kpe, 2, 3), DECODE_PAGES)
    att_smp = _ouv(o_lat.reshape(dbs, N_HEADS * KV_LORA), w_uv_flat, proj_small)

    y_prompt = _outproj(ssm_real.reshape(bsz * seq, D_SSM), att_real, w_out_b, x_real, g_fin, TM_OUTPROJ)
    y_sample = _outproj(ssm_smp, att_smp, w_out_b, x_sample.reshape(dbs, D_MODEL), g_fin, dbs)

    c_meta, k_meta = c_small[dbs:], kr_small[dbs:, :QK_ROPE]
    ckv_p = jnp.concatenate([jnp.broadcast_to(c_meta[None], (bsz, N_META, KV_LORA)),
                             c_real.reshape(bsz, seq, KV_LORA)], axis=1)
    kpe_p = jnp.concatenate([jnp.broadcast_to(k_meta[None], (bsz, N_META, QK_ROPE)),
                             kr_real[:, :QK_ROPE].reshape(bsz, seq, QK_ROPE)], axis=1)
    return (y_prompt.reshape(bsz, seq, D_MODEL),
            y_sample.reshape(dbs, 1, D_MODEL),
            ckv_p[None], kpe_p[None],
            h_fin[:bsz].reshape(1, bsz, N_GROUPS, N_STATE),
            h_fin[bsz:].reshape(1, bsz, N_GROUPS, N_STATE),
            c_small[:dbs].reshape(1, dbs, 1, KV_LORA),
            kr_small[:dbs, :QK_ROPE].reshape(1, dbs, 1, QK_ROPE),
            hr_s.reshape(1, dbs, N_GROUPS, N_STATE),
            hi_s.reshape(1, dbs, N_GROUPS, N_STATE))
```

```python
import functools
import math

import numpy as np
import jax
import jax.numpy as jnp
from jax import lax
from jax.experimental import pallas as pl
from jax.experimental.pallas import tpu as pltpu

F32 = jnp.float32
BF16 = jnp.bfloat16

D_MODEL = 2048
N_META = 16
D_SSM = 1024
SSM_GROUP = 16
N_GROUPS = 64
N_STATE = 64
D_STATE = N_GROUPS * N_STATE
D_ATT = 1024
V_HEAD = 128
N_HEADS = 8
QK_NOPE = 128
QK_ROPE = 64
KV_LORA = 512
Q_LORA = 768
ROPE_THETA = 10000.0
SOFTMAX_SCALE = (QK_NOPE + QK_ROPE) ** -0.5
Q_SCALE = SOFTMAX_SCALE * math.log2(math.e)
EPS = 1e-6

LANES = 128
SUBLANES = 8
MIB = 1024 * 1024

C_CQ = 0
C_KPE = 768
C_U = 1024
C_ZS = 2048
C_ZA = 3072
C_CKV = 4096
N_PROJ = 4608
HEAD_Q = 256
N_SLABS = D_STATE // LANES
SLAB_GROUP = 8

TM_INPROJ, TN_INPROJ = 1024, 1536
TM_PREP = 1024
TM_OUTPROJ = 512
ATT_TILE = 512
TC_SCAN = 128
DECODE_PAGES = 32
DECODE_SLOTS = 3
VMEM_MIB = 48
VMEM_MIB_LARGE = 56
SSM_KT = 256
SSM_NT = SSM_KT * N_STATE // SSM_GROUP


def _params(sem, vmem_mib=None):
    return pltpu.CompilerParams(
        dimension_semantics=sem,
        vmem_limit_bytes=None if vmem_mib is None else vmem_mib * MIB)


def _rms(x, g):
    return x * lax.rsqrt(jnp.mean(x * x, axis=-1, keepdims=True) + EPS) * g


def _rope128(x, cos_t, sin_lo, sin_hi):
    return (x * cos_t + pltpu.roll(x, LANES - QK_ROPE // 2, axis=1) * sin_lo
            + pltpu.roll(x, QK_ROPE // 2, axis=1) * sin_hi)


def _rope_table_kernel(pos_ref, inv_ref, cos_ref, slo_ref, shi_ref):
    ang = pos_ref[...] * inv_ref[...]
    c, s = jnp.cos(ang), jnp.sin(ang)
    lane = lax.broadcasted_iota(jnp.int32, ang.shape, 1)
    half = QK_ROPE // 2
    cos_ref[...] = jnp.where(lane < QK_ROPE, c, 0.0)
    slo_ref[...] = jnp.where(lane < half, -s, 0.0)
    shi_ref[...] = jnp.where((lane >= half) & (lane < QK_ROPE), s, 0.0)


def _rope_tables(pos):
    n = pos.shape[0]
    half = QK_ROPE // 2
    inv = ROPE_THETA ** (-(np.arange(LANES) % half).astype(np.float64) / half)
    inv = jnp.asarray(inv.astype(np.float32)).reshape(1, LANES)
    shp = jax.ShapeDtypeStruct((n, LANES), F32)
    return pl.pallas_call(_rope_table_kernel, out_shape=(shp, shp, shp), name="rope_tables")(
        pos.reshape(n, 1).astype(F32), inv)


def _win_prep_kernel(wt_ref, o_ref):
    o_cq, o_ckv, o_kpe = 2 * D_SSM, 2 * D_SSM + Q_LORA, 2 * D_SSM + Q_LORA + KV_LORA
    o_za = o_kpe + QK_ROPE
    cols = o_ref.shape[1]
    o_ref[C_CQ:C_CQ + Q_LORA, :] = wt_ref[o_cq:o_ckv, :].astype(BF16)
    o_ref[C_KPE:C_KPE + QK_ROPE, :] = wt_ref[o_kpe:o_za, :].astype(BF16)
    o_ref[C_KPE + QK_ROPE:C_U, :] = jnp.zeros((C_U - C_KPE - QK_ROPE, cols), BF16)
    o_ref[C_U:C_ZA, :] = wt_ref[:o_cq, :].astype(BF16)
    o_ref[C_ZA:C_CKV, :] = wt_ref[o_za:, :].astype(BF16)
    o_ref[C_CKV:, :] = wt_ref[o_ckv:o_kpe, :].astype(BF16)


def _win_prep(w_t, tc):
    n, k = w_t.shape
    return pl.pallas_call(
        _win_prep_kernel,
        grid=(k // tc,),
        in_specs=[pl.BlockSpec((n, tc), lambda i: (0, i))],
        out_specs=pl.BlockSpec((N_PROJ, tc), lambda i: (0, i)),
        out_shape=jax.ShapeDtypeStruct((N_PROJ, k), BF16),
        compiler_params=_params(("parallel",)),
        name="win_prep")(w_t)


def _inproj_kernel(x_ref, g_ref, w_ref, o_ref, xn_ref):
    @pl.when(pl.program_id(1) == 0)
    def _():
        xn_ref[...] = _rms(x_ref[...], g_ref[...]).astype(BF16)

    o_ref[...] = lax.dot_general(xn_ref[...], w_ref[...], (((1,), (1,)), ((), ())),
                                 preferred_element_type=F32)


def _inproj(x, g, w_t, tm, tn):
    m = x.shape[0]
    return pl.pallas_call(
        _inproj_kernel,
        grid=(m // tm, N_PROJ // tn),
        in_specs=[pl.BlockSpec((tm, D_MODEL), lambda i, j: (i, 0)),
                  pl.BlockSpec((1, D_MODEL), lambda i, j: (0, 0)),
                  pl.BlockSpec((tn, D_MODEL), lambda i, j: (j, 0))],
        out_specs=pl.BlockSpec((tm, tn), lambda i, j: (i, j)),
        out_shape=jax.ShapeDtypeStruct((m, N_PROJ), F32),
        scratch_shapes=[pltpu.VMEM((tm, D_MODEL), BF16)],
        compiler_params=_params(("parallel", "arbitrary"), VMEM_MIB_LARGE),
        name="inproj")(x, g, w_t)


def _kvprep_kernel(ckv_ref, kpe_ref, g_ref, cos_ref, slo_ref, shi_ref, wuk_ref, wuvt_ref,
                   c_ref, kr_ref, kcat_ref, vt_ref, *, tk):
    c = _rms(ckv_ref[...], g_ref[...])
    c_ref[...] = c
    cb = c.astype(BF16)
    kr = _rope128(kpe_ref[...], cos_ref[...], slo_ref[...], shi_ref[...])
    kr_ref[...] = kr
    krb = kr.astype(BF16)
    knope = jnp.dot(cb, wuk_ref[...], preferred_element_type=F32).astype(BF16)
    for h in range(N_HEADS):
        kcat_ref[:, HEAD_Q * h:HEAD_Q * h + QK_NOPE] = knope[:, QK_NOPE * h:QK_NOPE * (h + 1)]
        kcat_ref[:, HEAD_Q * h + QK_NOPE:HEAD_Q * (h + 1)] = krb
    vt = lax.dot_general(wuvt_ref[...], cb, (((1,), (1,)), ((), ())),
                         preferred_element_type=F32).astype(BF16)
    for kb in range(vt_ref.shape[0]):
        vt_ref[kb] = vt[:, tk * kb:tk * (kb + 1)]


def _kvprep(proj, g, tabs, wuk, wuvt, tm, tk):
    m = proj.shape[0]
    nt = tabs[0].shape[0] // tm
    tab = pl.BlockSpec((tm, LANES), lambda i: (i % nt, 0))
    row = lambda w: pl.BlockSpec((tm, w), lambda i: (i, 0))
    return pl.pallas_call(
        functools.partial(_kvprep_kernel, tk=tk),
        grid=(m // tm,),
        in_specs=[pl.BlockSpec((tm, KV_LORA), lambda i: (i, C_CKV // KV_LORA)),
                  pl.BlockSpec((tm, LANES), lambda i: (i, C_KPE // LANES)),
                  pl.BlockSpec((1, KV_LORA), lambda i: (0, 0)),
                  tab, tab, tab,
                  pl.BlockSpec((KV_LORA, D_ATT), lambda i: (0, 0)),
                  pl.BlockSpec((D_ATT, KV_LORA), lambda i: (0, 0))],
        out_specs=(row(KV_LORA), row(LANES), row(N_HEADS * HEAD_Q),
                   pl.BlockSpec((tm // tk, D_ATT, tk), lambda i: (i, 0, 0))),
        out_shape=(jax.ShapeDtypeStruct((m, KV_LORA), F32),
                   jax.ShapeDtypeStruct((m, LANES), F32),
                   jax.ShapeDtypeStruct((m, N_HEADS * HEAD_Q), BF16),
                   jax.ShapeDtypeStruct((m // tk, D_ATT, tk), BF16)),
        compiler_params=_params(("parallel",), VMEM_MIB),
        name="kvprep")(proj, proj, g, *tabs, wuk, wuvt)


def _qproj_kernel(cq_ref, g_ref, w_ref, cos_ref, slo_ref, shi_ref, q_ref):
    cqn = _rms(cq_ref[...], g_ref[...]).astype(BF16)
    q = jnp.dot(cqn, w_ref[...], preferred_element_type=F32)
    cos_t, slo, shi = cos_ref[...], slo_ref[...], shi_ref[...]
    for h in range(N_HEADS):
        lo = HEAD_Q * h
        q_ref[:, lo:lo + QK_NOPE] = (q[:, lo:lo + QK_NOPE] * Q_SCALE).astype(BF16)
        pe = _rope128(q[:, lo + QK_NOPE:lo + HEAD_Q], cos_t, slo, shi)
        q_ref[:, lo + QK_NOPE:lo + HEAD_Q] = (pe * Q_SCALE).astype(BF16)


def _qproj(proj, g, w, tabs, tm):
    m = proj.shape[0]
    nt = tabs[0].shape[0] // tm
    tab = pl.BlockSpec((tm, LANES), lambda i: (i % nt, 0))
    return pl.pallas_call(
        _qproj_kernel,
        grid=(m // tm,),
        in_specs=[pl.BlockSpec((tm, Q_LORA), lambda i: (i, C_CQ // Q_LORA)),
                  pl.BlockSpec((1, Q_LORA), lambda i: (0, 0)),
                  pl.BlockSpec((Q_LORA, N_HEADS * HEAD_Q), lambda i: (0, 0)),
                  tab, tab, tab],
        out_specs=pl.BlockSpec((tm, N_HEADS * HEAD_Q), lambda i: (i, 0)),
        out_shape=jax.ShapeDtypeStruct((m, N_HEADS * HEAD_Q), BF16),
        compiler_params=_params(("parallel",), VMEM_MIB),
        name="qproj")(proj, g, w, *tabs)


def _ssm_prep_kernel(are_ref, aim_ref, ldt_ref, bre_ref, bim_ref, cre_ref, cim_ref,
                     abr_ref, abi_ref, wbr_ref, wbi_ref, wcr_ref, wci_ref):
    dt = jnp.exp(ldt_ref[...])
    ar, ai = are_ref[...], aim_ref[...]
    mag = jnp.exp(dt * ar)
    abr, abi = mag * jnp.cos(dt * ai), mag * jnp.sin(dt * ai)
    den = ar * ar + ai * ai
    nr, ni = abr - 1.0, abi
    f_re = (nr * ar + ni * ai) / den
    f_im = (ni * ar - nr * ai) / den
    br, bi = bre_ref[...], bim_ref[...]
    abr_ref[...] = abr
    abi_ref[...] = abi
    bbr = f_re * br - f_im * bi
    bbi = f_re * bi + f_im * br

    gpt = SSM_KT // SSM_GROUP
    n_kt = D_SSM // SSM_KT

    def spread(x, width, period):
        sel = (lax.broadcasted_iota(jnp.int32, (period, width), 1) % period
               == lax.broadcasted_iota(jnp.int32, (period, width), 0))
        return jnp.dot(x.astype(BF16), jnp.where(sel, 1.0, 0.0).astype(BF16), preferred_element_type=F32)

    r_in = lax.broadcasted_iota(jnp.int32, (SSM_KT, SSM_NT), 0) // SSM_GROUP
    c_in = lax.broadcasted_iota(jnp.int32, (SSM_KT, SSM_NT), 1) // N_STATE
    r_out = lax.broadcasted_iota(jnp.int32, (SSM_NT, SSM_KT), 0) // N_STATE
    c_out = lax.broadcasted_iota(jnp.int32, (SSM_NT, SSM_KT), 1) // SSM_GROUP
    for kt in range(n_kt):
        gs = slice(gpt * kt, gpt * (kt + 1))
        for src, dst in ((bbr, wbr_ref), (bbi, wbi_ref)):
            x = src[gs].reshape(SSM_KT, N_STATE)
            dst[kt] = jnp.where(r_in == c_in, spread(x, SSM_NT, N_STATE), 0.0).astype(BF16)
        for src, dst in ((cre_ref, wcr_ref), (cim_ref, wci_ref)):
            x = src[gs].reshape(SSM_NT, SSM_GROUP)
            dst[kt] = jnp.where(r_out == c_out, spread(x, SSM_KT, SSM_GROUP), 0.0).astype(BF16)


def _ssm_prep(a_re, a_im, log_dt, b_re, b_im, c_re, c_im):
    gn = jax.ShapeDtypeStruct((N_GROUPS, 1, N_STATE), F32)
    w_in_shape = jax.ShapeDtypeStruct((D_SSM // SSM_KT, SSM_KT, SSM_NT), BF16)
    w_out_shape = jax.ShapeDtypeStruct((D_SSM // SSM_KT, SSM_NT, SSM_KT), BF16)
    return pl.pallas_call(
        _ssm_prep_kernel, out_shape=(gn, gn, w_in_shape, w_in_shape, w_out_shape, w_out_shape),
        name="ssm_prep")(
            a_re.reshape(N_GROUPS, 1, N_STATE), a_im.reshape(N_GROUPS, 1, N_STATE),
            log_dt.reshape(N_GROUPS, 1, 1), b_re.transpose(0, 2, 1), b_im.transpose(0, 2, 1),
            c_re.transpose(0, 2, 1), c_im.transpose(0, 2, 1))


def _ssm_epilogue(y, uf, zs, d, wglu, bglu):
    ys = jax.nn.gelu(y + d * uf)
    gate = jax.nn.sigmoid(jnp.dot(ys.astype(BF16), wglu, preferred_element_type=F32) + bglu)
    return ys * gate * (zs * jax.nn.sigmoid(zs))


def _s5_scan_kernel(u_ref, zs_ref, h0_ref, abr_ref, abi_ref, bre_ref, bim_ref, crt_ref, cit_ref,
                    d_ref, wglu_ref, bglu_ref, out_ref, hfin_ref, buf_ref, hst_ref, y_ref,
                    *, nb, tc, pitch):
    i = pl.program_id(0)

    @pl.when(i == 0)
    def _():
        hst_ref[...] = h0_ref[...]

    uf = u_ref[...].reshape(nb * tc, D_SSM)
    ub = uf.astype(BF16)
    n_kt = D_SSM // SSM_KT
    slabs_per_kt = SSM_NT // LANES
    row = lax.broadcasted_iota(jnp.int32, (2 * nb, LANES), 0)

    def drive(kt):
        lhs = ub[:, SSM_KT * kt:SSM_KT * (kt + 1)]
        for ri, w_ref in ((0, bre_ref), (1, bim_ref)):
            res = jnp.dot(lhs, w_ref[kt], preferred_element_type=F32)
            for s in range(slabs_per_kt):
                for b in range(nb):
                    r0 = (ri * nb + b) * pitch
                    buf_ref[kt * slabs_per_kt + s, r0:r0 + tc, :] = (
                        res[b * tc:(b + 1) * tc, LANES * s:LANES * (s + 1)])

    def scan_group(g):
        base = g * SLAB_GROUP
        a1, a2, h = [], [], []
        for s in range(SLAB_GROUP):
            a1.append(jnp.broadcast_to(abr_ref[base + s], (2 * nb, LANES)))
            ai = jnp.broadcast_to(abi_ref[base + s], (2 * nb, LANES))
            a2.append(jnp.where(row < nb, -ai, ai))
            h.append(hst_ref[base + s])
        for t in range(tc):
            for s in range(SLAB_GROUP):
                bu = buf_ref[base + s, pl.ds(t, 2 * nb, stride=pitch), :]
                h[s] = a1[s] * h[s] + a2[s] * pltpu.roll(h[s], nb, axis=0) + bu
                buf_ref[base + s, pl.ds(t, 2 * nb, stride=pitch), :] = h[s]
        for s in range(SLAB_GROUP):
            hst_ref[base + s] = h[s]

    def states(jt, ri):
        rows = []
        for b in range(nb):
            r0 = (ri * nb + b) * pitch
            rows.append(jnp.concatenate(
                [buf_ref[jt * slabs_per_kt + s, r0:r0 + tc, :] for s in range(slabs_per_kt)], axis=1))
        return jnp.concatenate(rows, axis=0).astype(BF16)

    def readout(jt):
        y_ref[:, SSM_KT * jt:SSM_KT * (jt + 1)] = (
            jnp.dot(states(jt, 0), crt_ref[jt], preferred_element_type=F32)
            - jnp.dot(states(jt, 1), cit_ref[jt], preferred_element_type=F32))

    def scan_step(g, carry):
        scan_group(g)
        return carry

    for kt in range(n_kt):
        drive(kt)
    lax.fori_loop(0, N_SLABS // SLAB_GROUP, scan_step, 0)
    for jt in range(n_kt):
        readout(jt)

    zs = zs_ref[...].reshape(nb * tc, D_SSM)
    out = _ssm_epilogue(y_ref[...], uf, zs, d_ref[...], wglu_ref[...], bglu_ref[...])
    out_ref[...] = out.reshape(nb, tc, D_SSM).astype(BF16)

    @pl.when(i == pl.num_programs(0) - 1)
    def _():
        hfin_ref[...] = hst_ref[...]


def _s5_scan(proj3, h0, abr, abi, bre, bim, crt, cit, d, wglu, bglu, tc):
    nb, t_len = proj3.shape[0], proj3.shape[1]
    pitch = tc + SUBLANES // 2
    const = lambda shape: pl.BlockSpec(shape, lambda i: (0,) * len(shape))
    kern = functools.partial(_s5_scan_kernel, nb=nb, tc=tc, pitch=pitch)
    return pl.pallas_call(
        kern,
        grid=(t_len // tc,),
        in_specs=[pl.BlockSpec((nb, tc, D_SSM), lambda i: (0, i, C_U // D_SSM)),
                  pl.BlockSpec((nb, tc, D_SSM), lambda i: (0, i, C_ZS // D_SSM)),
                  const((N_SLABS, 2 * nb, LANES)),
                  const((N_SLABS, 1, LANES)), const((N_SLABS, 1, LANES)),
                  const(bre.shape), const(bim.shape), const(crt.shape), const(cit.shape),
                  const((1, D_SSM)), const((D_SSM, D_SSM)), const((1, D_SSM))],
        out_specs=(pl.BlockSpec((nb, tc, D_SSM), lambda i: (0, i, 0)),
                   const((N_SLABS, 2 * nb, LANES))),
        out_shape=(jax.ShapeDtypeStruct((nb, t_len, D_SSM), BF16),
                   jax.ShapeDtypeStruct((N_SLABS, 2 * nb, LANES), F32)),
        scratch_shapes=[pltpu.VMEM((N_SLABS, 2 * nb * pitch, LANES), F32),
                        pltpu.VMEM((N_SLABS, 2 * nb, LANES), F32),
                        pltpu.VMEM((nb * tc, D_SSM), F32)],
        compiler_params=_params(("arbitrary",), VMEM_MIB_LARGE),
        name="s5_scan")(proj3, proj3, h0, abr.reshape(N_SLABS, 1, LANES), abi.reshape(N_SLABS, 1, LANES),
                        bre, bim, crt, cit, d, wglu, bglu)


def _s5_step_kernel(u_ref, zs_ref, h0r_ref, h0i_ref, abr_ref, abi_ref, bre_ref, bim_ref, crt_ref,
                    cit_ref, d_ref, wglu_ref, bglu_ref, out_ref, hr_ref, hi_ref, y_ref):
    uf = u_ref[...]
    ub = uf.astype(BF16)
    n_kt = D_SSM // SSM_KT
    for kt in range(n_kt):
        lhs = ub[:, SSM_KT * kt:SSM_KT * (kt + 1)]
        sl = slice(SSM_NT * kt, SSM_NT * (kt + 1))
        ar, ai = abr_ref[:, sl], abi_ref[:, sl]
        h0r, h0i = h0r_ref[:, sl], h0i_ref[:, sl]
        hr_ref[:, sl] = ar * h0r - ai * h0i + jnp.dot(lhs, bre_ref[kt], preferred_element_type=F32)
        hi_ref[:, sl] = ar * h0i + ai * h0r + jnp.dot(lhs, bim_ref[kt], preferred_element_type=F32)
    for jt in range(n_kt):
        sl = slice(SSM_NT * jt, SSM_NT * (jt + 1))
        y_ref[:, SSM_KT * jt:SSM_KT * (jt + 1)] = (
            jnp.dot(hr_ref[:, sl].astype(BF16), crt_ref[jt], preferred_element_type=F32)
            - jnp.dot(hi_ref[:, sl].astype(BF16), cit_ref[jt], preferred_element_type=F32))
    out = _ssm_epilogue(y_ref[...], uf, zs_ref[...], d_ref[...], wglu_ref[...], bglu_ref[...])
    out_ref[...] = out.astype(BF16)


def _s5_step(proj, h0r, h0i, abr, abi, bre, bim, crt, cit, d, wglu, bglu):
    m = h0r.shape[0]
    const = lambda shape: pl.BlockSpec(shape, lambda i: (0,) * len(shape))
    return pl.pallas_call(
        _s5_step_kernel,
        grid=(1,),
        in_specs=[pl.BlockSpec((m, D_SSM), lambda i: (0, C_U // D_SSM)),
                  pl.BlockSpec((m, D_SSM), lambda i: (0, C_ZS // D_SSM)),
                  const((m, D_STATE)), const((m, D_STATE)),
                  const((1, D_STATE)), const((1, D_STATE)),
                  const(bre.shape), const(bim.shape), const(crt.shape), const(cit.shape),
                  const((1, D_SSM)), const((D_SSM, D_SSM)), const((1, D_SSM))],
        out_specs=(const((m, D_SSM)), const((m, D_STATE)), const((m, D_STATE))),
        out_shape=(jax.ShapeDtypeStruct((m, D_SSM), BF16),
                   jax.ShapeDtypeStruct((m, D_STATE), F32),
                   jax.ShapeDtypeStruct((m, D_STATE), F32)),
        scratch_shapes=[pltpu.VMEM((m, D_SSM), F32)],
        compiler_params=_params(("arbitrary",), VMEM_MIB),
        name="s5_step")(proj, proj, h0r, h0i, abr, abi, bre, bim, crt, cit, d, wglu, bglu)


def _attn_kernel(q_ref, k_ref, vt_ref, km_ref, vmt_ref, za_ref, o_ref, *scratch, tq):
    m_ref, l_ref, acc_ref = (scratch[:N_HEADS], scratch[N_HEADS:2 * N_HEADS], scratch[2 * N_HEADS:])
    qi = pl.program_id(1)
    nt = (((1,), (1,)), ((), ()))
    heads = [(slice(HEAD_Q * h, HEAD_Q * (h + 1)), slice(V_HEAD * h, V_HEAD * (h + 1)))
             for h in range(N_HEADS)]

    s_meta = [lax.dot_general(km_ref[:, qs], q_ref[:, qs], nt, preferred_element_type=F32)
              for qs, _ in heads]
    p_meta = []
    for h, s in enumerate(s_meta):
        m = jnp.max(s, axis=0, keepdims=True)
        p = jnp.exp2(s - m)
        m_ref[h][...] = m
        l_ref[h][...] = jnp.sum(p, axis=0, keepdims=True)
        p_meta.append(p.astype(BF16))
    for h, (_, vs) in enumerate(heads):
        acc_ref[h][...] = jnp.dot(vmt_ref[vs, :], p_meta[h], preferred_element_type=F32)

    def kv_block(j, masked):
        off = pl.multiple_of(j * tq, tq)
        if masked:
            key = lax.broadcasted_iota(jnp.int32, (tq, tq), 0)
            qry = lax.broadcasted_iota(jnp.int32, (tq, tq), 1)
            keep = key <= qry

        def scores(h):
            qs = heads[h][0]
            return lax.dot_general(k_ref[pl.ds(off, tq), qs], q_ref[:, qs], nt,
                                   preferred_element_type=F32)

        def accumulate(h, alpha, pb):
            vs = heads[h][1]
            acc_ref[h][...] = alpha * acc_ref[h][...] + jnp.dot(
                vt_ref[j, vs, :], pb, preferred_element_type=F32)

        s_next = scores(0)
        pending = None
        for h in range(N_HEADS):
            s = s_next
            if h + 1 < N_HEADS:
                s_next = scores(h + 1)
            if masked:
                s = jnp.where(keep, s, -jnp.inf)
            m_prev = m_ref[h][...]
            m_new = jnp.maximum(m_prev, jnp.max(s, axis=0, keepdims=True))
            alpha = jnp.exp2(m_prev - m_new)
            p = jnp.exp2(s - m_new)
            l_ref[h][...] = alpha * l_ref[h][...] + jnp.sum(p, axis=0, keepdims=True)
            m_ref[h][...] = m_new
            if pending is not None:
                accumulate(*pending)
            pending = (h, alpha, p.astype(BF16))
        accumulate(*pending)

    def full_block(j, carry):
        kv_block(j, False)
        return carry

    lax.fori_loop(0, qi, full_block, 0)
    kv_block(qi, True)
    za = za_ref[...]
    gate = za * jax.nn.sigmoid(za)
    for h, (qs, vs) in enumerate(heads):
        o = (acc_ref[h][...] / l_ref[h][...]).T
        o_ref[:, vs] = (o * gate[:, vs]).astype(BF16)


def _attention(q, kcat, vt, kmeta, vmeta_t, proj, bsz, seq, tq):
    nq = seq // tq
    kern = functools.partial(_attn_kernel, tq=tq)
    return pl.pallas_call(
        kern,
        grid=(bsz, nq),
        in_specs=[pl.BlockSpec((tq, N_HEADS * HEAD_Q), lambda b, i: (b * nq + i, 0)),
                  pl.BlockSpec((None, seq, N_HEADS * HEAD_Q), lambda b, i: (b, 0, 0)),
                  pl.BlockSpec((nq, D_ATT, tq), lambda b, i: (b, 0, 0)),
                  pl.BlockSpec((N_META, N_HEADS * HEAD_Q), lambda b, i: (0, 0)),
                  pl.BlockSpec((D_ATT, N_META), lambda b, i: (0, 0)),
                  pl.BlockSpec((tq, D_ATT), lambda b, i: (b * nq + i, C_ZA // D_ATT))],
        out_specs=pl.BlockSpec((tq, D_ATT), lambda b, i: (b * nq + i, 0)),
        out_shape=jax.ShapeDtypeStruct((bsz * seq, D_ATT), BF16),
        scratch_shapes=([pltpu.VMEM((1, tq), F32)] * (2 * N_HEADS)
                        + [pltpu.VMEM((V_HEAD, tq), F32)] * N_HEADS),
        compiler_params=_params(("parallel", "arbitrary"), VMEM_MIB),
        name="attn_prompt")(q, kcat.reshape(bsz, seq, N_HEADS * HEAD_Q), vt, kmeta, vmeta_t, proj)


def _qabsorb_kernel(q_ref, wukt_ref, qa_ref):
    for h in range(N_HEADS):
        qn = q_ref[:, HEAD_Q * h:HEAD_Q * h + QK_NOPE]
        qa_ref[:, KV_LORA * h:KV_LORA * (h + 1)] = jnp.dot(
            qn, wukt_ref[h], preferred_element_type=F32).astype(BF16)


def _qabsorb(q, wukt):
    m = q.shape[0]
    return pl.pallas_call(
        _qabsorb_kernel,
        out_shape=jax.ShapeDtypeStruct((m, N_HEADS * KV_LORA), BF16),
        name="q_absorb")(q, wukt)


def _decode_kernel(pt_ref, qa_ref, qp_ref, cn_ref, kn_ref, ckv_hbm, kpe_hbm, o_ref,
                   cbuf, kbuf, sem, m_ref, l_ref, acc_ref, *, npg, n_pages):
    b, j = pl.program_id(0), pl.program_id(1)
    nj = pl.num_programs(1)
    t = b * nj + j
    last = pl.num_programs(0) * nj - 1
    ahead = DECODE_SLOTS - 1
    slot = t % DECODE_SLOTS

    def page_copies(step, buf, k):
        pg = pt_ref[(step // nj) * n_pages + (step % nj) * npg + k]
        return (pltpu.make_async_copy(ckv_hbm.at[0, pg], cbuf.at[buf, k], sem.at[buf]),
                pltpu.make_async_copy(kpe_hbm.at[0, pg], kbuf.at[buf, k], sem.at[buf]))

    def start_step(step, buf):
        for k in range(npg):
            for cp in page_copies(step, buf, k):
                cp.start()

    def wait_step(step, buf):
        for k in range(npg):
            for cp in page_copies(step, buf, k):
                cp.wait()

    @pl.when(t == 0)
    def _():
        for d in range(ahead):
            start_step(jnp.minimum(d, last), d)

    t_next = jnp.minimum(t + ahead, last)
    start_step(t_next, (t + ahead) % DECODE_SLOTS)
    wait_step(t, slot)
    c_refs = [cbuf.at[slot, k] for k in range(npg)]
    k_refs = [kbuf.at[slot, k] for k in range(npg)]

    @pl.when(j == 0)
    def _():
        m_ref[...] = jnp.full(m_ref.shape, -jnp.inf, F32)
        l_ref[...] = jnp.zeros(l_ref.shape, F32)
        acc_ref[...] = jnp.zeros(acc_ref.shape, F32)

    qa, qp = qa_ref[...], qp_ref[...]
    nt = (((1,), (1,)), ((), ()))
    page = c_refs[0].shape[0]
    cb = [r[...].astype(BF16) for r in c_refs]
    s = jnp.concatenate(
        [lax.dot_general(qa, cb[k], nt, preferred_element_type=F32)
         + jnp.dot(qp, k_refs[k][...].astype(BF16), preferred_element_type=F32)
         for k in range(npg)], axis=1)
    m_prev = m_ref[...]
    m_new = jnp.maximum(m_prev, jnp.max(s, axis=-1, keepdims=True))
    alpha = jnp.exp2(m_prev - m_new)
    p32 = jnp.exp2(s - m_new)
    p = p32.astype(BF16)
    pv = jnp.dot(p[:, :page], cb[0], preferred_element_type=F32)
    for k in range(1, npg):
        pv = pv + jnp.dot(p[:, page * k:page * (k + 1)], cb[k], preferred_element_type=F32)
    l_ref[...] = alpha * l_ref[...] + jnp.sum(p32, axis=-1, keepdims=True)
    acc_ref[...] = alpha * acc_ref[...] + pv
    m_ref[...] = m_new

    @pl.when(j == pl.num_programs(1) - 1)
    def _():
        cn, kn = cn_ref[...], kn_ref[...]
        s_new = (jnp.sum(qa.astype(F32) * cn, axis=-1, keepdims=True)
                 + jnp.sum(qp.astype(F32) * kn, axis=-1, keepdims=True))
        m_old = m_ref[...]
        m_fin = jnp.maximum(m_old, s_new)
        a = jnp.exp2(m_old - m_fin)
        pn = jnp.exp2(s_new - m_fin)
        l_fin = a * l_ref[...] + pn
        o_ref[...] = (a * acc_ref[...] + pn * cn) / l_fin

    @pl.when(t == last)
    def _():
        for d in range(1, DECODE_SLOTS):
            wait_step(last, (t + d) % DECODE_SLOTS)


def _decode(page_table, qa, qp, c_new, k_new, cache_ckv, cache_kpe_t, npg):
    dbs, n_pages = page_table.shape
    page = cache_ckv.shape[2]
    per_b = lambda shape: pl.BlockSpec((None,) + shape, lambda b, j, pt: (b, 0, 0))
    grid_spec = pltpu.PrefetchScalarGridSpec(
        num_scalar_prefetch=1,
        grid=(dbs, n_pages // npg),
        in_specs=[per_b((N_HEADS, KV_LORA)), per_b((N_HEADS, QK_ROPE)),
                  per_b((1, KV_LORA)), per_b((1, QK_ROPE)),
                  pl.BlockSpec(memory_space=pl.ANY), pl.BlockSpec(memory_space=pl.ANY)],
        out_specs=per_b((N_HEADS, KV_LORA)),
        scratch_shapes=[pltpu.VMEM((DECODE_SLOTS, npg, page, KV_LORA), F32),
                        pltpu.VMEM((DECODE_SLOTS, npg, QK_ROPE, page), F32),
                        pltpu.SemaphoreType.DMA((DECODE_SLOTS,)),
                        pltpu.VMEM((N_HEADS, 1), F32), pltpu.VMEM((N_HEADS, 1), F32),
                        pltpu.VMEM((N_HEADS, KV_LORA), F32)])
    return pl.pallas_call(
        functools.partial(_decode_kernel, npg=npg, n_pages=n_pages),
        grid_spec=grid_spec,
        out_shape=jax.ShapeDtypeStruct((dbs, N_HEADS, KV_LORA), F32),
        compiler_params=_params(("arbitrary", "arbitrary"), VMEM_MIB),
        name="attn_decode")(page_table.reshape(-1), qa, qp, c_new, k_new, cache_ckv, cache_kpe_t)


def _ouv_kernel(ol_ref, wuv_ref, za_ref, o_ref):
    za = za_ref[...]
    gate = za * jax.nn.sigmoid(za)
    for h in range(N_HEADS):
        o = jnp.dot(ol_ref[:, KV_LORA * h:KV_LORA * (h + 1)].astype(BF16),
                    wuv_ref[:, V_HEAD * h:V_HEAD * (h + 1)], preferred_element_type=F32)
        o_ref[:, V_HEAD * h:V_HEAD * (h + 1)] = (o * gate[:, V_HEAD * h:V_HEAD * (h + 1)]).astype(BF16)


def _ouv(o_lat, wuv, proj):
    m = o_lat.shape[0]
    const = lambda shape: pl.BlockSpec(shape, lambda i: (0,) * len(shape))
    return pl.pallas_call(
        _ouv_kernel,
        grid=(1,),
        in_specs=[const((m, N_HEADS * KV_LORA)), const((KV_LORA, D_ATT)),
                  pl.BlockSpec((m, D_ATT), lambda i: (0, C_ZA // D_ATT))],
        out_specs=const((m, D_ATT)),
        out_shape=jax.ShapeDtypeStruct((m, D_ATT), BF16),
        name="o_uv")(o_lat, wuv, proj)


def _outproj_kernel(ssm_ref, att_ref, wtop_ref, wbot_ref, x_ref, g_ref, y_ref):
    h = (x_ref[...]
         + jnp.dot(ssm_ref[...], wtop_ref[...], preferred_element_type=F32)
         + jnp.dot(att_ref[...], wbot_ref[...], preferred_element_type=F32))
    y_ref[...] = _rms(h, g_ref[...])


def _outproj(ssm, att, w, x, g, tm):
    m = x.shape[0]
    return pl.pallas_call(
        _outproj_kernel,
        grid=(m // tm,),
        in_specs=[pl.BlockSpec((tm, D_SSM), lambda i: (i, 0)),
                  pl.BlockSpec((tm, D_ATT), lambda i: (i, 0)),
                  pl.BlockSpec((D_SSM, D_MODEL), lambda i: (0, 0)),
                  pl.BlockSpec((D_ATT, D_MODEL), lambda i: (1, 0)),
                  pl.BlockSpec((tm, D_MODEL), lambda i: (i, 0)),
                  pl.BlockSpec((1, D_MODEL), lambda i: (0, 0))],
        out_specs=pl.BlockSpec((tm, D_MODEL), lambda i: (i, 0)),
        out_shape=jax.ShapeDtypeStruct((m, D_MODEL), F32),
        compiler_params=_params(("parallel",), VMEM_MIB),
        name="outproj")(ssm, att, w, w, x, g)


def _slab_minor(h):
    return h.transpose(1, 0, 2).reshape(h.shape[1], D_STATE)


def kernel(x_prompt, x_sample, cache_ckv, cache_kpe, state_ssm_re, state_ssm_im, page_table, meta_tokens, g_norm, w_in, ssm_a_re, ssm_a_im, ssm_log_dt, ssm_b_re, ssm_b_im, ssm_c_re, ssm_c_im, ssm_d, w_glu, b_glu, g_q_norm, w_uq, g_kv_norm, w_uk, w_uv, w_out, g_final):
    bsz, seq, _ = x_prompt.shape
    dbs = x_sample.shape[0]
    n_pages, page = page_table.shape[1], cache_ckv.shape[2]
    past_len = n_pages * page
    layer = 0

    w_in_p = _win_prep(jnp.swapaxes(w_in[layer], 0, 1), 256)
    w_uq_p = jnp.pad(w_uq[layer].reshape(Q_LORA, N_HEADS, QK_NOPE + QK_ROPE),
                     ((0, 0), (0, 0), (0, HEAD_Q - QK_NOPE - QK_ROPE))
                     ).reshape(Q_LORA, N_HEADS * HEAD_Q).astype(BF16)
    w_uk_flat = w_uk[layer].reshape(KV_LORA, D_ATT).astype(BF16)
    w_uk_t = w_uk[layer].transpose(1, 2, 0).astype(BF16)
    w_uv_flat = w_uv[layer].reshape(KV_LORA, D_ATT).astype(BF16)
    w_glu_b = w_glu[layer].astype(BF16)
    w_out_b = w_out[layer].astype(BF16)
    g_in = g_norm[layer].reshape(1, D_MODEL)
    g_q = g_q_norm[layer].reshape(1, Q_LORA)
    g_kv = g_kv_norm[layer].reshape(1, KV_LORA)
    g_fin = g_final.reshape(1, D_MODEL)
    d_skip = ssm_d[layer].reshape(1, D_SSM)
    bglu = b_glu[layer].reshape(1, D_SSM)

    abr, abi, bre, bim, crt, cit = _ssm_prep(ssm_a_re[layer], ssm_a_im[layer], ssm_log_dt[layer],
                                             ssm_b_re[layer], ssm_b_im[layer], ssm_c_re[layer], ssm_c_im[layer])
    abr_f, abi_f = abr.reshape(1, D_STATE), abi.reshape(1, D_STATE)
    ssm_w = (bre, bim, crt, cit, d_skip, w_glu_b, bglu)

    pos = jnp.concatenate([jnp.arange(N_META, N_META + seq), jnp.full((dbs,), past_len), jnp.arange(N_META)])
    tabs = _rope_tables(pos)
    tabs_real = tuple(t[:seq] for t in tabs)
    tabs_small = tuple(t[seq:] for t in tabs)

    x_real = x_prompt.reshape(bsz * seq, D_MODEL)
    x_small = jnp.concatenate([x_sample.reshape(dbs, D_MODEL), meta_tokens.astype(x_prompt.dtype)], axis=0)
    n_small = dbs + N_META
    proj_real = _inproj(x_real, g_in, w_in_p, TM_INPROJ, TN_INPROJ)
    proj_small = _inproj(x_small, g_in, w_in_p, n_small, TN_INPROJ)

    w_uv_t = w_uv_flat.T
    c_real, kr_real, kcat_real, vt_real = _kvprep(proj_real, g_kv, tabs_real, w_uk_flat, w_uv_t, TM_PREP, ATT_TILE)
    c_small, kr_small, kcat_small, vt_small = _kvprep(proj_small, g_kv, tabs_small, w_uk_flat, w_uv_t,
                                                      n_small, n_small)
    q_real = _qproj(proj_real, g_q, w_uq_p, tabs_real, TM_PREP)
    q_small = _qproj(proj_small, g_q, w_uq_p, tabs_small, n_small)

    proj_meta = jnp.broadcast_to(proj_small[dbs:][None], (bsz, N_META, N_PROJ))
    h_zero = jnp.zeros((N_SLABS, 2 * bsz, LANES), F32)
    _, h_meta = _s5_scan(proj_meta, h_zero, abr_f, abi_f, *ssm_w, tc=N_META)
    ssm_real, h_fin = _s5_scan(proj_real.reshape(bsz, seq, N_PROJ), h_meta, abr_f, abi_f, *ssm_w, tc=TC_SCAN)
    h_fin = _slab_minor(h_fin)
    ssm_smp, hr_s, hi_s = _s5_step(proj_small, state_ssm_re[layer].reshape(dbs, D_STATE),
                                   state_ssm_im[layer].reshape(dbs, D_STATE), abr_f, abi_f, *ssm_w)

    att_real = _attention(q_real, kcat_real, vt_real, kcat_small[dbs:], vt_small[0, :, dbs:], proj_real,
                          bsz, seq, ATT_TILE)

    qs = q_small[:dbs]
    qa = _qabsorb(qs, w_uk_t).reshape(dbs, N_HEADS, KV_LORA)
    qp = qs.reshape(dbs, N_HEADS, HEAD_Q)[:, :, QK_NOPE:QK_NOPE + QK_ROPE]
    c_new = c_small[:dbs].reshape(dbs, 1, KV_LORA)
    k_new = kr_small[:dbs, :QK_ROPE].reshape(dbs, 1, QK_ROPE)
    o_lat = _decode(page_table, qa, qp, c_new, k_new, cache_ckv, jnp.swapaxes(cache_kpe, 2, 3), DECODE_PAGES)
    att_smp = _ouv(o_lat.reshape(dbs, N_HEADS * KV_LORA), w_uv_flat, proj_small)

    y_prompt = _outproj(ssm_real.reshape(bsz * seq, D_SSM), att_real, w_out_b, x_real, g_fin, TM_OUTPROJ)
    y_sample = _outproj(ssm_smp, att_smp, w_out_b, x_sample.reshape(dbs, D_MODEL), g_fin, dbs)

    c_meta, k_meta = c_small[dbs:], kr_small[dbs:, :QK_ROPE]
    ckv_p = jnp.concatenate([jnp.broadcast_to(c_meta[None], (bsz, N_META, KV_LORA)),
                             c_real.reshape(bsz, seq, KV_LORA)], axis=1)
    kpe_p = jnp.concatenate([jnp.broadcast_to(k_meta[None], (bsz, N_META, QK_ROPE)),
                             kr_real[:, :QK_ROPE].reshape(bsz, seq, QK_ROPE)], axis=1)
    return (y_prompt.reshape(bsz, seq, D_MODEL),
            y_sample.reshape(dbs, 1, D_MODEL),
            ckv_p[None], kpe_p[None],
            h_fin[:bsz].reshape(1, bsz, N_GROUPS, N_STATE),
            h_fin[bsz:].reshape(1, bsz, N_GROUPS, N_STATE),
            c_small[:dbs].reshape(1, dbs, 1, KV_LORA),
            kr_small[:dbs, :QK_ROPE].reshape(1, dbs, 1, QK_ROPE),
            hr_s.reshape(1, dbs, N_GROUPS, N_STATE),
            hi_s.reshape(1, dbs, N_GROUPS, N_STATE))
```
